```python
import numpy as np
import jax
import jax.numpy as jnp
from jax import lax

D_MODEL = 2048
BATCH = 4
SEQ = 2048
DEPTH = 4
DEC_BATCH = 16
DEC_SEQ = 2048
PAST_LEN = 128

GRID_W = 64
A_HEADS = 16
A_HEAD_DIM = 64
A_WIDTH = A_HEADS * A_HEAD_DIM
A_DECAY_LORA = 64
A_ICLR_LORA = 64
A_GATE_LORA = 160
B_GROUPS = 8
B_GROUP_DIM = 64
B_WIDTH = B_GROUPS * B_GROUP_DIM
B_CHUNK = 128
C_HEADS = 8
C_HEAD_DIM = 64
C_WIDTH = C_HEADS * C_HEAD_DIM
C_WIN_ROWS = 8
C_WIN_COLS = 16
N_EXPERTS = 16
EXPERT_HIDDEN = 1024
EC_CAPACITY_FACTOR = 2
N_BRANCHES = 3

A_COLS = 3 * A_WIDTH + 2 * A_DECAY_LORA + 2 * A_ICLR_LORA + A_GATE_LORA
B_COLS = 2 * B_WIDTH
C_COLS = 3 * C_WIDTH
G_COLS = N_BRANCHES * D_MODEL
IN_COLS = A_COLS + B_COLS + C_COLS + G_COLS

DEEPNORM_ALPHA = (2 * DEPTH) ** 0.25
DEEPNORM_BETA = (8 * DEPTH) ** -0.25
LN_EPS = 1e-5
GN_EPS = 64e-5

kernel_name = "hybrid_rwkv7_gmlp_natten_ec_encoder"


def _layer_norm(x, g, b, eps=LN_EPS):
    xf = x.astype(jnp.float32)
    mu = xf.mean(-1, keepdims=True)
    var = jnp.square(xf - mu).mean(-1, keepdims=True)
    return ((xf - mu) * lax.rsqrt(var + eps) * g + b).astype(x.dtype)


def _centred_shift(p, mu_prev, mu_next):
    zero = jnp.zeros_like(p[:, :1])
    prev = jnp.concatenate([zero, p[:, :-1]], axis=1)
    nxt = jnp.concatenate([p[:, 1:], zero], axis=1)
    return p + mu_prev * (prev - p) + mu_next * (nxt - p)


def _wkv7_scan(r, w, k, v, a, b, reverse):
    bsz, _, h, n = r.shape

    def step(S, inp):
        r_t, w_t, k_t, v_t, a_t, b_t = inp
        sa = jnp.einsum('bhij,bhj->bhi', S, a_t)
        S = S * w_t[:, :, None, :] + sa[..., :, None] * b_t[:, :, None, :] + v_t[..., :, None] * k_t[:, :, None, :]
        return S, jnp.einsum('bhij,bhj->bhi', S, r_t)

    xs = tuple(jnp.swapaxes(t, 0, 1) for t in (r, w, k, v, a, b))
    s0 = jnp.zeros((bsz, h, n, n), jnp.float32)
    _, y = lax.scan(step, s0, xs, reverse=reverse)
    return jnp.swapaxes(y, 0, 1)


def _rwkv7_bidir(pa, mu_prev, mu_next, decay_w0, decay_w2, iclr_a0, iclr_a2, gate_g2, k_k, k_a, r_k, gn_g, gn_b):
    bsz, t, _ = pa.shape
    dt = pa.dtype
    f32 = jnp.float32
    pa = _centred_shift(pa, mu_prev, mu_next).astype(f32)
    splits = np.cumsum([A_WIDTH] * 3 + [A_DECAY_LORA] * 2 + [A_ICLR_LORA] * 2).tolist()
    r, k, v, wd_f, wd_b, ad_f, ad_b, gd = jnp.split(pa, splits, axis=-1)
    heads = lambda z: z.reshape(bsz, t, A_HEADS, A_HEAD_DIM)
    kk = heads(k * k_k.astype(f32))
    kk = kk / jnp.maximum(jnp.sqrt(jnp.sum(kk * kk, -1, keepdims=True)), 1e-12)
    rh, vh = heads(r), heads(v)
    y = 0.0
    k_bonus = 0.0
    for d, (wd, ad, rev) in enumerate(((wd_f, ad_f, False), (wd_b, ad_b, True))):
        w_log = -jax.nn.softplus(-(decay_w0[d].astype(f32) + jnp.tanh(wd) @ decay_w2[d].astype(f32))) - 0.5
        decay = jnp.exp(-jnp.exp(w_log))
        a = jax.nn.sigmoid(iclr_a0[d].astype(f32) + ad @ iclr_a2[d].astype(f32))
        k_d = k * (1.0 + (a - 1.0) * k_a.astype(f32))
        y = y + _wkv7_scan(rh, heads(decay), heads(k_d), vh, -kk, kk * heads(a), rev)
        k_bonus = k_bonus + k_d
    mu = y.mean(-1, keepdims=True)
    var = jnp.square(y - mu).mean(-1, keepdims=True)
    y = ((y - mu) * lax.rsqrt(var + GN_EPS)).reshape(bsz, t, A_WIDTH) * gn_g.astype(f32) + gn_b.astype(f32)
    bonus = jnp.sum(rh * heads(k_bonus) * r_k.astype(f32), -1, keepdims=True) * vh
    y = y + bonus.reshape(bsz, t, A_WIDTH)
    g = jax.nn.sigmoid(gd) @ gate_g2.astype(f32)
    return (y * g).astype(dt)


def _chunked_spatial_gating(pb, sg_ln_g, sg_ln_b, sg_w, sg_b):
    bsz, t, _ = pb.shape
    z = jax.nn.gelu(pb)
    u, v = jnp.split(z, 2, axis=-1)
    v = _layer_norm(v, sg_ln_g, sg_ln_b)
    vc = v.reshape(bsz, t // B_CHUNK, B_CHUNK, B_GROUPS, B_GROUP_DIM)
    mixed = jnp.einsum('gij,bcjgd->bcigd', sg_w, vc) + jnp.swapaxes(sg_b, 0, 1)[None, None, :, :, None]
    return u * mixed.reshape(bsz, t, B_WIDTH)


def _neighbourhood_attention(pc, rpb):
    bsz, t, _ = pc.shape
    rows = t // GRID_W
    kr = min(C_WIN_ROWS, rows)
    kc = C_WIN_COLS
    q, k, v = jnp.split(pc, 3, axis=-1)
    grid = lambda z: z.reshape(bsz, rows, GRID_W, C_HEADS, C_HEAD_DIM)
    q, k, v = grid(q), grid(k), grid(v)
    cols = np.arange(GRID_W)
    col_start = np.clip(cols - kc // 2, 0, GRID_W - kc)
    col_idx = col_start[:, None] + np.arange(kc)[None, :]
    col_off = col_idx - cols[:, None] + (C_WIN_COLS - 1)
    scale = C_HEAD_DIM ** -0.5

    def one_row(i):
        r0 = jnp.clip(i - kr // 2, 0, rows - kr)
        q_row = lax.dynamic_index_in_dim(q, i, axis=1, keepdims=False)
        k_win = lax.dynamic_slice_in_dim(k, r0, kr, axis=1)[:, :, col_idx]
        v_win = lax.dynamic_slice_in_dim(v, r0, kr, axis=1)[:, :, col_idx]
        row_off = r0 + jnp.arange(kr) - i + (C_WIN_ROWS - 1)
        bias = jnp.take(rpb, row_off, axis=1)[:, :, col_off]
        s = (jnp.einsum('bjhd,brjchd->bhjrc', q_row, k_win).astype(jnp.float32) * scale
             + jnp.transpose(bias, (0, 2, 1, 3))[None].astype(jnp.float32))
        p = jax.nn.softmax(s.reshape(bsz, C_HEADS, GRID_W, kr * kc), axis=-1).reshape(s.shape)
        return jnp.einsum('bhjrc,brjchd->bjhd', p.astype(pc.dtype), v_win)

    out = lax.map(one_row, jnp.arange(rows))
    return jnp.moveaxis(out, 0, 1).reshape(bsz, t, C_WIDTH)


def _mixer_block(x, w_in, mu_prev, mu_next, decay_w0, decay_w2, iclr_a0, iclr_a2, gate_g2, k_k, k_a, r_k,
                 gn_g, gn_b, sg_ln_g, sg_ln_b, sg_w, sg_b, rpb, p_a, p_b, p_c, w_out):
    proj = x @ w_in
    pa, pb, pc, pg = jnp.split(proj, [A_COLS, A_COLS + B_COLS, A_COLS + B_COLS + C_COLS], axis=-1)
    ya = _rwkv7_bidir(pa, mu_prev, mu_next, decay_w0, decay_w2, iclr_a0, iclr_a2, gate_g2, k_k, k_a, r_k,
                      gn_g, gn_b) @ p_a
    yb = _chunked_spatial_gating(pb, sg_ln_g, sg_ln_b, sg_w, sg_b) @ p_b
    yc = _neighbourhood_attention(pc, rpb) @ p_c
    ga, gb, gc = jnp.split(jax.nn.sigmoid(pg), N_BRANCHES, axis=-1)
    return (ga * ya + gb * yb + gc * yc) @ w_out


def _expert_choice_moe(x, w_router, e_gate, e_up, e_down):
    bsz, t, d = x.shape
    n = bsz * t
    cap = EC_CAPACITY_FACTOR * n // N_EXPERTS
    xt = x.reshape(n, d)
    aff = jax.nn.softmax((xt @ w_router).astype(jnp.float32), axis=-1)
    gate, idx = lax.top_k(aff.T, cap)
    xe = jnp.take(xt, idx, axis=0)
    h = jax.nn.silu(jnp.einsum('ecd,edf->ecf', xe, e_gate)) * jnp.einsum('ecd,edf->ecf', xe, e_up)
    ye = jnp.einsum('ecf,efd->ecd', h, e_down) * gate[..., None].astype(x.dtype)
    y = jnp.zeros_like(xt).at[idx.reshape(-1)].add(ye.reshape(-1, d))
    return y.reshape(bsz, t, d)


def _trunk(x, w_in, mu_prev, mu_next, decay_w0, decay_w2, iclr_a0, iclr_a2, gate_g2, k_k, k_a, r_k,
           gn_g, gn_b, sg_ln_g, sg_ln_b, sg_w, sg_b, rpb, p_a, p_b, p_c, w_out, ln_mix_g, ln_mix_b,
           w_router, e_gate, e_up, e_down, ln_ffn_g, ln_ffn_b):
    for l in range(DEPTH):
        h = _mixer_block(x, w_in[l], mu_prev[l], mu_next[l], decay_w0[l], decay_w2[l], iclr_a0[l], iclr_a2[l],
                         gate_g2[l], k_k[l], k_a[l], r_k[l], gn_g[l], gn_b[l], sg_ln_g[l], sg_ln_b[l], sg_w[l],
                         sg_b[l], rpb[l], p_a[l], p_b[l], p_c[l], w_out[l])
        x = _layer_norm(DEEPNORM_ALPHA * x + h, ln_mix_g[l], ln_mix_b[l])
        h = _expert_choice_moe(x, w_router[l], e_gate[l], e_up[l], e_down[l])
        x = _layer_norm(DEEPNORM_ALPHA * x + h, ln_ffn_g[l], ln_ffn_b[l])
    return x


def setup_inputs(seed: int = 0) -> dict:
    key = jax.random.key(seed)
    ks = iter(jax.random.split(key, 48))
    nrm = lambda shape, scale: jax.random.normal(next(ks), shape, jnp.float32) * scale
    uni = lambda shape, lo, hi: jax.random.uniform(next(ks), shape, jnp.float32, lo, hi)
    L, D = DEPTH, D_MODEL
    return {
        "x_prompt": nrm((BATCH, SEQ, D), 1.0),
        "x_sample": nrm((DEC_BATCH, DEC_SEQ, D), 1.0),
        "w_in": nrm((L, D, IN_COLS), D ** -0.5),
        "mu_prev": uni((L, A_COLS), 0.0, 0.5),
        "mu_next": uni((L, A_COLS), 0.0, 0.5),
        "decay_w0": uni((L, 2, A_WIDTH), -6.0, -1.0),
        "decay_w2": nrm((L, 2, A_DECAY_LORA, A_WIDTH), 0.1 * A_DECAY_LORA ** -0.5),
        "iclr_a0": nrm((L, 2, A_WIDTH), 0.1),
        "iclr_a2": nrm((L, 2, A_ICLR_LORA, A_WIDTH), 0.5 * A_ICLR_LORA ** -0.5),
        "gate_g2": nrm((L, A_GATE_LORA, A_WIDTH), A_GATE_LORA ** -0.5),
        "k_k": 0.85 + nrm((L, A_WIDTH), 0.02),
        "k_a": 1.0 + nrm((L, A_WIDTH), 0.02),
        "r_k": nrm((L, A_HEADS, A_HEAD_DIM), 0.1),
        "gn_g": 1.0 + nrm((L, A_WIDTH), 0.02),
        "gn_b": nrm((L, A_WIDTH), 0.02),
        "sg_ln_g": 1.0 + nrm((L, B_WIDTH), 0.02),
        "sg_ln_b": nrm((L, B_WIDTH), 0.02),
        "sg_w": nrm((L, B_GROUPS, B_CHUNK, B_CHUNK), B_CHUNK ** -0.5),
        "sg_b": 1.0 + nrm((L, B_GROUPS, B_CHUNK), 0.02),
        "rpb": nrm((L, C_HEADS, 2 * C_WIN_ROWS - 1, 2 * C_WIN_COLS - 1), 0.1),
        "p_a": nrm((L, A_WIDTH, D), A_WIDTH ** -0.5),
        "p_b": nrm((L, B_WIDTH, D), B_WIDTH ** -0.5),
        "p_c": nrm((L, C_WIDTH, D), C_WIDTH ** -0.5),
        "w_out": nrm((L, D, D), D ** -0.5 * DEEPNORM_BETA),
        "ln_mix_g": 1.0 + nrm((L, D), 0.02),
        "ln_mix_b": nrm((L, D), 0.02),
        "w_router": nrm((L, D, N_EXPERTS), D ** -0.5),
        "e_gate": nrm((L, N_EXPERTS, D, EXPERT_HIDDEN), D ** -0.5),
        "e_up": nrm((L, N_EXPERTS, D, EXPERT_HIDDEN), D ** -0.5),
        "e_down": nrm((L, N_EXPERTS, EXPERT_HIDDEN, D), EXPERT_HIDDEN ** -0.5 * DEEPNORM_BETA),
        "ln_ffn_g": 1.0 + nrm((L, D), 0.02),
        "ln_ffn_b": nrm((L, D), 0.02),
    }


def reference(x_prompt, x_sample, w_in, mu_prev, mu_next, decay_w0, decay_w2, iclr_a0, iclr_a2, gate_g2,
              k_k, k_a, r_k, gn_g, gn_b, sg_ln_g, sg_ln_b, sg_w, sg_b, rpb, p_a, p_b, p_c, w_out,
              ln_mix_g, ln_mix_b, w_router, e_gate, e_up, e_down, ln_ffn_g, ln_ffn_b):
    weights = (w_in, mu_prev, mu_next, decay_w0, decay_w2, iclr_a0, iclr_a2, gate_g2, k_k, k_a, r_k,
               gn_g, gn_b, sg_ln_g, sg_ln_b, sg_w, sg_b, rpb, p_a, p_b, p_c, w_out, ln_mix_g, ln_mix_b,
               w_router, e_gate, e_up, e_down, ln_ffn_g, ln_ffn_b)
    y_prompt = _trunk(x_prompt, *weights)
    y_sample = _trunk(x_sample, *weights)
    return (y_prompt, y_sample)
```

```python
import functools

import numpy as np
import jax
import jax.numpy as jnp
from jax import lax
from jax.experimental import pallas as pl
from jax.experimental.pallas import tpu as pltpu

F32 = jnp.float32
BF16 = jnp.bfloat16
I32 = jnp.int32

D_MODEL = 2048
DEPTH = 4
GRID_W = 64
A_HEADS = 16
A_HEAD_DIM = 64
A_WIDTH = A_HEADS * A_HEAD_DIM
A_DECAY_LORA = 64
A_ICLR_LORA = 64
A_GATE_LORA = 160
B_GROUPS = 8
B_GROUP_DIM = 64
B_WIDTH = B_GROUPS * B_GROUP_DIM
B_CHUNK = 128
C_HEADS = 8
C_HEAD_DIM = 64
C_WIDTH = C_HEADS * C_HEAD_DIM
C_WIN_ROWS = 8
C_WIN_COLS = 16
N_EXPERTS = 16
EXPERT_HIDDEN = 1024
EC_CAPACITY_FACTOR = 2
A_COLS = 3 * A_WIDTH + 2 * A_DECAY_LORA + 2 * A_ICLR_LORA + A_GATE_LORA
B_COLS = 2 * B_WIDTH
C_COLS = 3 * C_WIDTH
G_COLS = 3 * D_MODEL
DEEPNORM_ALPHA = (2 * DEPTH) ** 0.25
LN_EPS = 1e-5
GN_EPS = 64e-5

LORA_COLS = 2 * A_DECAY_LORA + 2 * A_ICLR_LORA + A_GATE_LORA
LORA_PAD = 512
COL_R, COL_K, COL_V = 0, A_WIDTH, 2 * A_WIDTH
COL_B = 3 * A_WIDTH
COL_G = COL_B + B_COLS
COL_C = COL_G + G_COLS
COL_L = COL_C + C_COLS
IN_PAD = COL_L + LORA_PAD

LANES = 128
SCAN_CHUNK = 64
MOE_TILE = 256
VMEM_LIMIT = 48 * 1024 * 1024


def _cparams(sem):
    return pltpu.CompilerParams(dimension_semantics=sem, vmem_limit_bytes=VMEM_LIMIT)


def _dot(a, b):
    return jnp.dot(a.astype(BF16), b.astype(BF16), preferred_element_type=F32)


def _dot_nt(a, b):
    return lax.dot_general(a.astype(BF16), b.astype(BF16), (((1,), (1,)), ((), ())),
                           preferred_element_type=F32)


def _split2(x):
    hi = x.astype(BF16)
    lo = (x - hi.astype(F32)).astype(BF16)
    return hi, lo


def _split3(x):
    hi = x.astype(BF16)
    r1 = x - hi.astype(F32)
    mid = r1.astype(BF16)
    lo = (r1 - mid.astype(F32)).astype(BF16)
    return hi, mid, lo


def _dot_exact01(m01, x):
    hi, mid, lo = _split3(x)
    m = m01.astype(BF16)
    return (jnp.dot(m, hi, preferred_element_type=F32) + jnp.dot(m, mid, preferred_element_type=F32)
            + jnp.dot(m, lo, preferred_element_type=F32))


def _dot_x01(x, m01):
    hi, mid, lo = _split3(x)
    m = m01.astype(BF16)
    return (jnp.dot(hi, m, preferred_element_type=F32) + jnp.dot(mid, m, preferred_element_type=F32)
            + jnp.dot(lo, m, preferred_element_type=F32))


def _sigmoid(x):
    return 1.0 / (1.0 + jnp.exp(-x))


def _layer_norm_rows(z, g, b):
    mu = jnp.mean(z, axis=-1, keepdims=True)
    d = z - mu
    var = jnp.mean(d * d, axis=-1, keepdims=True)
    return d * lax.rsqrt(var + LN_EPS) * g + b


def _mm_kernel(a_ref, w_ref, o_ref):
    o_ref[...] = jnp.dot(a_ref[...], w_ref[...], preferred_element_type=F32).astype(o_ref.dtype)


def _matmul(a, w, tm, tn, out_dtype=F32):
    m, k = a.shape
    n = w.shape[1]
    tm = min(tm, m)
    return pl.pallas_call(
        _mm_kernel,
        grid=(n // tn, m // tm),
        in_specs=[pl.BlockSpec((tm, k), lambda j, i: (i, 0)),
                  pl.BlockSpec((k, tn), lambda j, i: (0, j))],
        out_specs=pl.BlockSpec((tm, tn), lambda j, i: (i, j)),
        out_shape=jax.ShapeDtypeStruct((m, n), out_dtype),
        compiler_params=_cparams(("parallel", "parallel")),
        name="in_proj",
    )(a, w)


PREP_TT = 128
V_MUP_R, V_MUP_K, V_MUP_V, V_MUN_R, V_MUN_K, V_MUN_V, V_KK, V_KA, V_RK, V_W0F, V_W0B, V_A0F, V_A0B = range(13)


def _shift(cur, prv8, nxt8, mup, mun, i, nt):
    tt = cur.shape[0]
    row = lax.broadcasted_iota(I32, cur.shape, 0)
    first = jnp.where(i > 0, prv8[7:8, :], 0.0)
    last = jnp.where(i < nt - 1, nxt8[0:1, :], 0.0)
    prev = jnp.where(row == 0, first, pltpu.roll(cur, 1, 0))
    nxt = jnp.where(row == tt - 1, last, pltpu.roll(cur, tt - 1, 0))
    return cur + mup * (prev - cur) + mun * (nxt - cur)


def _prep_kernel(rc, kc, vc, lc, rp, kp, vp, lp, rn, kn, vn, ln_, vec_ref, lvec_ref, w2_ref, a2_ref, g2_ref,
                 bd_ref, r_o, v_o, kk_o, lwf_o, kf_o, bf_o, lwb_o, kb_o, bb_o, bonus_o, g_o):
    i = pl.program_id(1)
    nt = pl.num_programs(1)
    vec = vec_ref[...]
    row = lambda j: vec[j:j + 1, :]
    r = _shift(rc[0], rp[0], rn[0], row(V_MUP_R), row(V_MUN_R), i, nt)
    k = _shift(kc[0], kp[0], kn[0], row(V_MUP_K), row(V_MUN_K), i, nt)
    v = _shift(vc[0], vp[0], vn[0], row(V_MUP_V), row(V_MUN_V), i, nt)
    lo = _shift(lc[0], lp[0], ln_[0], lvec_ref[0:1, :], lvec_ref[1:2, :], i, nt)

    bd = bd_ref[...]
    kk = k * row(V_KK)
    kk = kk / jnp.maximum(jnp.sqrt(_dot_x01(kk * kk, bd)), 1e-12)

    tanh_lo = jnp.tanh(lo)
    ksum = jnp.zeros_like(k)
    outs = ((lwf_o, kf_o, bf_o), (lwb_o, kb_o, bb_o))
    for d in range(2):
        z = row(V_W0F + d) + _dot(tanh_lo, w2_ref[d])
        nz = -z
        softplus = jnp.maximum(nz, 0.0) + jnp.log(1.0 + jnp.exp(-jnp.abs(nz)))
        w_log = -softplus - 0.5
        lw = -jnp.exp(w_log)
        a = _sigmoid(row(V_A0F + d) + _dot(lo, a2_ref[d]))
        k_d = k * (1.0 + (a - 1.0) * row(V_KA))
        ksum = ksum + k_d
        lw_o, kd_o, bd_o = outs[d]
        lw_o[0] = lw
        kd_o[0] = k_d
        bd_o[0] = kk * a
    r_o[0] = r
    v_o[0] = v
    kk_o[0] = kk
    bonus_o[0] = _dot_x01(r * ksum * row(V_RK), bd) * v
    g_o[0] = _dot(_sigmoid(lo), g2_ref[...])


def _rwkv_prep(proj3, vec, lvec, w2f, a2f, g2f, bdiag):
    b, t, _ = proj3.shape
    tt = PREP_TT
    nt = t // tt
    h8 = tt // 8
    cur = lambda cb, w: pl.BlockSpec((1, tt, w), lambda bi, i: (bi, i, cb))
    prv = lambda cb, w: pl.BlockSpec((1, 8, w), lambda bi, i: (bi, jnp.maximum(i * h8 - 1, 0), cb))
    nxt = lambda cb, w: pl.BlockSpec((1, 8, w), lambda bi, i: (bi, jnp.minimum((i + 1) * h8, t // 8 - 1), cb))
    cols = [(COL_R // A_WIDTH, A_WIDTH), (COL_K // A_WIDTH, A_WIDTH), (COL_V // A_WIDTH, A_WIDTH),
            (COL_L // LORA_PAD, LORA_PAD)]
    full = lambda shape: pl.BlockSpec(shape, lambda bi, i: (0,) * len(shape))
    in_specs = ([cur(*c) for c in cols] + [prv(*c) for c in cols] + [nxt(*c) for c in cols]
                + [full(vec.shape), full(lvec.shape), full(w2f.shape), full(a2f.shape), full(g2f.shape),
                   full(bdiag.shape)])
    out_spec = pl.BlockSpec((1, tt, A_WIDTH), lambda bi, i: (bi, i, 0))
    out_sds = jax.ShapeDtypeStruct((b, t, A_WIDTH), F32)
    return pl.pallas_call(
        _prep_kernel,
        grid=(b, nt),
        in_specs=in_specs,
        out_specs=[out_spec] * 11,
        out_shape=[out_sds] * 11,
        compiler_params=_cparams(("parallel", "parallel")),
        name="rwkv_prep",
    )(*([proj3] * 12), vec, lvec, w2f, a2f, g2f, bdiag)


def _stack2(x, m0):
    return jnp.concatenate([jnp.where(m0, x, 0.0), jnp.where(m0, 0.0, x)], axis=0)


def _chunk_direction(r, v, kk, lw, kd, bd, s_ref, reverse):
    c = r.shape[0]
    ti = lax.broadcasted_iota(I32, (c, c), 0)
    si = lax.broadcasted_iota(I32, (c, c), 1)
    tri = jnp.where((si >= ti) if reverse else (si <= ti), 1.0, 0.0)
    cum = _dot_exact01(tri, lw)
    cum_ex = cum - lw
    tot = cum[0:1, :] if reverse else cum[c - 1:c, :]
    at = -kk * jnp.exp(cum_ex)
    rt = r * jnp.exp(cum)
    einv = jnp.exp(-cum)
    bt = bd * einv
    kt = kd * einv
    etail = jnp.exp(tot - cum)
    bw = bd * etail
    kw = kd * etail

    lane = lax.broadcasted_iota(I32, (c, LANES), 1)
    m0 = lane < A_HEAD_DIM
    t2 = lax.broadcasted_iota(I32, (c, 2 * c), 0)
    s2 = lax.broadcasted_iota(I32, (c, 2 * c), 1)
    s2 = jnp.where(s2 >= c, s2 - c, s2)
    strict = (s2 > t2) if reverse else (s2 < t2)
    incl = (s2 >= t2) if reverse else (s2 <= t2)
    half = lax.broadcasted_iota(I32, (c, 2 * c), 1) < c

    ar = jnp.concatenate([at, rt], axis=0)
    bk2 = jnp.concatenate([_stack2(bt, m0), _stack2(kt, m0)], axis=0)
    m = _dot_nt(ar, bk2)
    a_ab = jnp.where(strict, m[:c, :2 * c], 0.0)
    a_ak = jnp.where(strict, m[:c, 2 * c:], 0.0)
    a_rb = jnp.where(incl, m[c:, :2 * c], 0.0)
    a_rk = jnp.where(incl, m[c:, 2 * c:], 0.0)

    eye2 = jnp.where(s2 == t2, 1.0, 0.0)
    p = a_ab
    tm = eye2 + p
    steps = int(np.log2(c)) - 1
    for _ in range(steps):
        p = _dot(p, _stack2(p, half))
        tm = tm + _dot(tm, _stack2(p, half))

    s0 = s_ref[...]
    ars = _dot_nt(ar, s0)
    v2 = _stack2(v, m0)
    rhs = ars[:c] + _dot(a_ak, v2)
    u = _dot(tm, _stack2(rhs, m0))
    y = ars[c:] + _dot(jnp.concatenate([a_rb, a_rk], axis=1),
                       jnp.concatenate([_stack2(u, m0), v2], axis=0))
    uv = jnp.concatenate([u, v], axis=0)
    bkw = jnp.concatenate([bw, kw], axis=0)
    upd = _dot(uv.T, bkw)
    vi = lax.broadcasted_iota(I32, (LANES, LANES), 0)
    ki = lax.broadcasted_iota(I32, (LANES, LANES), 1)
    same_head = (vi < A_HEAD_DIM) == (ki < A_HEAD_DIM)
    s_ref[...] = s0 * jnp.exp(tot) + jnp.where(same_head, upd, 0.0)
    return y


def _scan_kernel(rf, vf, kkf, lwf, kf, bf, rb, vb, kkb, lwb, kb, bb, yf_o, yb_o, sf_ref, sb_ref):
    @pl.when(pl.program_id(2) == 0)
    def _():
        sf_ref[...] = jnp.zeros_like(sf_ref)
        sb_ref[...] = jnp.zeros_like(sb_ref)

    yf_o[0] = _chunk_direction(rf[0], vf[0], kkf[0], lwf[0], kf[0], bf[0], sf_ref, False)
    yb_o[0] = _chunk_direction(rb[0], vb[0], kkb[0], lwb[0], kb[0], bb[0], sb_ref, True)


def _rwkv_scan(r, v, kk, lwf, kf, bf, lwb, kb, bb):
    b, t, _ = r.shape
    c = SCAN_CHUNK
    nc = t // c
    fwd = pl.BlockSpec((1, c, LANES), lambda bi, hp, ci: (bi, ci, hp))
    bwd = pl.BlockSpec((1, c, LANES), lambda bi, hp, ci: (bi, nc - 1 - ci, hp))
    out_sds = jax.ShapeDtypeStruct((b, t, A_WIDTH), F32)
    return pl.pallas_call(
        _scan_kernel,
        grid=(b, A_WIDTH // LANES, nc),
        in_specs=[fwd] * 6 + [bwd] * 6,
        out_specs=[fwd, bwd],
        out_shape=[out_sds, out_sds],
        scratch_shapes=[pltpu.VMEM((LANES, LANES), F32), pltpu.VMEM((LANES, LANES), F32)],
        compiler_params=_cparams(("parallel", "parallel", "arbitrary")),
        name="rwkv_scan",
    )(r, v, kk, lwf, kf, bf, r, v, kk, lwb, kb, bb)


POST_TT = 256


def _post_kernel(yf, yb, bonus, g, vec_ref, bd_ref, o_ref):
    y = yf[...] + yb[...]
    bd = bd_ref[...]
    inv = 1.0 / A_HEAD_DIM
    mu = _dot_x01(y, bd) * inv
    d = y - mu
    var = _dot_x01(d * d, bd) * inv
    yn = d * lax.rsqrt(var + GN_EPS) * vec_ref[0:1, :] + vec_ref[1:2, :]
    o_ref[...] = ((yn + bonus[...]) * g[...]).astype(o_ref.dtype)


def _rwkv_post(yf, yb, bonus, g, gnvec, bdiag):
    n = yf.shape[0]
    tt = min(POST_TT, n)
    blk = pl.BlockSpec((tt, A_WIDTH), lambda i: (i, 0))
    full = lambda a: pl.BlockSpec(a.shape, lambda i: (0,) * a.ndim)
    return pl.pallas_call(
        _post_kernel,
        grid=(n // tt,),
        in_specs=[blk, blk, blk, blk, full(gnvec), full(bdiag)],
        out_specs=blk,
        out_shape=jax.ShapeDtypeStruct((n, A_WIDTH), BF16),
        compiler_params=_cparams(("parallel",)),
        name="rwkv_post",
    )(yf, yb, bonus, g, gnvec, bdiag)


def _gelu_tanh(x):
    return 0.5 * x * (1.0 + jnp.tanh(np.sqrt(2.0 / np.pi).astype(np.float32) * (x + 0.044715 * (x * x * x))))


def _sg_kernel(pb_ref, lnv_ref, w_ref, bias_ref, o_ref):
    z = _gelu_tanh(pb_ref[0])
    u = z[:, :B_WIDTH]
    v = _layer_norm_rows(z[:, B_WIDTH:], lnv_ref[0:1, :], lnv_ref[1:2, :])
    lane = lax.broadcasted_iota(I32, (B_CHUNK, LANES), 1)
    m0 = lane < B_GROUP_DIM
    parts = []
    for q in range(B_WIDTH // LANES):
        vq = v[:, q * LANES:(q + 1) * LANES]
        parts.append(_dot(w_ref[q], _stack2(vq, m0)))
    mixed = jnp.concatenate(parts, axis=1) + bias_ref[...]
    o_ref[0] = (u * mixed).astype(o_ref.dtype)


def _spatial_gating(proj3, lnvec, w2, bias_full):
    b, t, _ = proj3.shape
    full = lambda a: pl.BlockSpec(a.shape, lambda bi, i: (0,) * a.ndim)
    return pl.pallas_call(
        _sg_kernel,
        grid=(b, t // B_CHUNK),
        in_specs=[pl.BlockSpec((1, B_CHUNK, B_COLS), lambda bi, i: (bi, i, COL_B // B_COLS)),
                  full(lnvec), full(w2), full(bias_full)],
        out_specs=pl.BlockSpec((1, B_CHUNK, B_WIDTH), lambda bi, i: (bi, i, 0)),
        out_shape=jax.ShapeDtypeStruct((b, t, B_WIDTH), BF16),
        compiler_params=_cparams(("parallel", "parallel")),
        name="spatial_gating",
    )(proj3, lnvec, w2, bias_full)


NA_WIN = C_WIN_ROWS * GRID_W


def _na_row_start(i, rows):
    return jnp.clip(i - C_WIN_ROWS // 2, 0, rows - C_WIN_ROWS)


def _na_kernel(q_ref, k_ref, v_ref, bias_ref, o_ref, *, rows):
    i = pl.program_id(1)
    start = pl.multiple_of(_na_row_start(i, rows) * GRID_W, GRID_W)
    q = q_ref[0]
    kwin = k_ref[0, pl.ds(start, NA_WIN), :]
    vwin = v_ref[0, pl.ds(start, NA_WIN), :]
    scale = C_HEAD_DIM ** -0.5
    outs = []
    for h in range(C_HEADS):
        sl = slice(h * C_HEAD_DIM, (h + 1) * C_HEAD_DIM)
        s = _dot_nt(q[:, sl], kwin[:, sl]) * scale + bias_ref[0, h]
        s = s - jnp.max(s, axis=-1, keepdims=True)
        e = jnp.exp(s)
        p = e / jnp.sum(e, axis=-1, keepdims=True)
        outs.append(_dot(p, vwin[:, sl]))
    o_ref[0] = jnp.concatenate(outs, axis=1).astype(o_ref.dtype)


def _neighbourhood_attention(proj3, bias_tab):
    b, t, _ = proj3.shape
    rows = t // GRID_W
    cq = COL_C // C_WIDTH
    seq = lambda cb: pl.BlockSpec((1, t, C_WIDTH), lambda bi, i: (bi, 0, cb))
    return pl.pallas_call(
        functools.partial(_na_kernel, rows=rows),
        grid=(b, rows),
        in_specs=[pl.BlockSpec((1, GRID_W, C_WIDTH), lambda bi, i: (bi, i, cq)), seq(cq + 1), seq(cq + 2),
                  pl.BlockSpec((1, C_HEADS, GRID_W, NA_WIN),
                               lambda bi, i: (i - _na_row_start(i, rows), 0, 0, 0))],
        out_specs=pl.BlockSpec((1, GRID_W, C_WIDTH), lambda bi, i: (bi, i, 0)),
        out_shape=jax.ShapeDtypeStruct((b, t, C_WIDTH), BF16),
        compiler_params=_cparams(("parallel", "arbitrary")),
        name="nbr_attention",
    )(proj3, proj3, proj3, bias_tab)


def _na_bias_table(rpb, rows):
    kc = C_WIN_COLS
    cols = np.arange(GRID_W)
    col_start = np.clip(cols - kc // 2, 0, GRID_W - kc)
    key_col = np.arange(GRID_W)
    in_win = (key_col[None, :] >= col_start[:, None]) & (key_col[None, :] < col_start[:, None] + kc)
    col_off = np.clip(key_col[None, :] - cols[:, None] + (C_WIN_COLS - 1), 0, 2 * C_WIN_COLS - 2)
    tabs = []
    for delta in range(C_WIN_ROWS):
        row_off = np.arange(C_WIN_ROWS) - delta + (C_WIN_ROWS - 1)
        bias = rpb[:, row_off][:, :, col_off]
        bias = jnp.where(in_win[None, None], bias, -1e30)
        bias = jnp.transpose(bias, (0, 2, 1, 3)).reshape(C_HEADS, GRID_W, NA_WIN)
        tabs.append(bias)
    return jnp.stack(tabs).astype(F32)


MERGE_TM = 256


def _merge_kernel(ya, yb, yc, ga, gb, gc, pa, pb, pc, o_ref):
    m = _sigmoid(ga[...]) * jnp.dot(ya[...], pa[...], preferred_element_type=F32)
    m = m + _sigmoid(gb[...]) * jnp.dot(yb[...], pb[...], preferred_element_type=F32)
    m = m + _sigmoid(gc[...]) * jnp.dot(yc[...], pc[...], preferred_element_type=F32)
    o_ref[...] = m.astype(o_ref.dtype)


def _merge(ya, yb, yc, proj, p_a, p_b, p_c):
    n = ya.shape[0]
    tm = min(MERGE_TM, n)
    rowblk = lambda w: pl.BlockSpec((tm, w), lambda i: (i, 0))
    gate = lambda j: pl.BlockSpec((tm, D_MODEL), lambda i: (i, COL_G // D_MODEL + j))
    full = lambda a: pl.BlockSpec(a.shape, lambda i: (0,) * a.ndim)
    return pl.pallas_call(
        _merge_kernel,
        grid=(n // tm,),
        in_specs=[rowblk(A_WIDTH), rowblk(B_WIDTH), rowblk(C_WIDTH), gate(0), gate(1), gate(2),
                  full(p_a), full(p_b), full(p_c)],
        out_specs=rowblk(D_MODEL),
        out_shape=jax.ShapeDtypeStruct((n, D_MODEL), BF16),
        compiler_params=_cparams(("parallel",)),
        name="branch_merge",
    )(ya, yb, yc, proj, proj, proj, p_a, p_b, p_c)


OUT_TM = 256


def _outln_kernel(m_ref, w_ref, x_ref, lnv_ref, o_ref, ob_ref):
    h = jnp.dot(m_ref[...], w_ref[...], preferred_element_type=F32)
    y = _layer_norm_rows(DEEPNORM_ALPHA * x_ref[...] + h, lnv_ref[0:1, :], lnv_ref[1:2, :])
    o_ref[...] = y
    ob_ref[...] = y.astype(BF16)


def _out_proj_ln(m, w_out, x, lnvec):
    n = m.shape[0]
    tm = min(OUT_TM, n)
    blk = pl.BlockSpec((tm, D_MODEL), lambda i: (i, 0))
    full = lambda a: pl.BlockSpec(a.shape, lambda i: (0,) * a.ndim)
    return pl.pallas_call(
        _outln_kernel,
        grid=(n // tm,),
        in_specs=[blk, full(w_out), blk, full(lnvec)],
        out_specs=[blk, blk],
        out_shape=[jax.ShapeDtypeStruct((n, D_MODEL), F32), jax.ShapeDtypeStruct((n, D_MODEL), BF16)],
        compiler_params=_cparams(("parallel",)),
        name="out_proj_ln",
    )(m, w_out, x, lnvec)


ROUTER_TM = 512


def _router_kernel(x_ref, wt_ref, o_ref):
    xh, xl = _split2(x_ref[...])
    wh, wl = _split2(wt_ref[...])
    nt = lambda a, b: lax.dot_general(a, b, (((1,), (1,)), ((), ())), preferred_element_type=F32)
    logits = nt(wh, xh) + nt(wl, xh) + nt(wh, xl)
    logits = logits - jnp.max(logits, axis=0, keepdims=True)
    e = jnp.exp(logits)
    o_ref[...] = e / jnp.sum(e, axis=0, keepdims=True)


def _router(x, w_router_t):
    n = x.shape[0]
    tm = min(ROUTER_TM, n)
    return pl.pallas_call(
        _router_kernel,
        grid=(n // tm,),
        in_specs=[pl.BlockSpec((tm, D_MODEL), lambda i: (i, 0)),
                  pl.BlockSpec(w_router_t.shape, lambda i: (0, 0))],
        out_specs=pl.BlockSpec((N_EXPERTS, tm), lambda i: (0, i)),
        out_shape=jax.ShapeDtypeStruct((N_EXPERTS, n), F32),
        compiler_params=_cparams(("parallel",)),
        name="router",
    )(x, w_router_t)


def _select_kernel(aff_ref, pos_ref, off_ref, *, cap):
    aff = aff_ref[...]
    e_, g_, l_ = aff.shape
    n = g_ * l_
    bits = pltpu.bitcast(aff, I32)

    def count(mask):
        c = jnp.sum(jnp.where(mask, 1.0, 0.0), axis=2, keepdims=True)
        return jnp.sum(c, axis=1, keepdims=True)

    def thr_body(_, carry):
        lo, hi = carry
        mid = lo + (hi - lo + 1) // 2
        ok = count(bits >= mid) >= cap
        return jnp.where(ok, mid, lo), jnp.where(ok, hi, mid - 1)

    lo0 = jnp.zeros((e_, 1, 1), I32)
    hi0 = jnp.full((e_, 1, 1), 0x7F800000, I32)
    thr, _ = lax.fori_loop(0, 32, thr_body, (lo0, hi0))
    gt = bits > thr
    tie = bits == thr
    need = cap - count(gt)
    idx = lax.broadcasted_iota(I32, aff.shape, 1) * l_ + lax.broadcasted_iota(I32, aff.shape, 2)

    def idx_body(_, carry):
        lo, hi = carry
        mid = (lo + hi) // 2
        ok = count(tie & (idx <= mid)) >= need
        return jnp.where(ok, lo, mid + 1), jnp.where(ok, mid, hi)

    lo1 = jnp.zeros((e_, 1, 1), I32)
    hi1 = jnp.full((e_, 1, 1), n - 1, I32)
    cut, _ = lax.fori_loop(0, int(np.ceil(np.log2(n))) + 1, idx_body, (lo1, hi1))
    sel = jnp.where(gt | (tie & (idx <= cut)), 1.0, 0.0)

    sel2 = sel.reshape(e_ * g_, l_)
    ls = lax.broadcasted_iota(I32, (l_, l_), 0)
    lt = lax.broadcasted_iota(I32, (l_, l_), 1)
    incl = jnp.dot(sel2.astype(BF16), jnp.where(ls <= lt, 1.0, 0.0).astype(BF16), preferred_element_type=F32)
    tot = jnp.dot(sel2.astype(BF16), jnp.ones((l_, l_), BF16), preferred_element_type=F32)
    gs = lax.broadcasted_iota(I32, (g_, g_), 0)
    gt_ = lax.broadcasted_iota(I32, (g_, g_), 1)
    lower = jnp.where(gt_ < gs, 1.0, 0.0).astype(BF16)
    offs = [jnp.dot(lower, tot[e * g_:(e + 1) * g_].astype(BF16), preferred_element_type=F32)
            for e in range(e_)]
    off = jnp.concatenate(offs, axis=0)
    pos = incl + off - sel2
    pos_ref[...] = jnp.where(sel2 > 0.5, pos, -1.0).astype(I32).reshape(e_, g_, l_)
    off_ref[...] = off.astype(I32).reshape(e_, g_, l_)


def _select(aff3, cap):
    full = pl.BlockSpec(aff3.shape, lambda i: (0, 0, 0))
    sds = jax.ShapeDtypeStruct(aff3.shape, I32)
    return pl.pallas_call(
        functools.partial(_select_kernel, cap=cap),
        grid=(1,),
        in_specs=[full],
        out_specs=[full, full],
        out_shape=[sds, sds],
        compiler_params=_cparams(("arbitrary",)),
        name="expert_select",
    )(aff3)


def _gather_kernel(e_s, slab_s, tile_s, valid_s, first_s, x_ref, pos_ref, o_ref):
    it = pl.program_id(0)

    @pl.when(first_s[it] == 1)
    def _():
        o_ref[...] = jnp.zeros_like(o_ref)

    @pl.when(valid_s[it] == 1)
    def _():
        t = MOE_TILE
        want = lax.broadcasted_iota(I32, (t, t), 0) + slab_s[it] * t
        onehot = jnp.where(pos_ref[0] == want, 1.0, 0.0).astype(BF16)
        got = jnp.dot(onehot, x_ref[...], preferred_element_type=F32)
        o_ref[0] = (o_ref[0].astype(F32) + got).astype(o_ref.dtype)


def _moe_gather(x_bf, pos_en, items, cap):
    n = x_bf.shape[0]
    t = MOE_TILE
    ni = items[0].shape[0]
    grid_spec = pltpu.PrefetchScalarGridSpec(
        num_scalar_prefetch=5,
        grid=(ni,),
        in_specs=[pl.BlockSpec((t, D_MODEL), lambda it, e, s, tl, va, fi: (tl[it], 0)),
                  pl.BlockSpec((1, 1, t), lambda it, e, s, tl, va, fi: (e[it], 0, tl[it]))],
        out_specs=pl.BlockSpec((1, t, D_MODEL), lambda it, e, s, tl, va, fi: (e[it], s[it], 0)),
    )
    return pl.pallas_call(
        _gather_kernel,
        grid_spec=grid_spec,
        out_shape=jax.ShapeDtypeStruct((N_EXPERTS, cap, D_MODEL), BF16),
        compiler_params=_cparams(("arbitrary",)),
        name="moe_gather",
    )(*items, x_bf, pos_en.reshape(N_EXPERTS, 1, n))


FFN_TM = 512


def _ffn_kernel(x_ref, wg_ref, wu_ref, wd_ref, o_ref):
    x = x_ref[0]
    g = jnp.dot(x, wg_ref[0], preferred_element_type=F32)
    u = jnp.dot(x, wu_ref[0], preferred_element_type=F32)
    h = (g * _sigmoid(g)) * u
    o_ref[0] = jnp.dot(h.astype(BF16), wd_ref[0], preferred_element_type=F32)


def _expert_ffn(xe, wg, wu, wd):
    e_, cap, _ = xe.shape
    tm = min(FFN_TM, cap)
    wspec = lambda a: pl.BlockSpec((1,) + a.shape[1:], lambda e, i: (e, 0, 0))
    return pl.pallas_call(
        _ffn_kernel,
        grid=(e_, cap // tm),
        in_specs=[pl.BlockSpec((1, tm, D_MODEL), lambda e, i: (e, i, 0)), wspec(wg), wspec(wu), wspec(wd)],
        out_specs=pl.BlockSpec((1, tm, D_MODEL), lambda e, i: (e, i, 0)),
        out_shape=jax.ShapeDtypeStruct((e_, cap, D_MODEL), F32),
        compiler_params=_cparams(("parallel", "arbitrary")),
        name="expert_ffn",
    )(xe, wg, wu, wd)


def _combine_kernel(e_s, slab_s, tile_s, valid_s, first_s, last_s, ye_ref, pos_ref, aff_ref, x_ref, lnv_ref,
                    o_ref, ob_ref, acc_ref):
    it = pl.program_id(0)

    @pl.when(first_s[it] == 1)
    def _():
        acc_ref[...] = jnp.zeros_like(acc_ref)

    @pl.when(valid_s[it] == 1)
    def _():
        t = MOE_TILE
        e = e_s[it]
        mine = lax.broadcasted_iota(I32, (t, N_EXPERTS), 1) == e
        pos = jnp.sum(jnp.where(mine, pos_ref[...].astype(F32), 0.0), axis=1, keepdims=True)
        gate = jnp.sum(jnp.where(mine, aff_ref[...], 0.0), axis=1, keepdims=True)
        want = (lax.broadcasted_iota(I32, (t, t), 1) + slab_s[it] * t).astype(F32)
        onehot = jnp.where(pos == want, 1.0, 0.0).astype(BF16)
        hi, lo = _split2(ye_ref[0])
        got = jnp.dot(onehot, hi, preferred_element_type=F32) + jnp.dot(onehot, lo, preferred_element_type=F32)
        acc_ref[...] += gate * got

    @pl.when(last_s[it] == 1)
    def _():
        y = _layer_norm_rows(DEEPNORM_ALPHA * x_ref[...] + acc_ref[...], lnv_ref[0:1, :], lnv_ref[1:2, :])
        o_ref[...] = y
        ob_ref[...] = y.astype(BF16)


def _moe_combine(ye, pos_ne, aff_ne, x, lnvec, items):
    n = x.shape[0]
    t = MOE_TILE
    ni = items[0].shape[0]
    tile_blk = lambda w: pl.BlockSpec((t, w), lambda it, e, s, tl, va, fi, la: (tl[it], 0))
    grid_spec = pltpu.PrefetchScalarGridSpec(
        num_scalar_prefetch=6,
        grid=(ni,),
        in_specs=[pl.BlockSpec((1, t, D_MODEL), lambda it, e, s, tl, va, fi, la: (e[it], s[it], 0)),
                  tile_blk(N_EXPERTS), tile_blk(N_EXPERTS), tile_blk(D_MODEL),
                  pl.BlockSpec(lnvec.shape, lambda it, e, s, tl, va, fi, la: (0, 0))],
        out_specs=[tile_blk(D_MODEL), tile_blk(D_MODEL)],
        scratch_shapes=[pltpu.VMEM((t, D_MODEL), F32)],
    )
    return pl.pallas_call(
        _combine_kernel,
        grid_spec=grid_spec,
        out_shape=[jax.ShapeDtypeStruct((n, D_MODEL), F32), jax.ShapeDtypeStruct((n, D_MODEL), BF16)],
        compiler_params=_cparams(("arbitrary",)),
        name="moe_combine",
    )(*items, ye, pos_ne, aff_ne, x, lnvec)


def _moe_items(group_off, n, cap):
    t = MOE_TILE
    nt, ns = n // t, cap // t
    e_ = N_EXPERTS
    starts = group_off[:, ::t // LANES, 0]
    ends = jnp.concatenate([starts[:, 1:], jnp.full((e_, 1), cap, I32)], axis=1)
    e_p = jnp.broadcast_to(jnp.arange(e_, dtype=I32)[:, None], (e_, nt))
    tile_p = jnp.broadcast_to(jnp.arange(nt, dtype=I32)[None, :], (e_, nt))
    slab_p = jnp.minimum(starts // t, ns - 1)
    valid_p = (ends > starts).astype(I32)
    bound = jnp.arange(ns, dtype=I32) * t
    tile_s = jax.vmap(lambda s: jnp.searchsorted(s, bound, side="right"))(starts).astype(I32) - 1
    tile_s = jnp.clip(tile_s, 0, nt - 1)
    st_s = jnp.take_along_axis(starts, tile_s, axis=1)
    en_s = jnp.take_along_axis(ends, tile_s, axis=1)
    valid_s = ((st_s < bound[None, :]) & (bound[None, :] < en_s)).astype(I32)
    e_s = jnp.broadcast_to(jnp.arange(e_, dtype=I32)[:, None], (e_, ns))
    slab_s = jnp.broadcast_to(jnp.arange(ns, dtype=I32)[None, :], (e_, ns))

    cat = lambda a, b: jnp.concatenate([a.reshape(-1), b.reshape(-1)])
    e_a, slab_a, tile_a, valid_a = cat(e_p, e_s), cat(slab_p, slab_s), cat(tile_p, tile_s), cat(valid_p, valid_s)
    secondary = cat(jnp.zeros_like(e_p), jnp.ones_like(e_s))
    ni = e_a.shape[0]
    big = jnp.int32(2 ** 30)

    def ordered(key, keep):
        key = jnp.where(keep == 1, key, big)
        order = jnp.argsort(key)
        nkeep = jnp.sum(keep)
        src = jnp.where(jnp.arange(ni) < nkeep, order, order[jnp.maximum(nkeep - 1, 0)])
        live = (jnp.arange(ni) < nkeep).astype(I32)
        return [a[src] for a in (e_a, slab_a, tile_a)] + [valid_a[src] * live]

    ge, gs, gtl, gv = ordered((e_a * ns + slab_a) * nt + tile_a, valid_a)
    blk = ge * ns + gs
    gfirst = jnp.concatenate([jnp.ones((1,), I32), (blk[1:] != blk[:-1]).astype(I32)])
    keep_c = jnp.maximum(valid_a, 1 - secondary)
    ce, cs, ctl, cv = ordered((tile_a * e_ + e_a) * 2 + secondary, keep_c)
    cfirst = jnp.concatenate([jnp.ones((1,), I32), (ctl[1:] != ctl[:-1]).astype(I32)])
    clast = jnp.concatenate([(ctl[1:] != ctl[:-1]).astype(I32), jnp.ones((1,), I32)])
    return (ge, gs, gtl, gv, gfirst), (ce, cs, ctl, cv, cfirst, clast)


def _expert_choice_moe_ln(x, x_bf, w_router_t, wg, wu, wd, lnvec):
    n = x.shape[0]
    cap = EC_CAPACITY_FACTOR * n // N_EXPERTS
    aff_en = _router(x, w_router_t)
    pos3, off3 = _select(aff_en.reshape(N_EXPERTS, n // LANES, LANES), cap)
    pos_en = pos3.reshape(N_EXPERTS, n)
    g_items, c_items = _moe_items(off3, n, cap)
    xe = _moe_gather(x_bf, pos_en, g_items, cap)
    ye = _expert_ffn(xe, wg, wu, wd)
    return _moe_combine(ye, pos_en.T, aff_en.T, x, lnvec, c_items)


def _pack_cols(w):
    a_end = A_COLS
    b_end = a_end + B_COLS
    c_end = b_end + C_COLS
    pad = jnp.zeros(w.shape[:-1] + (LORA_PAD - LORA_COLS,), w.dtype)
    return jnp.concatenate([w[..., :3 * A_WIDTH], w[..., a_end:b_end], w[..., c_end:], w[..., b_end:c_end],
                            w[..., 3 * A_WIDTH:a_end], pad], axis=-1)


def _lora_rows(w, start):
    k = w.shape[-2]
    return jnp.pad(w, [(0, 0)] * (w.ndim - 2) + [(start, LORA_PAD - start - k), (0, 0)])


def _prepare(w_in, mu_prev, mu_next, decay_w0, decay_w2, iclr_a0, iclr_a2, gate_g2, k_k, k_a, r_k, gn_g, gn_b,
             sg_ln_g, sg_ln_b, sg_w, sg_b, rpb, p_a, p_b, p_c, w_out, ln_mix_g, ln_mix_b, w_router, e_gate, e_up,
             e_down, ln_ffn_g, ln_ffn_b, rows):
    l_ = w_in.shape[0]
    pad_a = lambda m: jnp.pad(m, ((0, 0), (0, LORA_PAD - LORA_COLS)))
    mup, mun = mu_prev, mu_next
    vec_rows = [mup[:, :A_WIDTH], mup[:, A_WIDTH:2 * A_WIDTH], mup[:, 2 * A_WIDTH:3 * A_WIDTH],
                mun[:, :A_WIDTH], mun[:, A_WIDTH:2 * A_WIDTH], mun[:, 2 * A_WIDTH:3 * A_WIDTH],
                k_k, k_a, r_k.reshape(l_, A_WIDTH), decay_w0[:, 0], decay_w0[:, 1], iclr_a0[:, 0], iclr_a0[:, 1]]
    vec = jnp.stack(vec_rows + [jnp.zeros_like(k_k)] * (16 - len(vec_rows)), axis=1)
    lvec = jnp.stack([pad_a(mup[:, 3 * A_WIDTH:]), pad_a(mun[:, 3 * A_WIDTH:])]
                     + [jnp.zeros((l_, LORA_PAD), F32)] * 6, axis=1)
    w2f = jnp.stack([_lora_rows(decay_w2[:, 0], 0), _lora_rows(decay_w2[:, 1], A_DECAY_LORA)], axis=1)
    a2f = jnp.stack([_lora_rows(iclr_a2[:, 0], 2 * A_DECAY_LORA),
                     _lora_rows(iclr_a2[:, 1], 2 * A_DECAY_LORA + A_ICLR_LORA)], axis=1)
    g2f = _lora_rows(gate_g2, 2 * A_DECAY_LORA + 2 * A_ICLR_LORA)
    head = np.arange(A_WIDTH) // A_HEAD_DIM
    bdiag = jnp.asarray(head[:, None] == head[None, :], BF16)
    sgw2 = sg_w.reshape(l_, B_GROUPS // 2, 2, B_CHUNK, B_CHUNK).transpose(0, 1, 3, 2, 4)
    sgw2 = sgw2.reshape(l_, B_GROUPS // 2, B_CHUNK, 2 * B_CHUNK)
    sg_bias = jnp.repeat(jnp.swapaxes(sg_b, 1, 2), B_GROUP_DIM, axis=2)
    return dict(
        w_in=_pack_cols(w_in).astype(BF16), vec=vec, lvec=lvec, w2f=w2f.astype(BF16), a2f=a2f.astype(BF16),
        g2f=g2f.astype(BF16), bdiag=bdiag, gnvec=jnp.stack([gn_g, gn_b], axis=1),
        sg_ln=jnp.stack([sg_ln_g, sg_ln_b], axis=1), sgw2=sgw2.astype(BF16), sg_bias=sg_bias,
        na_bias=jnp.stack([_na_bias_table(rpb[l], rows) for l in range(l_)]),
        p_a=p_a.astype(BF16), p_b=p_b.astype(BF16), p_c=p_c.astype(BF16), w_out=w_out.astype(BF16),
        ln_mix=jnp.stack([ln_mix_g, ln_mix_b], axis=1), w_router_t=jnp.swapaxes(w_router, 1, 2),
        e_gate=e_gate.astype(BF16), e_up=e_up.astype(BF16), e_down=e_down.astype(BF16),
        ln_ffn=jnp.stack([ln_ffn_g, ln_ffn_b], axis=1))


def _mixer(x, x_bf, p, l, b, t):
    n = b * t
    proj = _matmul(x_bf, p["w_in"][l], 512, 1024)
    proj3 = proj.reshape(b, t, IN_PAD)
    r, v, kk, lwf, kf, bf, lwb, kb, bb, bonus, g = _rwkv_prep(
        proj3, p["vec"][l], p["lvec"][l], p["w2f"][l], p["a2f"][l], p["g2f"][l], p["bdiag"])
    yf, yb = _rwkv_scan(r, v, kk, lwf, kf, bf, lwb, kb, bb)
    flat = lambda a: a.reshape(n, a.shape[-1])
    ya = _rwkv_post(flat(yf), flat(yb), flat(bonus), flat(g), p["gnvec"][l], p["bdiag"])
    ybr = _spatial_gating(proj3, p["sg_ln"][l], p["sgw2"][l], p["sg_bias"][l])
    ycr = _neighbourhood_attention(proj3, p["na_bias"][l])
    m = _merge(ya, flat(ybr), flat(ycr), proj, p["p_a"][l], p["p_b"][l], p["p_c"][l])
    return _out_proj_ln(m, p["w_out"][l], x, p["ln_mix"][l])


def _trunk(x3, p):
    b, t, _ = x3.shape
    x = x3.reshape(b * t, D_MODEL)
    x_bf = x.astype(BF16)
    for l in range(DEPTH):
        x, x_bf = _mixer(x, x_bf, p, l, b, t)
        x, x_bf = _expert_choice_moe_ln(x, x_bf, p["w_router_t"][l], p["e_gate"][l], p["e_up"][l],
                                        p["e_down"][l], p["ln_ffn"][l])
    return x.reshape(b, t, D_MODEL)


def kernel(x_prompt, x_sample, w_in, mu_prev, mu_next, decay_w0, decay_w2, iclr_a0, iclr_a2, gate_g2, k_k, k_a, r_k, gn_g, gn_b, sg_ln_g, sg_ln_b, sg_w, sg_b, rpb, p_a, p_b, p_c, w_out, ln_mix_g, ln_mix_b, w_router, e_gate, e_up, e_down, ln_ffn_g, ln_ffn_b):
    assert x_prompt.shape[1] == x_sample.shape[1]
    rows = x_prompt.shape[1] // GRID_W
    p = _prepare(w_in, mu_prev, mu_next, decay_w0, decay_w2, iclr_a0, iclr_a2, gate_g2, k_k, k_a, r_k, gn_g, gn_b,
                 sg_ln_g, sg_ln_b, sg_w, sg_b, rpb, p_a, p_b, p_c, w_out, ln_mix_g, ln_mix_b, w_router, e_gate,
                 e_up, e_down, ln_ffn_g, ln_ffn_b, rows)
    return (_trunk(x_prompt, p), _trunk(x_sample, p))
```

```python
import functools
from typing import NamedTuple

import numpy as np
import jax
import jax.numpy as jnp
from jax import lax
from jax.experimental import pallas as pl
from jax.experimental.pallas import tpu as pltpu

F32 = jnp.float32
BF16 = jnp.bfloat16
I32 = jnp.int32

D_MODEL = 2048
DEPTH = 4
GRID_W = 64
A_HEADS = 16
A_HEAD_DIM = 64
A_WIDTH = A_HEADS * A_HEAD_DIM
A_DECAY_LORA = 64
A_ICLR_LORA = 64
A_GATE_LORA = 160
B_GROUPS = 8
B_GROUP_DIM = 64
B_WIDTH = B_GROUPS * B_GROUP_DIM
B_CHUNK = 128
C_HEADS = 8
C_HEAD_DIM = 64
C_WIDTH = C_HEADS * C_HEAD_DIM
C_WIN_ROWS = 8
C_WIN_COLS = 16
N_EXPERTS = 16
EXPERT_HIDDEN = 1024
EC_CAPACITY_FACTOR = 2
A_COLS = 3 * A_WIDTH + 2 * A_DECAY_LORA + 2 * A_ICLR_LORA + A_GATE_LORA
B_COLS = 2 * B_WIDTH
C_COLS = 3 * C_WIDTH
G_COLS = 3 * D_MODEL
DEEPNORM_ALPHA = (2 * DEPTH) ** 0.25
LN_EPS = 1e-5
GN_EPS = 64e-5

LORA_COLS = 2 * A_DECAY_LORA + 2 * A_ICLR_LORA + A_GATE_LORA
LORA_PAD = 512
COL_R, COL_K, COL_V = 0, A_WIDTH, 2 * A_WIDTH
COL_B = 3 * A_WIDTH
COL_G = COL_B + B_COLS
COL_C = COL_G + G_COLS
COL_L = COL_C + C_COLS
IN_PAD = COL_L + LORA_PAD

LANES = 128
SCAN_CHUNK = 64
MOE_TILE = 256
MOE_SUB = 64
VMEM_LIMIT = 48 * 1024 * 1024


def _cparams(sem):
    return pltpu.CompilerParams(dimension_semantics=sem, vmem_limit_bytes=VMEM_LIMIT)


def _dot(a, b):
    return jnp.dot(a.astype(BF16), b.astype(BF16), preferred_element_type=F32)


def _dot_nt(a, b):
    return lax.dot_general(a.astype(BF16), b.astype(BF16), (((1,), (1,)), ((), ())),
                           preferred_element_type=F32)


def _split2(x):
    hi = x.astype(BF16)
    lo = (x - hi.astype(F32)).astype(BF16)
    return hi, lo


def _split3(x):
    hi = x.astype(BF16)
    r1 = x - hi.astype(F32)
    mid = r1.astype(BF16)
    lo = (r1 - mid.astype(F32)).astype(BF16)
    return hi, mid, lo


def _dot_exact01(m01, x):
    hi, mid, lo = _split3(x)
    m = m01.astype(BF16)
    return (jnp.dot(m, hi, preferred_element_type=F32) + jnp.dot(m, mid, preferred_element_type=F32)
            + jnp.dot(m, lo, preferred_element_type=F32))


def _dot_x01(x, m01):
    hi, mid, lo = _split3(x)
    m = m01.astype(BF16)
    return (jnp.dot(hi, m, preferred_element_type=F32) + jnp.dot(mid, m, preferred_element_type=F32)
            + jnp.dot(lo, m, preferred_element_type=F32))


def _head_sums(x, hsum, hexp):
    return _dot_x01(_dot_x01(x, hsum), hexp)


def _sigmoid(x):
    return 1.0 / (1.0 + jnp.exp(-x))


def _layer_norm_rows(z, g, b):
    mu = jnp.mean(z, axis=-1, keepdims=True)
    d = z - mu
    var = jnp.mean(d * d, axis=-1, keepdims=True)
    return d * lax.rsqrt(var + LN_EPS) * g + b


def _mm_kernel(a_ref, w_ref, o_ref):
    o_ref[...] = jnp.dot(a_ref[...], w_ref[...], preferred_element_type=F32).astype(o_ref.dtype)


def _matmul(a, w, tm, tn, out_dtype=F32):
    m, k = a.shape
    n = w.shape[1]
    tm = min(tm, m)
    return pl.pallas_call(
        _mm_kernel,
        grid=(n // tn, m // tm),
        in_specs=[pl.BlockSpec((tm, k), lambda j, i: (i, 0)),
                  pl.BlockSpec((k, tn), lambda j, i: (0, j))],
        out_specs=pl.BlockSpec((tm, tn), lambda j, i: (i, j)),
        out_shape=jax.ShapeDtypeStruct((m, n), out_dtype),
        compiler_params=_cparams(("parallel", "parallel")),
        name="in_proj",
    )(a, w)


PREP_TT = 128
V_MUP_R, V_MUP_K, V_MUP_V, V_MUN_R, V_MUN_K, V_MUN_V, V_KK, V_KA, V_RK, V_W0F, V_W0B, V_A0F, V_A0B = range(13)


def _shift(cur, prv8, nxt8, mup, mun, i, nt):
    tt = cur.shape[0]
    row = lax.broadcasted_iota(I32, cur.shape, 0)
    first = jnp.where(i > 0, prv8[7:8, :], 0.0)
    last = jnp.where(i < nt - 1, nxt8[0:1, :], 0.0)
    prev = jnp.where(row == 0, first, pltpu.roll(cur, 1, 0))
    nxt = jnp.where(row == tt - 1, last, pltpu.roll(cur, tt - 1, 0))
    return cur + mup * (prev - cur) + mun * (nxt - cur)


def _prep_kernel(rc, kc, vc, lc, rp, kp, vp, lp, rn, kn, vn, ln_, vec_ref, lvec_ref, w2_ref, a2_ref, g2_ref,
                 hs_ref, he_ref, r_o, v_o, kk_o, lwf_o, kf_o, bf_o, lwb_o, kb_o, bb_o, bonus_o, g_o):
    i = pl.program_id(1)
    nt = pl.num_programs(1)
    vec = vec_ref[...]
    row = lambda j: vec[j:j + 1, :]
    r = _shift(rc[0], rp[0], rn[0], row(V_MUP_R), row(V_MUN_R), i, nt)
    k = _shift(kc[0], kp[0], kn[0], row(V_MUP_K), row(V_MUN_K), i, nt)
    v = _shift(vc[0], vp[0], vn[0], row(V_MUP_V), row(V_MUN_V), i, nt)
    lo = _shift(lc[0], lp[0], ln_[0], lvec_ref[0:1, :], lvec_ref[1:2, :], i, nt)

    hsum, hexp = hs_ref[...], he_ref[...]
    kk = k * row(V_KK)
    kk = kk / jnp.maximum(jnp.sqrt(_head_sums(kk * kk, hsum, hexp)), 1e-12)

    tanh_lo = jnp.tanh(lo)
    ksum = jnp.zeros_like(k)
    outs = ((lwf_o, kf_o, bf_o), (lwb_o, kb_o, bb_o))
    for d in range(2):
        z = row(V_W0F + d) + _dot(tanh_lo, w2_ref[d])
        nz = -z
        softplus = jnp.maximum(nz, 0.0) + jnp.log(1.0 + jnp.exp(-jnp.abs(nz)))
        w_log = -softplus - 0.5
        lw = -jnp.exp(w_log)
        a = _sigmoid(row(V_A0F + d) + _dot(lo, a2_ref[d]))
        k_d = k * (1.0 + (a - 1.0) * row(V_KA))
        ksum = ksum + k_d
        lw_o, kd_o, bd_o = outs[d]
        lw_o[0] = lw
        kd_o[0] = k_d
        bd_o[0] = kk * a
    r_o[0] = r
    v_o[0] = v
    kk_o[0] = kk
    bonus_o[0] = _head_sums(r * ksum * row(V_RK), hsum, hexp) * v
    g_o[0] = _dot(_sigmoid(lo), g2_ref[...])


def _rwkv_prep(proj3, vec, lvec, w2f, a2f, g2f, hsum, hexp):
    b, t, _ = proj3.shape
    tt = PREP_TT
    nt = t // tt
    h8 = tt // 8
    cur = lambda cb, w: pl.BlockSpec((1, tt, w), lambda bi, i: (bi, i, cb))
    prv = lambda cb, w: pl.BlockSpec((1, 8, w), lambda bi, i: (bi, jnp.maximum(i * h8 - 1, 0), cb))
    nxt = lambda cb, w: pl.BlockSpec((1, 8, w), lambda bi, i: (bi, jnp.minimum((i + 1) * h8, t // 8 - 1), cb))
    cols = [(COL_R // A_WIDTH, A_WIDTH), (COL_K // A_WIDTH, A_WIDTH), (COL_V // A_WIDTH, A_WIDTH),
            (COL_L // LORA_PAD, LORA_PAD)]
    full = lambda shape: pl.BlockSpec(shape, lambda bi, i: (0,) * len(shape))
    in_specs = ([cur(*c) for c in cols] + [prv(*c) for c in cols] + [nxt(*c) for c in cols]
                + [full(vec.shape), full(lvec.shape), full(w2f.shape), full(a2f.shape), full(g2f.shape),
                   full(hsum.shape), full(hexp.shape)])
    out_spec = pl.BlockSpec((1, tt, A_WIDTH), lambda bi, i: (bi, i, 0))
    out_sds = jax.ShapeDtypeStruct((b, t, A_WIDTH), F32)
    return pl.pallas_call(
        _prep_kernel,
        grid=(b, nt),
        in_specs=in_specs,
        out_specs=[out_spec] * 11,
        out_shape=[out_sds] * 11,
        compiler_params=_cparams(("parallel", "parallel")),
        name="rwkv_prep",
    )(*([proj3] * 12), vec, lvec, w2f, a2f, g2f, hsum, hexp)


def _stack2(x, m0):
    return jnp.concatenate([jnp.where(m0, x, 0.0), jnp.where(m0, 0.0, x)], axis=0)


class _Chain(NamedTuple):
    r: jax.Array
    v: jax.Array
    kk: jax.Array
    lw: jax.Array
    kd: jax.Array
    bd: jax.Array
    s_ref: object
    rev: bool


def _scan_masks(c, reverse):
    ti = lax.broadcasted_iota(I32, (c, c), 0)
    si = lax.broadcasted_iota(I32, (c, c), 1)
    tri = jnp.where((si >= ti) if reverse else (si <= ti), 1.0, 0.0).astype(BF16)
    t2 = lax.broadcasted_iota(I32, (c, 2 * c), 0)
    s2 = lax.broadcasted_iota(I32, (c, 2 * c), 1)
    s2 = jnp.where(s2 >= c, s2 - c, s2)
    strict = (s2 > t2) if reverse else (s2 < t2)
    incl = (s2 >= t2) if reverse else (s2 <= t2)
    eye2 = jnp.where(s2 == t2, 1.0, 0.0)
    return tri, strict, incl, eye2


def _scan_chunk(chains):
    c = SCAN_CHUNK
    masks = {rev: _scan_masks(c, rev) for rev in (False, True)}
    m0 = lax.broadcasted_iota(I32, (c, LANES), 1) < A_HEAD_DIM
    half = lax.broadcasted_iota(I32, (c, 2 * c), 1) < c
    vi = lax.broadcasted_iota(I32, (LANES, LANES), 0)
    ki = lax.broadcasted_iota(I32, (LANES, LANES), 1)
    same_head = (vi < A_HEAD_DIM) == (ki < A_HEAD_DIM)
    bf = lambda x: x.astype(BF16)
    mm = lambda a, b: jnp.dot(a, b, preferred_element_type=F32)
    mm_nt = lambda a, b: lax.dot_general(a, b, (((1,), (1,)), ((), ())), preferred_element_type=F32)
    cat = jnp.concatenate

    cum = []
    for ch in chains:
        tri = masks[ch.rev][0]
        hi, mid, lo = _split3(ch.lw)
        cum.append(mm(tri, hi) + mm(tri, mid) + mm(tri, lo))

    ar, bk2, bkw, v2, etot = [], [], [], [], []
    for ch, cm in zip(chains, cum):
        tot = cm[0:1, :] if ch.rev else cm[c - 1:c, :]
        einv = jnp.exp(-cm)
        etail = jnp.exp(tot - cm)
        at = -ch.kk * jnp.exp(cm - ch.lw)
        rt = ch.r * jnp.exp(cm)
        ar.append(bf(cat([at, rt], axis=0)))
        bk2.append(bf(cat([_stack2(ch.bd * einv, m0), _stack2(ch.kd * einv, m0)], axis=0)))
        bkw.append(bf(cat([ch.bd * etail, ch.kd * etail], axis=0)))
        v2.append(bf(_stack2(ch.v, m0)))
        etot.append(jnp.exp(tot))

    mt = [mm_nt(a, b) for a, b in zip(ar, bk2)]
    s0 = [ch.s_ref[...] for ch in chains]
    ars = [mm_nt(a, bf(s)) for a, s in zip(ar, s0)]
    a_ab, a_ak, a_rbk = [], [], []
    for ch, m in zip(chains, mt):
        _, strict, incl, _ = masks[ch.rev]
        a_ab.append(jnp.where(strict, m[:c, :2 * c], 0.0))
        a_ak.append(bf(jnp.where(strict, m[:c, 2 * c:], 0.0)))
        a_rbk.append(bf(cat([jnp.where(incl, m[c:, :2 * c], 0.0), jnp.where(incl, m[c:, 2 * c:], 0.0)], axis=1)))
    akv = [mm(a, v) for a, v in zip(a_ak, v2)]

    tm = [masks[ch.rev][3] + a for ch, a in zip(chains, a_ab)]
    pw = [mm(bf(a), _stack2(bf(a), half)) for a in a_ab]
    steps = int(np.log2(c)) - 1
    for k in range(steps):
        blk = [_stack2(bf(x), half) for x in pw]
        tm = [t + mm(bf(t), d) for t, d in zip(tm, blk)]
        if k < steps - 1:
            pw = [mm(bf(x), d) for x, d in zip(pw, blk)]

    u = [mm(bf(t), bf(_stack2(a[:c] + k, m0))) for t, a, k in zip(tm, ars, akv)]
    y = [a[c:] + mm(ab, cat([bf(_stack2(uu, m0)), vv], axis=0)) for a, ab, uu, vv in zip(ars, a_rbk, u, v2)]
    upd = [mm(bf(cat([uu, ch.v], axis=0).T), w) for uu, ch, w in zip(u, chains, bkw)]
    for ch, s, e, up in zip(chains, s0, etot, upd):
        ch.s_ref[...] = s * e + jnp.where(same_head, up, 0.0)
    return y


SCAN_HP = 8


def _scan_kernel(rf, vf, kkf, lwf, kf, bf, rb, vb, kkb, lwb, kb, bb, yf_o, yb_o, sf_ref, sb_ref):
    @pl.when(pl.program_id(2) == 0)
    def _():
        sf_ref[...] = jnp.zeros_like(sf_ref)
        sb_ref[...] = jnp.zeros_like(sb_ref)

    chains = []
    for hp in range(SCAN_HP):
        ln = slice(hp * LANES, (hp + 1) * LANES)
        chains.append(_Chain(rf[0, :, ln], vf[0, :, ln], kkf[0, :, ln], lwf[0, :, ln], kf[0, :, ln], bf[0, :, ln],
                             sf_ref.at[hp], False))
        chains.append(_Chain(rb[0, :, ln], vb[0, :, ln], kkb[0, :, ln], lwb[0, :, ln], kb[0, :, ln], bb[0, :, ln],
                             sb_ref.at[hp], True))
    y = _scan_chunk(chains)
    for hp in range(SCAN_HP):
        ln = slice(hp * LANES, (hp + 1) * LANES)
        yf_o[0, :, ln] = y[2 * hp]
        yb_o[0, :, ln] = y[2 * hp + 1]


def _rwkv_scan(r, v, kk, lwf, kf, bf, lwb, kb, bb):
    b, t, _ = r.shape
    c = SCAN_CHUNK
    nc = t // c
    w = SCAN_HP * LANES
    fwd = pl.BlockSpec((1, c, w), lambda bi, hp, ci: (bi, ci, hp))
    bwd = pl.BlockSpec((1, c, w), lambda bi, hp, ci: (bi, nc - 1 - ci, hp))
    out_sds = jax.ShapeDtypeStruct((b, t, A_WIDTH), F32)
    return pl.pallas_call(
        _scan_kernel,
        grid=(b, A_WIDTH // w, nc),
        in_specs=[fwd] * 6 + [bwd] * 6,
        out_specs=[fwd, bwd],
        out_shape=[out_sds, out_sds],
        scratch_shapes=[pltpu.VMEM((SCAN_HP, LANES, LANES), F32), pltpu.VMEM((SCAN_HP, LANES, LANES), F32)],
        compiler_params=_cparams(("parallel", "parallel", "arbitrary")),
        name="rwkv_scan",
    )(r, v, kk, lwf, kf, bf, r, v, kk, lwb, kb, bb)


POST_TT = 256


def _post_kernel(yf, yb, bonus, g, vec_ref, hs_ref, he_ref, o_ref):
    y = yf[...] + yb[...]
    hsum, hexp = hs_ref[...], he_ref[...]
    inv = 1.0 / A_HEAD_DIM
    mu = _head_sums(y, hsum, hexp) * inv
    d = y - mu
    var = _head_sums(d * d, hsum, hexp) * inv
    yn = d * lax.rsqrt(var + GN_EPS) * vec_ref[0:1, :] + vec_ref[1:2, :]
    o_ref[...] = ((yn + bonus[...]) * g[...]).astype(o_ref.dtype)


def _rwkv_post(yf, yb, bonus, g, gnvec, hsum, hexp):
    n = yf.shape[0]
    tt = min(POST_TT, n)
    blk = pl.BlockSpec((tt, A_WIDTH), lambda i: (i, 0))
    full = lambda a: pl.BlockSpec(a.shape, lambda i: (0,) * a.ndim)
    return pl.pallas_call(
        _post_kernel,
        grid=(n // tt,),
        in_specs=[blk, blk, blk, blk, full(gnvec), full(hsum), full(hexp)],
        out_specs=blk,
        out_shape=jax.ShapeDtypeStruct((n, A_WIDTH), BF16),
        compiler_params=_cparams(("parallel",)),
        name="rwkv_post",
    )(yf, yb, bonus, g, gnvec, hsum, hexp)


def _gelu_tanh(x):
    return 0.5 * x * (1.0 + jnp.tanh(np.sqrt(2.0 / np.pi).astype(np.float32) * (x + 0.044715 * (x * x * x))))


def _sg_kernel(pb_ref, lnv_ref, w_ref, bias_ref, o_ref):
    z = _gelu_tanh(pb_ref[0])
    u = z[:, :B_WIDTH]
    v = _layer_norm_rows(z[:, B_WIDTH:], lnv_ref[0:1, :], lnv_ref[1:2, :])
    lane = lax.broadcasted_iota(I32, (B_CHUNK, LANES), 1)
    m0 = lane < B_GROUP_DIM
    parts = []
    for q in range(B_WIDTH // LANES):
        vq = v[:, q * LANES:(q + 1) * LANES]
        parts.append(_dot(w_ref[q], _stack2(vq, m0)))
    mixed = jnp.concatenate(parts, axis=1) + bias_ref[...]
    o_ref[0] = (u * mixed).astype(o_ref.dtype)


def _spatial_gating(proj3, lnvec, w2, bias_full):
    b, t, _ = proj3.shape
    full = lambda a: pl.BlockSpec(a.shape, lambda bi, i: (0,) * a.ndim)
    return pl.pallas_call(
        _sg_kernel,
        grid=(b, t // B_CHUNK),
        in_specs=[pl.BlockSpec((1, B_CHUNK, B_COLS), lambda bi, i: (bi, i, COL_B // B_COLS)),
                  full(lnvec), full(w2), full(bias_full)],
        out_specs=pl.BlockSpec((1, B_CHUNK, B_WIDTH), lambda bi, i: (bi, i, 0)),
        out_shape=jax.ShapeDtypeStruct((b, t, B_WIDTH), BF16),
        compiler_params=_cparams(("parallel", "parallel")),
        name="spatial_gating",
    )(proj3, lnvec, w2, bias_full)


NA_WIN = C_WIN_ROWS * GRID_W


def _na_row_start(i, rows):
    return jnp.clip(i - C_WIN_ROWS // 2, 0, rows - C_WIN_ROWS)


NA_QR = 2


def _na_kernel(q_ref, k_ref, v_ref, *rest, rows):
    bias_refs, o_ref = rest[:NA_QR], rest[NA_QR]
    i = pl.program_id(1)
    scale = C_HEAD_DIM ** -0.5
    qs, ks, vs, bs = [], [], [], []
    for j in range(NA_QR):
        start = pl.multiple_of(_na_row_start(i * NA_QR + j, rows) * GRID_W, GRID_W)
        q = q_ref[0, j * GRID_W:(j + 1) * GRID_W, :]
        kwin = k_ref[0, pl.ds(start, NA_WIN), :]
        vwin = v_ref[0, pl.ds(start, NA_WIN), :]
        for h in range(C_HEADS):
            sl = slice(h * C_HEAD_DIM, (h + 1) * C_HEAD_DIM)
            qs.append(q[:, sl].astype(BF16))
            ks.append(kwin[:, sl].astype(BF16))
            vs.append(vwin[:, sl].astype(BF16))
            bs.append(bias_refs[j][0, h])
    nt = lambda a, b: lax.dot_general(a, b, (((1,), (1,)), ((), ())), preferred_element_type=F32)
    ss = [nt(q, k) * scale + bias for q, k, bias in zip(qs, ks, bs)]
    ps = []
    for s in ss:
        e = jnp.exp(s - jnp.max(s, axis=-1, keepdims=True))
        ps.append((e / jnp.sum(e, axis=-1, keepdims=True)).astype(BF16))
    os_ = [jnp.dot(p, v, preferred_element_type=F32) for p, v in zip(ps, vs)]
    for j in range(NA_QR):
        o_ref[0, j * GRID_W:(j + 1) * GRID_W, :] = jnp.concatenate(
            os_[j * C_HEADS:(j + 1) * C_HEADS], axis=1).astype(o_ref.dtype)


def _neighbourhood_attention(proj3, bias_tab):
    b, t, _ = proj3.shape
    rows = t // GRID_W
    cq = COL_C // C_WIDTH
    qr = NA_QR
    seq = lambda cb: pl.BlockSpec((1, t, C_WIDTH), lambda bi, i: (bi, 0, cb))
    bias = lambda j: pl.BlockSpec((1, C_HEADS, GRID_W, NA_WIN),
                                  lambda bi, i: (i * qr + j - _na_row_start(i * qr + j, rows), 0, 0, 0))
    return pl.pallas_call(
        functools.partial(_na_kernel, rows=rows),
        grid=(b, rows // qr),
        in_specs=[pl.BlockSpec((1, qr * GRID_W, C_WIDTH), lambda bi, i: (bi, i, cq)), seq(cq + 1), seq(cq + 2)]
                 + [bias(j) for j in range(qr)],
        out_specs=pl.BlockSpec((1, qr * GRID_W, C_WIDTH), lambda bi, i: (bi, i, 0)),
        out_shape=jax.ShapeDtypeStruct((b, t, C_WIDTH), BF16),
        compiler_params=_cparams(("parallel", "arbitrary")),
        name="nbr_attention",
    )(proj3, proj3, proj3, *([bias_tab] * qr))


def _na_bias_table(rpb, rows):
    kc = C_WIN_COLS
    cols = np.arange(GRID_W)
    col_start = np.clip(cols - kc // 2, 0, GRID_W - kc)
    key_col = np.arange(GRID_W)
    in_win = (key_col[None, :] >= col_start[:, None]) & (key_col[None, :] < col_start[:, None] + kc)
    col_off = np.clip(key_col[None, :] - cols[:, None] + (C_WIN_COLS - 1), 0, 2 * C_WIN_COLS - 2)
    tabs = []
    for delta in range(C_WIN_ROWS):
        row_off = np.arange(C_WIN_ROWS) - delta + (C_WIN_ROWS - 1)
        bias = rpb[:, row_off][:, :, col_off]
        bias = jnp.where(in_win[None, None], bias, -1e30)
        bias = jnp.transpose(bias, (0, 2, 1, 3)).reshape(C_HEADS, GRID_W, NA_WIN)
        tabs.append(bias)
    return jnp.stack(tabs).astype(F32)


MERGE_TM = 256


def _merge_kernel(ya, yb, yc, ga, gb, gc, pa, pb, pc, o_ref):
    m = _sigmoid(ga[...]) * jnp.dot(ya[...], pa[...], preferred_element_type=F32)
    m = m + _sigmoid(gb[...]) * jnp.dot(yb[...], pb[...], preferred_element_type=F32)
    m = m + _sigmoid(gc[...]) * jnp.dot(yc[...], pc[...], preferred_element_type=F32)
    o_ref[...] = m.astype(o_ref.dtype)


def _merge(ya, yb, yc, proj, p_a, p_b, p_c):
    n = ya.shape[0]
    tm = min(MERGE_TM, n)
    rowblk = lambda w: pl.BlockSpec((tm, w), lambda i: (i, 0))
    gate = lambda j: pl.BlockSpec((tm, D_MODEL), lambda i: (i, COL_G // D_MODEL + j))
    full = lambda a: pl.BlockSpec(a.shape, lambda i: (0,) * a.ndim)
    return pl.pallas_call(
        _merge_kernel,
        grid=(n // tm,),
        in_specs=[rowblk(A_WIDTH), rowblk(B_WIDTH), rowblk(C_WIDTH), gate(0), gate(1), gate(2),
                  full(p_a), full(p_b), full(p_c)],
        out_specs=rowblk(D_MODEL),
        out_shape=jax.ShapeDtypeStruct((n, D_MODEL), BF16),
        compiler_params=_cparams(("parallel",)),
        name="branch_merge",
    )(ya, yb, yc, proj, proj, proj, p_a, p_b, p_c)


OUT_TM = 256


def _outln_kernel(m_ref, w_ref, x_ref, lnv_ref, o_ref, ob_ref):
    h = jnp.dot(m_ref[...], w_ref[...], preferred_element_type=F32)
    y = _layer_norm_rows(DEEPNORM_ALPHA * x_ref[...] + h, lnv_ref[0:1, :], lnv_ref[1:2, :])
    o_ref[...] = y
    ob_ref[...] = y.astype(BF16)


def _out_proj_ln(m, w_out, x, lnvec):
    n = m.shape[0]
    tm = min(OUT_TM, n)
    blk = pl.BlockSpec((tm, D_MODEL), lambda i: (i, 0))
    full = lambda a: pl.BlockSpec(a.shape, lambda i: (0,) * a.ndim)
    return pl.pallas_call(
        _outln_kernel,
        grid=(n // tm,),
        in_specs=[blk, full(w_out), blk, full(lnvec)],
        out_specs=[blk, blk],
        out_shape=[jax.ShapeDtypeStruct((n, D_MODEL), F32), jax.ShapeDtypeStruct((n, D_MODEL), BF16)],
        compiler_params=_cparams(("parallel",)),
        name="out_proj_ln",
    )(m, w_out, x, lnvec)


ROUTER_TM = 512


def _router_kernel(x_ref, wt_ref, o_ref):
    xh, xl = _split2(x_ref[...])
    wh, wl = _split2(wt_ref[...])
    nt = lambda a, b: lax.dot_general(a, b, (((1,), (1,)), ((), ())), preferred_element_type=F32)
    logits = nt(wh, xh) + nt(wl, xh) + nt(wh, xl)
    logits = logits - jnp.max(logits, axis=0, keepdims=True)
    e = jnp.exp(logits)
    o_ref[...] = e / jnp.sum(e, axis=0, keepdims=True)


def _router(x, w_router_t):
    n = x.shape[0]
    tm = min(ROUTER_TM, n)
    return pl.pallas_call(
        _router_kernel,
        grid=(n // tm,),
        in_specs=[pl.BlockSpec((tm, D_MODEL), lambda i: (i, 0)),
                  pl.BlockSpec(w_router_t.shape, lambda i: (0, 0))],
        out_specs=pl.BlockSpec((N_EXPERTS, tm), lambda i: (0, i)),
        out_shape=jax.ShapeDtypeStruct((N_EXPERTS, n), F32),
        compiler_params=_cparams(("parallel",)),
        name="router",
    )(x, w_router_t)


def _select_kernel(aff_ref, pos_ref, off_ref, *, cap):
    aff = aff_ref[...]
    e_, g_, l_ = aff.shape
    n = g_ * l_
    bits = pltpu.bitcast(aff, I32)

    def count(mask):
        c = jnp.sum(jnp.where(mask, 1.0, 0.0), axis=2, keepdims=True)
        return jnp.sum(c, axis=1, keepdims=True)

    def thr_body(_, carry):
        lo, hi = carry
        mid = lo + (hi - lo + 1) // 2
        ok = count(bits >= mid) >= cap
        return jnp.where(ok, mid, lo), jnp.where(ok, hi, mid - 1)

    lo0 = jnp.zeros((e_, 1, 1), I32)
    hi0 = jnp.full((e_, 1, 1), 0x7F800000, I32)
    thr, _ = lax.fori_loop(0, 32, thr_body, (lo0, hi0))
    gt = bits > thr
    tie = bits == thr
    need = cap - count(gt)
    idx = lax.broadcasted_iota(I32, aff.shape, 1) * l_ + lax.broadcasted_iota(I32, aff.shape, 2)

    def idx_body(_, carry):
        lo, hi = carry
        mid = (lo + hi) // 2
        ok = count(tie & (idx <= mid)) >= need
        return jnp.where(ok, lo, mid + 1), jnp.where(ok, mid, hi)

    lo1 = jnp.zeros((e_, 1, 1), I32)
    hi1 = jnp.full((e_, 1, 1), n - 1, I32)
    cut, _ = lax.fori_loop(0, int(np.ceil(np.log2(n))) + 1, idx_body, (lo1, hi1))
    sel = jnp.where(gt | (tie & (idx <= cut)), 1.0, 0.0)

    sel2 = sel.reshape(e_ * g_, l_)
    ls = lax.broadcasted_iota(I32, (l_, l_), 0)
    lt = lax.broadcasted_iota(I32, (l_, l_), 1)
    incl = jnp.dot(sel2.astype(BF16), jnp.where(ls <= lt, 1.0, 0.0).astype(BF16), preferred_element_type=F32)
    tot = jnp.dot(sel2.astype(BF16), jnp.ones((l_, l_), BF16), preferred_element_type=F32)
    gs = lax.broadcasted_iota(I32, (g_, g_), 0)
    gt_ = lax.broadcasted_iota(I32, (g_, g_), 1)
    lower = jnp.where(gt_ < gs, 1.0, 0.0).astype(BF16)
    offs = [jnp.dot(lower, tot[e * g_:(e + 1) * g_].astype(BF16), preferred_element_type=F32)
            for e in range(e_)]
    off = jnp.concatenate(offs, axis=0)
    pos = incl + off - sel2
    pos_ref[...] = jnp.where(sel2 > 0.5, pos, -1.0).astype(I32).reshape(e_, g_, l_)
    off_ref[...] = off.astype(I32).reshape(e_, g_, l_)


def _select(aff3, cap):
    full = pl.BlockSpec(aff3.shape, lambda i: (0, 0, 0))
    sds = jax.ShapeDtypeStruct(aff3.shape, I32)
    return pl.pallas_call(
        functools.partial(_select_kernel, cap=cap),
        grid=(1,),
        in_specs=[full],
        out_specs=[full, full],
        out_shape=[sds, sds],
        compiler_params=_cparams(("arbitrary",)),
        name="expert_select",
    )(aff3)


def _gather_kernel(e_s, slab_s, tile_s, valid_s, first_s, x_ref, pos_ref, aff_ref, o_ref, g_ref):
    it = pl.program_id(0)

    @pl.when(first_s[it] == 1)
    def _():
        o_ref[...] = jnp.zeros_like(o_ref)
        g_ref[...] = jnp.zeros_like(g_ref)

    @pl.when(valid_s[it] == 1)
    def _():
        t = MOE_TILE
        want = lax.broadcasted_iota(I32, (t, t), 0) + slab_s[it] * t
        hit = pos_ref[0] == want
        got = jnp.dot(jnp.where(hit, 1.0, 0.0).astype(BF16), x_ref[...], preferred_element_type=F32)
        o_ref[0] = (o_ref[0].astype(F32) + got).astype(o_ref.dtype)
        g_ref[0] += jnp.sum(jnp.where(hit, aff_ref[0], 0.0), axis=1, keepdims=True)


def _moe_gather(x_bf, pos_en, aff_en, items, cap):
    n = x_bf.shape[0]
    t = MOE_TILE
    ni = items[0].shape[0]
    row = pl.BlockSpec((1, 1, t), lambda it, e, s, tl, va, fi: (e[it], 0, tl[it]))
    grid_spec = pltpu.PrefetchScalarGridSpec(
        num_scalar_prefetch=5,
        grid=(ni,),
        in_specs=[pl.BlockSpec((t, D_MODEL), lambda it, e, s, tl, va, fi: (tl[it], 0)), row, row],
        out_specs=[pl.BlockSpec((1, t, D_MODEL), lambda it, e, s, tl, va, fi: (e[it], s[it], 0)),
                   pl.BlockSpec((1, t, 1), lambda it, e, s, tl, va, fi: (e[it], s[it], 0))],
    )
    return pl.pallas_call(
        _gather_kernel,
        grid_spec=grid_spec,
        out_shape=[jax.ShapeDtypeStruct((N_EXPERTS, cap, D_MODEL), BF16),
                   jax.ShapeDtypeStruct((N_EXPERTS, cap, 1), F32)],
        compiler_params=_cparams(("arbitrary",)),
        name="moe_gather",
    )(*items, x_bf, pos_en.reshape(N_EXPERTS, 1, n), aff_en.reshape(N_EXPERTS, 1, n))


FFN_TM = 512


def _ffn_kernel(x_ref, gate_ref, wg_ref, wu_ref, wd_ref, hi_ref, lo_ref):
    x = x_ref[0]
    g = jnp.dot(x, wg_ref[0], preferred_element_type=F32)
    u = jnp.dot(x, wu_ref[0], preferred_element_type=F32)
    h = (g * _sigmoid(g)) * u
    y = jnp.dot(h.astype(BF16), wd_ref[0], preferred_element_type=F32) * gate_ref[0]
    hi, lo = _split2(y)
    hi_ref[0] = hi
    lo_ref[0] = lo


def _expert_ffn(xe, gate, wg, wu, wd):
    e_, cap, _ = xe.shape
    tm = min(FFN_TM, cap)
    wspec = lambda a: pl.BlockSpec((1,) + a.shape[1:], lambda e, i: (e, 0, 0))
    rows = lambda w: pl.BlockSpec((1, tm, w), lambda e, i: (e, i, 0))
    sds = jax.ShapeDtypeStruct((e_, cap, D_MODEL), BF16)
    return pl.pallas_call(
        _ffn_kernel,
        grid=(e_, cap // tm),
        in_specs=[rows(D_MODEL), rows(1), wspec(wg), wspec(wu), wspec(wd)],
        out_specs=[rows(D_MODEL), rows(D_MODEL)],
        out_shape=[sds, sds],
        compiler_params=_cparams(("parallel", "arbitrary")),
        name="expert_ffn",
    )(xe, gate, wg, wu, wd)


def _combine_kernel(e_s, slab_s, tile_s, valid_s, first_s, last_s, hi_ref, lo_ref, pos_ref, x_ref, lnv_ref,
                    o_ref, ob_ref, acc_ref):
    it = pl.program_id(0)

    @pl.when(first_s[it] == 1)
    def _():
        acc_ref[...] = jnp.zeros_like(acc_ref)

    @pl.when(valid_s[it] == 1)
    def _():
        t = MOE_TILE
        mine = lax.broadcasted_iota(I32, (t, N_EXPERTS), 1) == e_s[it]
        pos = jnp.sum(jnp.where(mine, pos_ref[...].astype(F32), 0.0), axis=1, keepdims=True)
        want = (lax.broadcasted_iota(I32, (t, MOE_SUB), 1) + slab_s[it] * MOE_SUB).astype(F32)
        onehot = jnp.where(pos == want, 1.0, 0.0).astype(BF16)
        acc_ref[...] += (jnp.dot(onehot, hi_ref[0], preferred_element_type=F32)
                         + jnp.dot(onehot, lo_ref[0], preferred_element_type=F32))

    @pl.when(last_s[it] == 1)
    def _():
        y = _layer_norm_rows(DEEPNORM_ALPHA * x_ref[...] + acc_ref[...], lnv_ref[0:1, :], lnv_ref[1:2, :])
        o_ref[...] = y
        ob_ref[...] = y.astype(BF16)


def _moe_combine(ye_hi, ye_lo, pos_ne, x, lnvec, items):
    n = x.shape[0]
    t = MOE_TILE
    ni = items[0].shape[0]
    tile_blk = lambda w: pl.BlockSpec((t, w), lambda it, e, s, tl, va, fi, la: (tl[it], 0))
    slab = pl.BlockSpec((1, MOE_SUB, D_MODEL), lambda it, e, s, tl, va, fi, la: (e[it], s[it], 0))
    grid_spec = pltpu.PrefetchScalarGridSpec(
        num_scalar_prefetch=6,
        grid=(ni,),
        in_specs=[slab, slab, tile_blk(N_EXPERTS), tile_blk(D_MODEL),
                  pl.BlockSpec(lnvec.shape, lambda it, e, s, tl, va, fi, la: (0, 0))],
        out_specs=[tile_blk(D_MODEL), tile_blk(D_MODEL)],
        scratch_shapes=[pltpu.VMEM((t, D_MODEL), F32)],
    )
    return pl.pallas_call(
        _combine_kernel,
        grid_spec=grid_spec,
        out_shape=[jax.ShapeDtypeStruct((n, D_MODEL), F32), jax.ShapeDtypeStruct((n, D_MODEL), BF16)],
        compiler_params=_cparams(("arbitrary",)),
        name="moe_combine",
    )(*items, ye_hi, ye_lo, pos_ne, x, lnvec)


def _moe_items(group_off, n, cap):
    t = MOE_TILE
    nt = n // t
    e_ = N_EXPERTS
    starts = group_off[:, ::t // LANES, 0]
    ends = jnp.concatenate([starts[:, 1:], jnp.full((e_, 1), cap, I32)], axis=1)
    big = jnp.int32(2 ** 30)

    def build(rows, by_tile):
        ns = cap // rows
        e_p = jnp.broadcast_to(jnp.arange(e_, dtype=I32)[:, None], (e_, nt))
        tile_p = jnp.broadcast_to(jnp.arange(nt, dtype=I32)[None, :], (e_, nt))
        slab_p = jnp.minimum(starts // rows, ns - 1)
        valid_p = (ends > starts).astype(I32)
        bound = jnp.arange(ns, dtype=I32) * rows
        tile_s = jnp.sum((starts[:, :, None] <= bound[None, None, :]).astype(I32), axis=1) - 1
        tile_s = jnp.clip(tile_s, 0, nt - 1)
        st_s = jnp.take_along_axis(starts, tile_s, axis=1)
        en_s = jnp.take_along_axis(ends, tile_s, axis=1)
        valid_s = ((st_s < bound[None, :]) & (bound[None, :] < en_s)).astype(I32)
        e_s = jnp.broadcast_to(jnp.arange(e_, dtype=I32)[:, None], (e_, ns))
        slab_s = jnp.broadcast_to(jnp.arange(ns, dtype=I32)[None, :], (e_, ns))
        cat = lambda a, b: jnp.concatenate([a.reshape(-1), b.reshape(-1)])
        e_a, slab_a, tile_a = cat(e_p, e_s), cat(slab_p, slab_s), cat(tile_p, tile_s)
        valid_a = cat(valid_p, valid_s)
        secondary = cat(jnp.zeros_like(e_p), jnp.ones_like(e_s))
        ni = e_a.shape[0]
        ar = jnp.arange(ni, dtype=I32)
        if by_tile:
            keep = jnp.maximum(valid_a, 1 - secondary)
            key = ((tile_a * e_ + e_a) * 2 + secondary) * ns + slab_a
        else:
            keep = valid_a
            key = (e_a * ns + slab_a) * nt + tile_a
        key = jnp.where(keep == 1, key, big + ar)
        rank = jnp.sum((key[None, :] < key[:, None]).astype(I32), axis=1)
        order = jnp.sum(jnp.where(rank[None, :] == ar[:, None], ar[None, :], 0), axis=1)
        nkeep = jnp.sum(keep)
        src = jnp.where(ar < nkeep, order, order[jnp.maximum(nkeep - 1, 0)])
        live = (ar < nkeep).astype(I32)
        oe, os_, otl = e_a[src], slab_a[src], tile_a[src]
        blk = otl if by_tile else oe * ns + os_
        change = (blk[1:] != blk[:-1]).astype(I32)
        one = jnp.ones((1,), I32)
        return oe, os_, otl, valid_a[src] * live, jnp.concatenate([one, change]), jnp.concatenate([change, one])

    return build(MOE_TILE, False)[:5], build(MOE_SUB, True)


def _expert_choice_moe_ln(x, x_bf, w_router_t, wg, wu, wd, lnvec):
    n = x.shape[0]
    cap = EC_CAPACITY_FACTOR * n // N_EXPERTS
    aff_en = _router(x, w_router_t)
    pos3, off3 = _select(aff_en.reshape(N_EXPERTS, n // LANES, LANES), cap)
    pos_en = pos3.reshape(N_EXPERTS, n)
    g_items, c_items = _moe_items(off3, n, cap)
    xe, gate = _moe_gather(x_bf, pos_en, aff_en, g_items, cap)
    ye_hi, ye_lo = _expert_ffn(xe, gate, wg, wu, wd)
    return _moe_combine(ye_hi, ye_lo, pos_en.T, x, lnvec, c_items)


def _pack_cols(w):
    a_end = A_COLS
    b_end = a_end + B_COLS
    c_end = b_end + C_COLS
    pad = jnp.zeros(w.shape[:-1] + (LORA_PAD - LORA_COLS,), w.dtype)
    return jnp.concatenate([w[..., :3 * A_WIDTH], w[..., a_end:b_end], w[..., c_end:], w[..., b_end:c_end],
                            w[..., 3 * A_WIDTH:a_end], pad], axis=-1)


def _lora_rows(w, start):
    k = w.shape[-2]
    return jnp.pad(w, [(0, 0)] * (w.ndim - 2) + [(start, LORA_PAD - start - k), (0, 0)])


def _prepare(w_in, mu_prev, mu_next, decay_w0, decay_w2, iclr_a0, iclr_a2, gate_g2, k_k, k_a, r_k, gn_g, gn_b,
             sg_ln_g, sg_ln_b, sg_w, sg_b, rpb, p_a, p_b, p_c, w_out, ln_mix_g, ln_mix_b, w_router, e_gate, e_up,
             e_down, ln_ffn_g, ln_ffn_b, rows):
    l_ = w_in.shape[0]
    pad_a = lambda m: jnp.pad(m, ((0, 0), (0, LORA_PAD - LORA_COLS)))
    mup, mun = mu_prev, mu_next
    vec_rows = [mup[:, :A_WIDTH], mup[:, A_WIDTH:2 * A_WIDTH], mup[:, 2 * A_WIDTH:3 * A_WIDTH],
                mun[:, :A_WIDTH], mun[:, A_WIDTH:2 * A_WIDTH], mun[:, 2 * A_WIDTH:3 * A_WIDTH],
                k_k, k_a, r_k.reshape(l_, A_WIDTH), decay_w0[:, 0], decay_w0[:, 1], iclr_a0[:, 0], iclr_a0[:, 1]]
    vec = jnp.stack(vec_rows + [jnp.zeros_like(k_k)] * (16 - len(vec_rows)), axis=1)
    lvec = jnp.stack([pad_a(mup[:, 3 * A_WIDTH:]), pad_a(mun[:, 3 * A_WIDTH:])]
                     + [jnp.zeros((l_, LORA_PAD), F32)] * 6, axis=1)
    w2f = jnp.stack([_lora_rows(decay_w2[:, 0], 0), _lora_rows(decay_w2[:, 1], A_DECAY_LORA)], axis=1)
    a2f = jnp.stack([_lora_rows(iclr_a2[:, 0], 2 * A_DECAY_LORA),
                     _lora_rows(iclr_a2[:, 1], 2 * A_DECAY_LORA + A_ICLR_LORA)], axis=1)
    g2f = _lora_rows(gate_g2, 2 * A_DECAY_LORA + 2 * A_ICLR_LORA)
    head = np.arange(A_WIDTH) // A_HEAD_DIM
    hsum = jnp.asarray(head[:, None] == np.arange(LANES)[None, :], BF16)
    sgw2 = sg_w.reshape(l_, B_GROUPS // 2, 2, B_CHUNK, B_CHUNK).transpose(0, 1, 3, 2, 4)
    sgw2 = sgw2.reshape(l_, B_GROUPS // 2, B_CHUNK, 2 * B_CHUNK)
    sg_bias = jnp.repeat(jnp.swapaxes(sg_b, 1, 2), B_GROUP_DIM, axis=2)
    return dict(
        w_in=_pack_cols(w_in).astype(BF16), vec=vec, lvec=lvec, w2f=w2f.astype(BF16), a2f=a2f.astype(BF16),
        g2f=g2f.astype(BF16), hsum=hsum, hexp=hsum.T, gnvec=jnp.stack([gn_g, gn_b], axis=1),
        sg_ln=jnp.stack([sg_ln_g, sg_ln_b], axis=1), sgw2=sgw2.astype(BF16), sg_bias=sg_bias,
        na_bias=jnp.stack([_na_bias_table(rpb[l], rows) for l in range(l_)]),
        p_a=p_a.astype(BF16), p_b=p_b.astype(BF16), p_c=p_c.astype(BF16), w_out=w_out.astype(BF16),
        ln_mix=jnp.stack([ln_mix_g, ln_mix_b], axis=1), w_router_t=jnp.swapaxes(w_router, 1, 2),
        e_gate=e_gate.astype(BF16), e_up=e_up.astype(BF16), e_down=e_down.astype(BF16),
        ln_ffn=jnp.stack([ln_ffn_g, ln_ffn_b], axis=1))


def _mixer(x, x_bf, p, l, b, t):
    n = b * t
    proj = _matmul(x_bf, p["w_in"][l], 512, 1024)
    proj3 = proj.reshape(b, t, IN_PAD)
    r, v, kk, lwf, kf, bf, lwb, kb, bb, bonus, g = _rwkv_prep(
        proj3, p["vec"][l], p["lvec"][l], p["w2f"][l], p["a2f"][l], p["g2f"][l], p["hsum"], p["hexp"])
    yf, yb = _rwkv_scan(r, v, kk, lwf, kf, bf, lwb, kb, bb)
    flat = lambda a: a.reshape(n, a.shape[-1])
    ya = _rwkv_post(flat(yf), flat(yb), flat(bonus), flat(g), p["gnvec"][l], p["hsum"], p["hexp"])
    ybr = _spatial_gating(proj3, p["sg_ln"][l], p["sgw2"][l], p["sg_bias"][l])
    ycr = _neighbourhood_attention(proj3, p["na_bias"][l])
    m = _merge(ya, flat(ybr), flat(ycr), proj, p["p_a"][l], p["p_b"][l], p["p_c"][l])
    return _out_proj_ln(m, p["w_out"][l], x, p["ln_mix"][l])


def _trunk(x3, p):
    b, t, _ = x3.shape
    x = x3.reshape(b * t, D_MODEL)
    x_bf = x.astype(BF16)
    for l in range(DEPTH):
        x, x_bf = _mixer(x, x_bf, p, l, b, t)
        x, x_bf = _expert_choice_moe_ln(x, x_bf, p["w_router_t"][l], p["e_gate"][l], p["e_up"][l],
                                        p["e_down"][l], p["ln_ffn"][l])
    return x.reshape(b, t, D_MODEL)


def kernel(x_prompt, x_sample, w_in, mu_prev, mu_next, decay_w0, decay_w2, iclr_a0, iclr_a2, gate_g2, k_k, k_a, r_k, gn_g, gn_b, sg_ln_g, sg_ln_b, sg_w, sg_b, rpb, p_a, p_b, p_c, w_out, ln_mix_g, ln_mix_b, w_router, e_gate, e_up, e_down, ln_ffn_g, ln_ffn_b):
    assert x_prompt.shape[1] == x_sample.shape[1]
    rows = x_prompt.shape[1] // GRID_W
    p = _prepare(w_in, mu_prev, mu_next, decay_w0, decay_w2, iclr_a0, iclr_a2, gate_g2, k_k, k_a, r_k, gn_g, gn_b,
                 sg_ln_g, sg_ln_b, sg_w, sg_b, rpb, p_a, p_b, p_c, w_out, ln_mix_g, ln_mix_b, w_router, e_gate,
                 e_up, e_down, ln_ffn_g, ln_ffn_b, rows)
    return (_trunk(x_prompt, p), _trunk(x_sample, p))
```

```python
import functools
from typing import NamedTuple

import numpy as np
import jax
import jax.numpy as jnp
from jax import lax
from jax.experimental import pallas as pl
from jax.experimental.pallas import tpu as pltpu

F32 = jnp.float32
BF16 = jnp.bfloat16
I32 = jnp.int32

D_MODEL = 2048
DEPTH = 4
GRID_W = 64
A_HEADS = 16
A_HEAD_DIM = 64
A_WIDTH = A_HEADS * A_HEAD_DIM
A_DECAY_LORA = 64
A_ICLR_LORA = 64
A_GATE_LORA = 160
B_GROUPS = 8
B_GROUP_DIM = 64
B_WIDTH = B_GROUPS * B_GROUP_DIM
B_CHUNK = 128
C_HEADS = 8
C_HEAD_DIM = 64
C_WIDTH = C_HEADS * C_HEAD_DIM
C_WIN_ROWS = 8
C_WIN_COLS = 16
N_EXPERTS = 16
EXPERT_HIDDEN = 1024
EC_CAPACITY_FACTOR = 2
A_COLS = 3 * A_WIDTH + 2 * A_DECAY_LORA + 2 * A_ICLR_LORA + A_GATE_LORA
B_COLS = 2 * B_WIDTH
C_COLS = 3 * C_WIDTH
G_COLS = 3 * D_MODEL
DEEPNORM_ALPHA = (2 * DEPTH) ** 0.25
LN_EPS = 1e-5
GN_EPS = 64e-5

LORA_COLS = 2 * A_DECAY_LORA + 2 * A_ICLR_LORA + A_GATE_LORA
LORA_PAD = 512
COL_R, COL_K, COL_V = 0, A_WIDTH, 2 * A_WIDTH
COL_B = 3 * A_WIDTH
COL_G = COL_B + B_COLS
COL_C = COL_G + G_COLS
COL_L = COL_C + C_COLS
IN_PAD = COL_L + LORA_PAD

LANES = 128
SCAN_CHUNK = 64
MOE_TILE = 256
MOE_SUB = 64
MOE_GROUP = 4
VMEM_LIMIT = 48 * 1024 * 1024


def _cparams(sem):
    return pltpu.CompilerParams(dimension_semantics=sem, vmem_limit_bytes=VMEM_LIMIT)


def _dot(a, b):
    return jnp.dot(a.astype(BF16), b.astype(BF16), preferred_element_type=F32)


def _dot_nt(a, b):
    return lax.dot_general(a.astype(BF16), b.astype(BF16), (((1,), (1,)), ((), ())),
                           preferred_element_type=F32)


def _split2(x):
    hi = x.astype(BF16)
    lo = (x - hi.astype(F32)).astype(BF16)
    return hi, lo


def _split3(x):
    hi = x.astype(BF16)
    r1 = x - hi.astype(F32)
    mid = r1.astype(BF16)
    lo = (r1 - mid.astype(F32)).astype(BF16)
    return hi, mid, lo


def _dot_exact01(m01, x):
    hi, mid, lo = _split3(x)
    m = m01.astype(BF16)
    return (jnp.dot(m, hi, preferred_element_type=F32) + jnp.dot(m, mid, preferred_element_type=F32)
            + jnp.dot(m, lo, preferred_element_type=F32))


def _dot_x01(x, m01):
    hi, mid, lo = _split3(x)
    m = m01.astype(BF16)
    return (jnp.dot(hi, m, preferred_element_type=F32) + jnp.dot(mid, m, preferred_element_type=F32)
            + jnp.dot(lo, m, preferred_element_type=F32))


def _head_sums(x, hsum, hexp):
    return _dot_x01(_dot_x01(x, hsum), hexp)


def _sigmoid(x):
    return 1.0 / (1.0 + jnp.exp(-x))


def _layer_norm_rows(z, g, b):
    mu = jnp.mean(z, axis=-1, keepdims=True)
    d = z - mu
    var = jnp.mean(d * d, axis=-1, keepdims=True)
    return d * lax.rsqrt(var + LN_EPS) * g + b


def _mm_kernel(a_ref, w_ref, o_ref):
    o_ref[...] = jnp.dot(a_ref[...], w_ref[...], preferred_element_type=F32).astype(o_ref.dtype)


def _matmul(a, w, tm, tn, out_dtype=F32):
    m, k = a.shape
    n = w.shape[1]
    tm = min(tm, m)
    return pl.pallas_call(
        _mm_kernel,
        grid=(n // tn, m // tm),
        in_specs=[pl.BlockSpec((tm, k), lambda j, i: (i, 0)),
                  pl.BlockSpec((k, tn), lambda j, i: (0, j))],
        out_specs=pl.BlockSpec((tm, tn), lambda j, i: (i, j)),
        out_shape=jax.ShapeDtypeStruct((m, n), out_dtype),
        compiler_params=_cparams(("parallel", "parallel")),
        name="in_proj",
    )(a, w)


PREP_TT = 128
V_MUP_R, V_MUP_K, V_MUP_V, V_MUN_R, V_MUN_K, V_MUN_V, V_KK, V_KA, V_RK, V_W0F, V_W0B, V_A0F, V_A0B = range(13)


def _shift(cur, prv8, nxt8, mup, mun, i, nt):
    tt = cur.shape[0]
    row = lax.broadcasted_iota(I32, cur.shape, 0)
    first = jnp.where(i > 0, prv8[7:8, :], 0.0)
    last = jnp.where(i < nt - 1, nxt8[0:1, :], 0.0)
    prev = jnp.where(row == 0, first, pltpu.roll(cur, 1, 0))
    nxt = jnp.where(row == tt - 1, last, pltpu.roll(cur, tt - 1, 0))
    return cur + mup * (prev - cur) + mun * (nxt - cur)


def _prep_kernel(rc, kc, vc, lc, rp, kp, vp, lp, rn, kn, vn, ln_, vec_ref, lvec_ref, w2_ref, a2_ref, g2_ref,
                 hs_ref, he_ref, r_o, v_o, kk_o, lwf_o, kf_o, bf_o, lwb_o, kb_o, bb_o, bonus_o, g_o):
    i = pl.program_id(1)
    nt = pl.num_programs(1)
    vec = vec_ref[...]
    row = lambda j: vec[j:j + 1, :]
    r = _shift(rc[0], rp[0], rn[0], row(V_MUP_R), row(V_MUN_R), i, nt)
    k = _shift(kc[0], kp[0], kn[0], row(V_MUP_K), row(V_MUN_K), i, nt)
    v = _shift(vc[0], vp[0], vn[0], row(V_MUP_V), row(V_MUN_V), i, nt)
    lo = _shift(lc[0], lp[0], ln_[0], lvec_ref[0:1, :], lvec_ref[1:2, :], i, nt)

    hsum, hexp = hs_ref[...], he_ref[...]
    kk = k * row(V_KK)
    kk = kk / jnp.maximum(jnp.sqrt(_head_sums(kk * kk, hsum, hexp)), 1e-12)

    tanh_lo = jnp.tanh(lo)
    ksum = jnp.zeros_like(k)
    outs = ((lwf_o, kf_o, bf_o), (lwb_o, kb_o, bb_o))
    for d in range(2):
        z = row(V_W0F + d) + _dot(tanh_lo, w2_ref[d])
        nz = -z
        softplus = jnp.maximum(nz, 0.0) + jnp.log(1.0 + jnp.exp(-jnp.abs(nz)))
        w_log = -softplus - 0.5
        lw = -jnp.exp(w_log)
        a = _sigmoid(row(V_A0F + d) + _dot(lo, a2_ref[d]))
        k_d = k * (1.0 + (a - 1.0) * row(V_KA))
        ksum = ksum + k_d
        lw_o, kd_o, bd_o = outs[d]
        lw_o[0] = lw
        kd_o[0] = k_d
        bd_o[0] = kk * a
    r_o[0] = r
    v_o[0] = v
    kk_o[0] = kk
    bonus_o[0] = _head_sums(r * ksum * row(V_RK), hsum, hexp) * v
    g_o[0] = _dot(_sigmoid(lo), g2_ref[...])


def _rwkv_prep(proj3, vec, lvec, w2f, a2f, g2f, hsum, hexp):
    b, t, _ = proj3.shape
    tt = PREP_TT
    nt = t // tt
    h8 = tt // 8
    cur = lambda cb, w: pl.BlockSpec((1, tt, w), lambda bi, i: (bi, i, cb))
    prv = lambda cb, w: pl.BlockSpec((1, 8, w), lambda bi, i: (bi, jnp.maximum(i * h8 - 1, 0), cb))
    nxt = lambda cb, w: pl.BlockSpec((1, 8, w), lambda bi, i: (bi, jnp.minimum((i + 1) * h8, t // 8 - 1), cb))
    cols = [(COL_R // A_WIDTH, A_WIDTH), (COL_K // A_WIDTH, A_WIDTH), (COL_V // A_WIDTH, A_WIDTH),
            (COL_L // LORA_PAD, LORA_PAD)]
    full = lambda shape: pl.BlockSpec(shape, lambda bi, i: (0,) * len(shape))
    in_specs = ([cur(*c) for c in cols] + [prv(*c) for c in cols] + [nxt(*c) for c in cols]
                + [full(vec.shape), full(lvec.shape), full(w2f.shape), full(a2f.shape), full(g2f.shape),
                   full(hsum.shape), full(hexp.shape)])
    out_spec = pl.BlockSpec((1, tt, A_WIDTH), lambda bi, i: (bi, i, 0))
    out_sds = jax.ShapeDtypeStruct((b, t, A_WIDTH), F32)
    return pl.pallas_call(
        _prep_kernel,
        grid=(b, nt),
        in_specs=in_specs,
        out_specs=[out_spec] * 11,
        out_shape=[out_sds] * 11,
        compiler_params=_cparams(("parallel", "parallel")),
        name="rwkv_prep",
    )(*([proj3] * 12), vec, lvec, w2f, a2f, g2f, hsum, hexp)


def _stack2(x, m0):
    return jnp.concatenate([jnp.where(m0, x, 0.0), jnp.where(m0, 0.0, x)], axis=0)


class _Chain(NamedTuple):
    r: jax.Array
    v: jax.Array
    kk: jax.Array
    lw: jax.Array
    kd: jax.Array
    bd: jax.Array
    s_ref: object
    rev: bool


def _scan_masks(c, reverse):
    ti = lax.broadcasted_iota(I32, (c, c), 0)
    si = lax.broadcasted_iota(I32, (c, c), 1)
    tri = jnp.where((si >= ti) if reverse else (si <= ti), 1.0, 0.0).astype(BF16)
    t2 = lax.broadcasted_iota(I32, (c, 2 * c), 0)
    s2 = lax.broadcasted_iota(I32, (c, 2 * c), 1)
    s2 = jnp.where(s2 >= c, s2 - c, s2)
    strict = (s2 > t2) if reverse else (s2 < t2)
    incl = (s2 >= t2) if reverse else (s2 <= t2)
    eye2 = jnp.where(s2 == t2, 1.0, 0.0)
    return tri, strict, incl, eye2


def _scan_chunk(chains):
    c = SCAN_CHUNK
    masks = {rev: _scan_masks(c, rev) for rev in (False, True)}
    m0 = lax.broadcasted_iota(I32, (c, LANES), 1) < A_HEAD_DIM
    half = lax.broadcasted_iota(I32, (c, 2 * c), 1) < c
    vi = lax.broadcasted_iota(I32, (LANES, LANES), 0)
    ki = lax.broadcasted_iota(I32, (LANES, LANES), 1)
    same_head = (vi < A_HEAD_DIM) == (ki < A_HEAD_DIM)
    bf = lambda x: x.astype(BF16)
    mm = lambda a, b: jnp.dot(a, b, preferred_element_type=F32)
    mm_nt = lambda a, b: lax.dot_general(a, b, (((1,), (1,)), ((), ())), preferred_element_type=F32)
    cat = jnp.concatenate

    cum = []
    for ch in chains:
        tri = masks[ch.rev][0]
        hi, mid, lo = _split3(ch.lw)
        cum.append(mm(tri, hi) + mm(tri, mid) + mm(tri, lo))

    ar, bk2, bkw, v2, etot = [], [], [], [], []
    for ch, cm in zip(chains, cum):
        tot = cm[0:1, :] if ch.rev else cm[c - 1:c, :]
        einv = jnp.exp(-cm)
        etail = jnp.exp(tot - cm)
        at = -ch.kk * jnp.exp(cm - ch.lw)
        rt = ch.r * jnp.exp(cm)
        ar.append(bf(cat([at, rt], axis=0)))
        bk2.append(bf(cat([_stack2(ch.bd * einv, m0), _stack2(ch.kd * einv, m0)], axis=0)))
        bkw.append(bf(cat([ch.bd * etail, ch.kd * etail], axis=0)))
        v2.append(bf(_stack2(ch.v, m0)))
        etot.append(jnp.exp(tot))

    mt = [mm_nt(a, b) for a, b in zip(ar, bk2)]
    s0 = [ch.s_ref[...] for ch in chains]
    ars = [mm_nt(a, bf(s)) for a, s in zip(ar, s0)]
    a_ab, a_ak, a_rbk = [], [], []
    for ch, m in zip(chains, mt):
        _, strict, incl, _ = masks[ch.rev]
        a_ab.append(jnp.where(strict, m[:c, :2 * c], 0.0))
        a_ak.append(bf(jnp.where(strict, m[:c, 2 * c:], 0.0)))
        a_rbk.append(bf(cat([jnp.where(incl, m[c:, :2 * c], 0.0), jnp.where(incl, m[c:, 2 * c:], 0.0)], axis=1)))
    akv = [mm(a, v) for a, v in zip(a_ak, v2)]

    tm = [masks[ch.rev][3] + a for ch, a in zip(chains, a_ab)]
    pw = [mm(bf(a), _stack2(bf(a), half)) for a in a_ab]
    steps = int(np.log2(c)) - 1
    for k in range(steps):
        blk = [_stack2(bf(x), half) for x in pw]
        tm = [t + mm(bf(t), d) for t, d in zip(tm, blk)]
        if k < steps - 1:
            pw = [mm(bf(x), d) for x, d in zip(pw, blk)]

    u = [mm(bf(t), bf(_stack2(a[:c] + k, m0))) for t, a, k in zip(tm, ars, akv)]
    y = [a[c:] + mm(ab, cat([bf(_stack2(uu, m0)), vv], axis=0)) for a, ab, uu, vv in zip(ars, a_rbk, u, v2)]
    upd = [mm(bf(cat([uu, ch.v], axis=0).T), w) for uu, ch, w in zip(u, chains, bkw)]
    for ch, s, e, up in zip(chains, s0, etot, upd):
        ch.s_ref[...] = s * e + jnp.where(same_head, up, 0.0)
    return y


SCAN_HP = 8


def _scan_kernel(rf, vf, kkf, lwf, kf, bf, rb, vb, kkb, lwb, kb, bb, yf_o, yb_o, sf_ref, sb_ref):
    @pl.when(pl.program_id(2) == 0)
    def _():
        sf_ref[...] = jnp.zeros_like(sf_ref)
        sb_ref[...] = jnp.zeros_like(sb_ref)

    chains = []
    for hp in range(SCAN_HP):
        ln = slice(hp * LANES, (hp + 1) * LANES)
        chains.append(_Chain(rf[0, :, ln], vf[0, :, ln], kkf[0, :, ln], lwf[0, :, ln], kf[0, :, ln], bf[0, :, ln],
                             sf_ref.at[hp], False))
        chains.append(_Chain(rb[0, :, ln], vb[0, :, ln], kkb[0, :, ln], lwb[0, :, ln], kb[0, :, ln], bb[0, :, ln],
                             sb_ref.at[hp], True))
    y = _scan_chunk(chains)
    for hp in range(SCAN_HP):
        ln = slice(hp * LANES, (hp + 1) * LANES)
        yf_o[0, :, ln] = y[2 * hp]
        yb_o[0, :, ln] = y[2 * hp + 1]


def _rwkv_scan(r, v, kk, lwf, kf, bf, lwb, kb, bb):
    b, t, _ = r.shape
    c = SCAN_CHUNK
    nc = t // c
    w = SCAN_HP * LANES
    fwd = pl.BlockSpec((1, c, w), lambda bi, hp, ci: (bi, ci, hp))
    bwd = pl.BlockSpec((1, c, w), lambda bi, hp, ci: (bi, nc - 1 - ci, hp))
    out_sds = jax.ShapeDtypeStruct((b, t, A_WIDTH), F32)
    return pl.pallas_call(
        _scan_kernel,
        grid=(b, A_WIDTH // w, nc),
        in_specs=[fwd] * 6 + [bwd] * 6,
        out_specs=[fwd, bwd],
        out_shape=[out_sds, out_sds],
        scratch_shapes=[pltpu.VMEM((SCAN_HP, LANES, LANES), F32), pltpu.VMEM((SCAN_HP, LANES, LANES), F32)],
        compiler_params=_cparams(("parallel", "parallel", "arbitrary")),
        name="rwkv_scan",
    )(r, v, kk, lwf, kf, bf, r, v, kk, lwb, kb, bb)


POST_TT = 256


def _post_kernel(yf, yb, bonus, g, vec_ref, hs_ref, he_ref, o_ref):
    y = yf[...] + yb[...]
    hsum, hexp = hs_ref[...], he_ref[...]
    inv = 1.0 / A_HEAD_DIM
    mu = _head_sums(y, hsum, hexp) * inv
    d = y - mu
    var = _head_sums(d * d, hsum, hexp) * inv
    yn = d * lax.rsqrt(var + GN_EPS) * vec_ref[0:1, :] + vec_ref[1:2, :]
    o_ref[...] = ((yn + bonus[...]) * g[...]).astype(o_ref.dtype)


def _rwkv_post(yf, yb, bonus, g, gnvec, hsum, hexp):
    n = yf.shape[0]
    tt = min(POST_TT, n)
    blk = pl.BlockSpec((tt, A_WIDTH), lambda i: (i, 0))
    full = lambda a: pl.BlockSpec(a.shape, lambda i: (0,) * a.ndim)
    return pl.pallas_call(
        _post_kernel,
        grid=(n // tt,),
        in_specs=[blk, blk, blk, blk, full(gnvec), full(hsum), full(hexp)],
        out_specs=blk,
        out_shape=jax.ShapeDtypeStruct((n, A_WIDTH), BF16),
        compiler_params=_cparams(("parallel",)),
        name="rwkv_post",
    )(yf, yb, bonus, g, gnvec, hsum, hexp)


def _gelu_tanh(x):
    return 0.5 * x * (1.0 + jnp.tanh(np.sqrt(2.0 / np.pi).astype(np.float32) * (x + 0.044715 * (x * x * x))))


def _sg_kernel(pb_ref, lnv_ref, w_ref, bias_ref, o_ref):
    z = _gelu_tanh(pb_ref[0])
    u = z[:, :B_WIDTH]
    v = _layer_norm_rows(z[:, B_WIDTH:], lnv_ref[0:1, :], lnv_ref[1:2, :])
    lane = lax.broadcasted_iota(I32, (B_CHUNK, LANES), 1)
    m0 = lane < B_GROUP_DIM
    parts = []
    for q in range(B_WIDTH // LANES):
        vq = v[:, q * LANES:(q + 1) * LANES]
        parts.append(_dot(w_ref[q], _stack2(vq, m0)))
    mixed = jnp.concatenate(parts, axis=1) + bias_ref[...]
    o_ref[0] = (u * mixed).astype(o_ref.dtype)


def _spatial_gating(proj3, lnvec, w2, bias_full):
    b, t, _ = proj3.shape
    full = lambda a: pl.BlockSpec(a.shape, lambda bi, i: (0,) * a.ndim)
    return pl.pallas_call(
        _sg_kernel,
        grid=(b, t // B_CHUNK),
        in_specs=[pl.BlockSpec((1, B_CHUNK, B_COLS), lambda bi, i: (bi, i, COL_B // B_COLS)),
                  full(lnvec), full(w2), full(bias_full)],
        out_specs=pl.BlockSpec((1, B_CHUNK, B_WIDTH), lambda bi, i: (bi, i, 0)),
        out_shape=jax.ShapeDtypeStruct((b, t, B_WIDTH), BF16),
        compiler_params=_cparams(("parallel", "parallel")),
        name="spatial_gating",
    )(proj3, lnvec, w2, bias_full)


NA_WIN = C_WIN_ROWS * GRID_W


def _na_row_start(i, rows):
    return jnp.clip(i - C_WIN_ROWS // 2, 0, rows - C_WIN_ROWS)


NA_QR = 2


def _na_kernel(q_ref, k_ref, v_ref, *rest, rows):
    bias_refs, o_ref = rest[:NA_QR], rest[NA_QR]
    i = pl.program_id(1)
    scale = C_HEAD_DIM ** -0.5
    qs, ks, vs, bs = [], [], [], []
    for j in range(NA_QR):
        start = pl.multiple_of(_na_row_start(i * NA_QR + j, rows) * GRID_W, GRID_W)
        q = q_ref[0, j * GRID_W:(j + 1) * GRID_W, :]
        kwin = k_ref[0, pl.ds(start, NA_WIN), :]
        vwin = v_ref[0, pl.ds(start, NA_WIN), :]
        for h in range(C_HEADS):
            sl = slice(h * C_HEAD_DIM, (h + 1) * C_HEAD_DIM)
            qs.append(q[:, sl].astype(BF16))
            ks.append(kwin[:, sl].astype(BF16))
            vs.append(vwin[:, sl].astype(BF16))
            bs.append(bias_refs[j][0, h])
    nt = lambda a, b: lax.dot_general(a, b, (((1,), (1,)), ((), ())), preferred_element_type=F32)
    ss = [nt(q, k) * scale + bias for q, k, bias in zip(qs, ks, bs)]
    ps = []
    for s in ss:
        e = jnp.exp(s - jnp.max(s, axis=-1, keepdims=True))
        ps.append((e / jnp.sum(e, axis=-1, keepdims=True)).astype(BF16))
    os_ = [jnp.dot(p, v, preferred_element_type=F32) for p, v in zip(ps, vs)]
    for j in range(NA_QR):
        o_ref[0, j * GRID_W:(j + 1) * GRID_W, :] = jnp.concatenate(
            os_[j * C_HEADS:(j + 1) * C_HEADS], axis=1).astype(o_ref.dtype)


def _neighbourhood_attention(proj3, bias_tab):
    b, t, _ = proj3.shape
    rows = t // GRID_W
    cq = COL_C // C_WIDTH
    qr = NA_QR
    seq = lambda cb: pl.BlockSpec((1, t, C_WIDTH), lambda bi, i: (bi, 0, cb))
    bias = lambda j: pl.BlockSpec((1, C_HEADS, GRID_W, NA_WIN),
                                  lambda bi, i: (i * qr + j - _na_row_start(i * qr + j, rows), 0, 0, 0))
    return pl.pallas_call(
        functools.partial(_na_kernel, rows=rows),
        grid=(b, rows // qr),
        in_specs=[pl.BlockSpec((1, qr * GRID_W, C_WIDTH), lambda bi, i: (bi, i, cq)), seq(cq + 1), seq(cq + 2)]
                 + [bias(j) for j in range(qr)],
        out_specs=pl.BlockSpec((1, qr * GRID_W, C_WIDTH), lambda bi, i: (bi, i, 0)),
        out_shape=jax.ShapeDtypeStruct((b, t, C_WIDTH), BF16),
        compiler_params=_cparams(("parallel", "arbitrary")),
        name="nbr_attention",
    )(proj3, proj3, proj3, *([bias_tab] * qr))


def _na_bias_table(rpb, rows):
    kc = C_WIN_COLS
    cols = np.arange(GRID_W)
    col_start = np.clip(cols - kc // 2, 0, GRID_W - kc)
    key_col = np.arange(GRID_W)
    in_win = (key_col[None, :] >= col_start[:, None]) & (key_col[None, :] < col_start[:, None] + kc)
    delta = np.arange(C_WIN_ROWS)
    row_off = np.arange(C_WIN_ROWS)[None, :] - delta[:, None] + (C_WIN_ROWS - 1)
    by_row = rpb[:, row_off]
    span = 2 * GRID_W
    lead = GRID_W - C_WIN_COLS
    padded = jnp.pad(by_row, [(0, 0)] * 3 + [(lead, span - lead - (2 * C_WIN_COLS - 1))])
    skew = jnp.tile(padded, GRID_W)[..., :GRID_W * (span - 1)].reshape(by_row.shape[:3] + (GRID_W, span - 1))
    bias = skew[..., GRID_W - 1:]
    bias = jnp.where(in_win, bias, -1e30)
    bias = jnp.transpose(bias, (1, 0, 3, 2, 4)).reshape(C_WIN_ROWS, C_HEADS, GRID_W, NA_WIN)
    return bias.astype(F32)


MERGE_TM = 256


def _merge_kernel(ya, yb, yc, ga, gb, gc, pa, pb, pc, o_ref):
    m = _sigmoid(ga[...]) * jnp.dot(ya[...], pa[...], preferred_element_type=F32)
    m = m + _sigmoid(gb[...]) * jnp.dot(yb[...], pb[...], preferred_element_type=F32)
    m = m + _sigmoid(gc[...]) * jnp.dot(yc[...], pc[...], preferred_element_type=F32)
    o_ref[...] = m.astype(o_ref.dtype)


def _merge(ya, yb, yc, proj, p_a, p_b, p_c):
    n = ya.shape[0]
    tm = min(MERGE_TM, n)
    rowblk = lambda w: pl.BlockSpec((tm, w), lambda i: (i, 0))
    gate = lambda j: pl.BlockSpec((tm, D_MODEL), lambda i: (i, COL_G // D_MODEL + j))
    full = lambda a: pl.BlockSpec(a.shape, lambda i: (0,) * a.ndim)
    return pl.pallas_call(
        _merge_kernel,
        grid=(n // tm,),
        in_specs=[rowblk(A_WIDTH), rowblk(B_WIDTH), rowblk(C_WIDTH), gate(0), gate(1), gate(2),
                  full(p_a), full(p_b), full(p_c)],
        out_specs=rowblk(D_MODEL),
        out_shape=jax.ShapeDtypeStruct((n, D_MODEL), BF16),
        compiler_params=_cparams(("parallel",)),
        name="branch_merge",
    )(ya, yb, yc, proj, proj, proj, p_a, p_b, p_c)


OUT_TM = 256


def _outln_kernel(m_ref, w_ref, x_ref, lnv_ref, o_ref, ob_ref):
    h = jnp.dot(m_ref[...], w_ref[...], preferred_element_type=F32)
    y = _layer_norm_rows(DEEPNORM_ALPHA * x_ref[...] + h, lnv_ref[0:1, :], lnv_ref[1:2, :])
    o_ref[...] = y
    ob_ref[...] = y.astype(BF16)


def _out_proj_ln(m, w_out, x, lnvec):
    n = m.shape[0]
    tm = min(OUT_TM, n)
    blk = pl.BlockSpec((tm, D_MODEL), lambda i: (i, 0))
    full = lambda a: pl.BlockSpec(a.shape, lambda i: (0,) * a.ndim)
    return pl.pallas_call(
        _outln_kernel,
        grid=(n // tm,),
        in_specs=[blk, full(w_out), blk, full(lnvec)],
        out_specs=[blk, blk],
        out_shape=[jax.ShapeDtypeStruct((n, D_MODEL), F32), jax.ShapeDtypeStruct((n, D_MODEL), BF16)],
        compiler_params=_cparams(("parallel",)),
        name="out_proj_ln",
    )(m, w_out, x, lnvec)


ROUTER_TM = 512


def _router_kernel(x_ref, wt_ref, o_ref):
    xh, xl = _split2(x_ref[...])
    wh, wl = _split2(wt_ref[...])
    nt = lambda a, b: lax.dot_general(a, b, (((1,), (1,)), ((), ())), preferred_element_type=F32)
    logits = nt(wh, xh) + nt(wl, xh) + nt(wh, xl)
    logits = logits - jnp.max(logits, axis=0, keepdims=True)
    e = jnp.exp(logits)
    o_ref[...] = e / jnp.sum(e, axis=0, keepdims=True)


def _router(x, w_router_t):
    n = x.shape[0]
    tm = min(ROUTER_TM, n)
    return pl.pallas_call(
        _router_kernel,
        grid=(n // tm,),
        in_specs=[pl.BlockSpec((tm, D_MODEL), lambda i: (i, 0)),
                  pl.BlockSpec(w_router_t.shape, lambda i: (0, 0))],
        out_specs=pl.BlockSpec((N_EXPERTS, tm), lambda i: (0, i)),
        out_shape=jax.ShapeDtypeStruct((N_EXPERTS, n), F32),
        compiler_params=_cparams(("parallel",)),
        name="router",
    )(x, w_router_t)


def _select_kernel(aff_ref, pos_ref, off_ref, *, cap):
    aff = aff_ref[...]
    e_, g_, l_ = aff.shape
    n = g_ * l_
    bits = pltpu.bitcast(aff, I32)

    def count(mask):
        c = jnp.sum(jnp.where(mask, 1.0, 0.0), axis=2, keepdims=True)
        return jnp.sum(c, axis=1, keepdims=True)

    def thr_body(_, carry):
        lo, hi = carry
        mid = lo + (hi - lo + 1) // 2
        ok = count(bits >= mid) >= cap
        return jnp.where(ok, mid, lo), jnp.where(ok, hi, mid - 1)

    lo0 = jnp.zeros((e_, 1, 1), I32)
    hi0 = jnp.full((e_, 1, 1), 0x7F800000, I32)
    thr, _ = lax.fori_loop(0, 32, thr_body, (lo0, hi0))
    gt = bits > thr
    tie = bits == thr
    need = cap - count(gt)
    idx = lax.broadcasted_iota(I32, aff.shape, 1) * l_ + lax.broadcasted_iota(I32, aff.shape, 2)

    def idx_body(_, carry):
        lo, hi = carry
        mid = (lo + hi) // 2
        ok = count(tie & (idx <= mid)) >= need
        return jnp.where(ok, lo, mid + 1), jnp.where(ok, mid, hi)

    lo1 = jnp.zeros((e_, 1, 1), I32)
    hi1 = jnp.full((e_, 1, 1), n - 1, I32)
    cut, _ = lax.fori_loop(0, int(np.ceil(np.log2(n))) + 1, idx_body, (lo1, hi1))
    sel = jnp.where(gt | (tie & (idx <= cut)), 1.0, 0.0)

    sel2 = sel.reshape(e_ * g_, l_)
    ls = lax.broadcasted_iota(I32, (l_, l_), 0)
    lt = lax.broadcasted_iota(I32, (l_, l_), 1)
    incl = jnp.dot(sel2.astype(BF16), jnp.where(ls <= lt, 1.0, 0.0).astype(BF16), preferred_element_type=F32)
    tot = jnp.dot(sel2.astype(BF16), jnp.ones((l_, l_), BF16), preferred_element_type=F32)
    gs = lax.broadcasted_iota(I32, (g_, g_), 0)
    gt_ = lax.broadcasted_iota(I32, (g_, g_), 1)
    lower = jnp.where(gt_ < gs, 1.0, 0.0).astype(BF16)
    offs = [jnp.dot(lower, tot[e * g_:(e + 1) * g_].astype(BF16), preferred_element_type=F32)
            for e in range(e_)]
    off = jnp.concatenate(offs, axis=0)
    pos = incl + off - sel2
    pos_ref[...] = jnp.where(sel2 > 0.5, pos, -1.0).astype(I32).reshape(e_, g_, l_)
    off_ref[...] = off.astype(I32).reshape(e_, g_, l_)


def _select(aff3, cap):
    full = pl.BlockSpec(aff3.shape, lambda i: (0, 0, 0))
    sds = jax.ShapeDtypeStruct(aff3.shape, I32)
    return pl.pallas_call(
        functools.partial(_select_kernel, cap=cap),
        grid=(1,),
        in_specs=[full],
        out_specs=[full, full],
        out_shape=[sds, sds],
        compiler_params=_cparams(("arbitrary",)),
        name="expert_select",
    )(aff3)


def _gather_kernel(tile_s, valid_s, e_g, slab_g, first_g, *refs):
    k_ = MOE_GROUP
    x_refs, pos_refs, aff_refs = refs[:k_], refs[k_:2 * k_], refs[2 * k_:3 * k_]
    o_ref, g_ref = refs[3 * k_:]
    g = pl.program_id(0)
    t = MOE_TILE

    @pl.when(first_g[g] == 1)
    def _():
        o_ref[...] = jnp.zeros_like(o_ref)
        g_ref[...] = jnp.zeros_like(g_ref)

    nvalid = valid_s[g * k_]
    for k in range(1, k_):
        nvalid = nvalid + valid_s[g * k_ + k]

    @pl.when(nvalid > 0)
    def _():
        rank = lax.broadcasted_iota(I32, (t, t), 0) + slab_g[g] * t
        got = o_ref[0].astype(F32)
        gate = g_ref[0]
        for k in range(k_):
            want = jnp.where(valid_s[g * k_ + k] == 1, rank, -2)
            hit = pos_refs[k][0] == want
            got = got + jnp.dot(jnp.where(hit, 1.0, 0.0).astype(BF16), x_refs[k][...],
                                preferred_element_type=F32)
            gate = gate + jnp.sum(jnp.where(hit, aff_refs[k][0], 0.0), axis=1, keepdims=True)
        o_ref[0] = got.astype(o_ref.dtype)
        g_ref[0] = gate


def _moe_gather(x_bf, pos_en, aff_en, items, cap):
    n = x_bf.shape[0]
    t = MOE_TILE
    k_ = MOE_GROUP
    ng = items[2].shape[0]
    xs = [pl.BlockSpec((t, D_MODEL), lambda g, tl, va, e, s, fi, k=k: (tl[g * k_ + k], 0)) for k in range(k_)]
    rows = [pl.BlockSpec((1, 1, t), lambda g, tl, va, e, s, fi, k=k: (e[g], 0, tl[g * k_ + k])) for k in range(k_)]
    grid_spec = pltpu.PrefetchScalarGridSpec(
        num_scalar_prefetch=5,
        grid=(ng,),
        in_specs=xs + rows + rows,
        out_specs=[pl.BlockSpec((1, t, D_MODEL), lambda g, tl, va, e, s, fi: (e[g], s[g], 0)),
                   pl.BlockSpec((1, t, 1), lambda g, tl, va, e, s, fi: (e[g], s[g], 0))],
    )
    pos3 = pos_en.reshape(N_EXPERTS, 1, n)
    aff3 = aff_en.reshape(N_EXPERTS, 1, n)
    return pl.pallas_call(
        _gather_kernel,
        grid_spec=grid_spec,
        out_shape=[jax.ShapeDtypeStruct((N_EXPERTS, cap, D_MODEL), BF16),
                   jax.ShapeDtypeStruct((N_EXPERTS, cap, 1), F32)],
        compiler_params=_cparams(("arbitrary",)),
        name="moe_gather",
    )(*items, *([x_bf] * k_), *([pos3] * k_), *([aff3] * k_))


FFN_TM = 512


def _ffn_kernel(x_ref, gate_ref, wg_ref, wu_ref, wd_ref, hi_ref, lo_ref):
    x = x_ref[0]
    g = jnp.dot(x, wg_ref[0], preferred_element_type=F32)
    u = jnp.dot(x, wu_ref[0], preferred_element_type=F32)
    h = (g * _sigmoid(g)) * u
    y = jnp.dot(h.astype(BF16), wd_ref[0], preferred_element_type=F32) * gate_ref[0]
    hi, lo = _split2(y)
    hi_ref[0] = hi
    lo_ref[0] = lo


def _expert_ffn(xe, gate, wg, wu, wd):
    e_, cap, _ = xe.shape
    tm = min(FFN_TM, cap)
    wspec = lambda a: pl.BlockSpec((1,) + a.shape[1:], lambda e, i: (e, 0, 0))
    rows = lambda w: pl.BlockSpec((1, tm, w), lambda e, i: (e, i, 0))
    sds = jax.ShapeDtypeStruct((e_, cap, D_MODEL), BF16)
    return pl.pallas_call(
        _ffn_kernel,
        grid=(e_, cap // tm),
        in_specs=[rows(D_MODEL), rows(1), wspec(wg), wspec(wu), wspec(wd)],
        out_specs=[rows(D_MODEL), rows(D_MODEL)],
        out_shape=[sds, sds],
        compiler_params=_cparams(("parallel", "arbitrary")),
        name="expert_ffn",
    )(xe, gate, wg, wu, wd)


def _combine_kernel(e_s, slab_s, valid_s, tile_g, first_g, last_g, *refs):
    k_ = MOE_GROUP
    hi_refs, lo_refs = refs[:k_], refs[k_:2 * k_]
    pos_ref, x_ref, lnv_ref, o_ref, ob_ref, acc_ref = refs[2 * k_:]
    g = pl.program_id(0)
    t = MOE_TILE

    @pl.when(first_g[g] == 1)
    def _():
        acc_ref[...] = jnp.zeros_like(acc_ref)

    nvalid = valid_s[g * k_]
    for k in range(1, k_):
        nvalid = nvalid + valid_s[g * k_ + k]

    @pl.when(nvalid > 0)
    def _():
        posf = pos_ref[...].astype(F32)
        lane = lax.broadcasted_iota(I32, (t, N_EXPERTS), 1)
        row = lax.broadcasted_iota(I32, (t, MOE_SUB), 1)
        onehots = []
        for k in range(k_):
            s = g * k_ + k
            pos = jnp.sum(jnp.where(lane == e_s[s], posf, 0.0), axis=1, keepdims=True)
            base = jnp.where(valid_s[s] == 1, slab_s[s] * MOE_SUB, -2 * MOE_SUB)
            onehots.append(jnp.where(pos == (row + base).astype(F32), 1.0, 0.0).astype(BF16))
        onehot = jnp.concatenate(onehots, axis=1)
        hi = jnp.concatenate([r[0] for r in hi_refs], axis=0)
        lo = jnp.concatenate([r[0] for r in lo_refs], axis=0)
        acc_ref[...] += (jnp.dot(onehot, hi, preferred_element_type=F32)
                         + jnp.dot(onehot, lo, preferred_element_type=F32))

    @pl.when(last_g[g] == 1)
    def _():
        y = _layer_norm_rows(DEEPNORM_ALPHA * x_ref[...] + acc_ref[...], lnv_ref[0:1, :], lnv_ref[1:2, :])
        o_ref[...] = y
        ob_ref[...] = y.astype(BF16)


def _moe_combine(ye_hi, ye_lo, pos_ne, x, lnvec, items):
    n = x.shape[0]
    t = MOE_TILE
    k_ = MOE_GROUP
    ng = items[3].shape[0]
    tile_blk = lambda w: pl.BlockSpec((t, w), lambda g, e, s, va, tl, fi, la: (tl[g], 0))
    slabs = [pl.BlockSpec((1, MOE_SUB, D_MODEL), lambda g, e, s, va, tl, fi, la, k=k: (e[g * k_ + k], s[g * k_ + k], 0))
             for k in range(k_)]
    grid_spec = pltpu.PrefetchScalarGridSpec(
        num_scalar_prefetch=6,
        grid=(ng,),
        in_specs=slabs + slabs + [tile_blk(N_EXPERTS), tile_blk(D_MODEL),
                                  pl.BlockSpec(lnvec.shape, lambda g, e, s, va, tl, fi, la: (0, 0))],
        out_specs=[tile_blk(D_MODEL), tile_blk(D_MODEL)],
        scratch_shapes=[pltpu.VMEM((t, D_MODEL), F32)],
    )
    return pl.pallas_call(
        _combine_kernel,
        grid_spec=grid_spec,
        out_shape=[jax.ShapeDtypeStruct((n, D_MODEL), F32), jax.ShapeDtypeStruct((n, D_MODEL), BF16)],
        compiler_params=_cparams(("arbitrary",)),
        name="moe_combine",
    )(*items, *([ye_hi] * k_), *([ye_lo] * k_), pos_ne, x, lnvec)


def _moe_items(group_off, n, cap):
    t = MOE_TILE
    nt = n // t
    e_ = N_EXPERTS
    starts = group_off[:, ::t // LANES, 0]
    ends = jnp.concatenate([starts[:, 1:], jnp.full((e_, 1), cap, I32)], axis=1)
    big = jnp.int32(2 ** 30)

    def build(rows, by_tile):
        ns = cap // rows
        e_p = jnp.broadcast_to(jnp.arange(e_, dtype=I32)[:, None], (e_, nt))
        tile_p = jnp.broadcast_to(jnp.arange(nt, dtype=I32)[None, :], (e_, nt))
        slab_p = jnp.minimum(starts // rows, ns - 1)
        valid_p = (ends > starts).astype(I32)
        bound = jnp.arange(ns, dtype=I32) * rows
        tile_s = jnp.sum((starts[:, :, None] <= bound[None, None, :]).astype(I32), axis=1) - 1
        tile_s = jnp.clip(tile_s, 0, nt - 1)
        st_s = jnp.take_along_axis(starts, tile_s, axis=1)
        en_s = jnp.take_along_axis(ends, tile_s, axis=1)
        valid_s = ((st_s < bound[None, :]) & (bound[None, :] < en_s)).astype(I32)
        e_s = jnp.broadcast_to(jnp.arange(e_, dtype=I32)[:, None], (e_, ns))
        slab_s = jnp.broadcast_to(jnp.arange(ns, dtype=I32)[None, :], (e_, ns))
        cat = lambda a, b: jnp.concatenate([a.reshape(-1), b.reshape(-1)])
        e_a, slab_a, tile_a = cat(e_p, e_s), cat(slab_p, slab_s), cat(tile_p, tile_s)
        valid_a = cat(valid_p, valid_s)
        secondary = cat(jnp.zeros_like(e_p), jnp.ones_like(e_s))
        ni = e_a.shape[0]
        ar = jnp.arange(ni, dtype=I32)
        if by_tile:
            keep = jnp.maximum(valid_a, 1 - secondary)
            key = ((tile_a * e_ + e_a) * 2 + secondary) * ns + slab_a
        else:
            keep = valid_a
            key = (e_a * ns + slab_a) * nt + tile_a
        key = jnp.where(keep == 1, key, big + ar)
        rank = jnp.sum((key[None, :] < key[:, None]).astype(I32), axis=1)
        order = jnp.sum(jnp.where(rank[None, :] == ar[:, None], ar[None, :], 0), axis=1)
        nkeep = jnp.sum(keep)
        kept = ar < nkeep
        oe, os_, otl, ov = e_a[order], slab_a[order], tile_a[order], valid_a[order]
        blk = otl if by_tile else oe * ns + os_

        k_ = MOE_GROUP
        nblk = nt if by_tile else e_ * ns
        ng = -(-ni // k_) + nblk
        first = jnp.concatenate([jnp.ones((1,), bool), blk[1:] != blk[:-1]])
        run_start = lax.cummax(jnp.where(first, ar, 0))
        sub = (ar - run_start) % k_
        leads = (sub == 0) & kept
        gid = jnp.cumsum(leads.astype(I32)) - 1
        slot = jnp.where(kept, gid * k_ + sub, -1)
        sar = jnp.arange(ng * k_, dtype=I32)
        hit = slot[None, :] == sar[:, None]
        has = jnp.sum(hit.astype(I32), axis=1)
        idx = jnp.sum(jnp.where(hit, ar[None, :], 0), axis=1)
        n_real = jnp.sum(leads.astype(I32))
        lead = idx.reshape(ng, k_)[:, 0]
        lead = jnp.where(jnp.arange(ng, dtype=I32) < n_real, lead, lead[jnp.maximum(n_real - 1, 0)])
        idx = jnp.where(has == 1, idx, lead[sar // k_])
        gblk = blk[lead]
        change = (gblk[1:] != gblk[:-1]).astype(I32)
        one = jnp.ones((1,), I32)
        slots = dict(e=oe[idx], slab=os_[idx], tile=otl[idx], valid=ov[idx] * has)
        groups = dict(e=oe[lead], slab=os_[lead], tile=otl[lead], first=jnp.concatenate([one, change]),
                      last=jnp.concatenate([change, one]))
        return slots, groups

    gs, gg = build(MOE_TILE, False)
    cs, cg = build(MOE_SUB, True)
    return ((gs["tile"], gs["valid"], gg["e"], gg["slab"], gg["first"]),
            (cs["e"], cs["slab"], cs["valid"], cg["tile"], cg["first"], cg["last"]))


def _expert_choice_moe_ln(x, x_bf, w_router_t, wg, wu, wd, lnvec):
    n = x.shape[0]
    cap = EC_CAPACITY_FACTOR * n // N_EXPERTS
    aff_en = _router(x, w_router_t)
    pos3, off3 = _select(aff_en.reshape(N_EXPERTS, n // LANES, LANES), cap)
    pos_en = pos3.reshape(N_EXPERTS, n)
    g_items, c_items = _moe_items(off3, n, cap)
    xe, gate = _moe_gather(x_bf, pos_en, aff_en, g_items, cap)
    ye_hi, ye_lo = _expert_ffn(xe, gate, wg, wu, wd)
    return _moe_combine(ye_hi, ye_lo, pos_en.T, x, lnvec, c_items)


def _pack_cols(w):
    a_end = A_COLS
    b_end = a_end + B_COLS
    c_end = b_end + C_COLS
    pad = jnp.zeros(w.shape[:-1] + (LORA_PAD - LORA_COLS,), w.dtype)
    return jnp.concatenate([w[..., :3 * A_WIDTH], w[..., a_end:b_end], w[..., c_end:], w[..., b_end:c_end],
                            w[..., 3 * A_WIDTH:a_end], pad], axis=-1)


def _lora_rows(w, start):
    k = w.shape[-2]
    return jnp.pad(w, [(0, 0)] * (w.ndim - 2) + [(start, LORA_PAD - start - k), (0, 0)])


def _prepare(w_in, mu_prev, mu_next, decay_w0, decay_w2, iclr_a0, iclr_a2, gate_g2, k_k, k_a, r_k, gn_g, gn_b,
             sg_ln_g, sg_ln_b, sg_w, sg_b, rpb, p_a, p_b, p_c, w_out, ln_mix_g, ln_mix_b, w_router, e_gate, e_up,
             e_down, ln_ffn_g, ln_ffn_b, rows):
    l_ = w_in.shape[0]
    pad_a = lambda m: jnp.pad(m, ((0, 0), (0, LORA_PAD - LORA_COLS)))
    mup, mun = mu_prev, mu_next
    vec_rows = [mup[:, :A_WIDTH], mup[:, A_WIDTH:2 * A_WIDTH], mup[:, 2 * A_WIDTH:3 * A_WIDTH],
                mun[:, :A_WIDTH], mun[:, A_WIDTH:2 * A_WIDTH], mun[:, 2 * A_WIDTH:3 * A_WIDTH],
                k_k, k_a, r_k.reshape(l_, A_WIDTH), decay_w0[:, 0], decay_w0[:, 1], iclr_a0[:, 0], iclr_a0[:, 1]]
    vec = jnp.stack(vec_rows + [jnp.zeros_like(k_k)] * (16 - len(vec_rows)), axis=1)
    lvec = jnp.stack([pad_a(mup[:, 3 * A_WIDTH:]), pad_a(mun[:, 3 * A_WIDTH:])]
                     + [jnp.zeros((l_, LORA_PAD), F32)] * 6, axis=1)
    w2f = jnp.stack([_lora_rows(decay_w2[:, 0], 0), _lora_rows(decay_w2[:, 1], A_DECAY_LORA)], axis=1)
    a2f = jnp.stack([_lora_rows(iclr_a2[:, 0], 2 * A_DECAY_LORA),
                     _lora_rows(iclr_a2[:, 1], 2 * A_DECAY_LORA + A_ICLR_LORA)], axis=1)
    g2f = _lora_rows(gate_g2, 2 * A_DECAY_LORA + 2 * A_ICLR_LORA)
    head = np.arange(A_WIDTH) // A_HEAD_DIM
    hsum = jnp.asarray(head[:, None] == np.arange(LANES)[None, :], BF16)
    sgw2 = sg_w.reshape(l_, B_GROUPS // 2, 2, B_CHUNK, B_CHUNK).transpose(0, 1, 3, 2, 4)
    sgw2 = sgw2.reshape(l_, B_GROUPS // 2, B_CHUNK, 2 * B_CHUNK)
    sg_bias = jnp.repeat(jnp.swapaxes(sg_b, 1, 2), B_GROUP_DIM, axis=2)
    return dict(
        w_in=_pack_cols(w_in.astype(BF16)), vec=vec, lvec=lvec, w2f=w2f.astype(BF16), a2f=a2f.astype(BF16),
        g2f=g2f.astype(BF16), hsum=hsum, hexp=hsum.T, gnvec=jnp.stack([gn_g, gn_b], axis=1),
        sg_ln=jnp.stack([sg_ln_g, sg_ln_b], axis=1), sgw2=sgw2.astype(BF16), sg_bias=sg_bias,
        na_bias=jnp.stack([_na_bias_table(rpb[l], rows) for l in range(l_)]),
        p_a=p_a.astype(BF16), p_b=p_b.astype(BF16), p_c=p_c.astype(BF16), w_out=w_out.astype(BF16),
        ln_mix=jnp.stack([ln_mix_g, ln_mix_b], axis=1), w_router_t=jnp.swapaxes(w_router, 1, 2),
        e_gate=e_gate.astype(BF16), e_up=e_up.astype(BF16), e_down=e_down.astype(BF16),
        ln_ffn=jnp.stack([ln_ffn_g, ln_ffn_b], axis=1))


def _mixer(x, x_bf, p, l, b, t):
    n = b * t
    proj = _matmul(x_bf, p["w_in"][l], 512, 1024)
    proj3 = proj.reshape(b, t, IN_PAD)
    r, v, kk, lwf, kf, bf, lwb, kb, bb, bonus, g = _rwkv_prep(
        proj3, p["vec"][l], p["lvec"][l], p["w2f"][l], p["a2f"][l], p["g2f"][l], p["hsum"], p["hexp"])
    yf, yb = _rwkv_scan(r, v, kk, lwf, kf, bf, lwb, kb, bb)
    flat = lambda a: a.reshape(n, a.shape[-1])
    ya = _rwkv_post(flat(yf), flat(yb), flat(bonus), flat(g), p["gnvec"][l], p["hsum"], p["hexp"])
    ybr = _spatial_gating(proj3, p["sg_ln"][l], p["sgw2"][l], p["sg_bias"][l])
    ycr = _neighbourhood_attention(proj3, p["na_bias"][l])
    m = _merge(ya, flat(ybr), flat(ycr), proj, p["p_a"][l], p["p_b"][l], p["p_c"][l])
    return _out_proj_ln(m, p["w_out"][l], x, p["ln_mix"][l])


def _trunk(x3, p):
    b, t, _ = x3.shape
    x = x3.reshape(b * t, D_MODEL)
    x_bf = x.astype(BF16)
    for l in range(DEPTH):
        x, x_bf = _mixer(x, x_bf, p, l, b, t)
        x, x_bf = _expert_choice_moe_ln(x, x_bf, p["w_router_t"][l], p["e_gate"][l], p["e_up"][l],
                                        p["e_down"][l], p["ln_ffn"][l])
    return x.reshape(b, t, D_MODEL)


def kernel(x_prompt, x_sample, w_in, mu_prev, mu_next, decay_w0, decay_w2, iclr_a0, iclr_a2, gate_g2, k_k, k_a, r_k, gn_g, gn_b, sg_ln_g, sg_ln_b, sg_w, sg_b, rpb, p_a, p_b, p_c, w_out, ln_mix_g, ln_mix_b, w_router, e_gate, e_up, e_down, ln_ffn_g, ln_ffn_b):
    assert x_prompt.shape[1] == x_sample.shape[1]
    rows = x_prompt.shape[1] // GRID_W
    p = _prepare(w_in, mu_prev, mu_next, decay_w0, decay_w2, iclr_a0, iclr_a2, gate_g2, k_k, k_a, r_k, gn_g, gn_b,
                 sg_ln_g, sg_ln_b, sg_w, sg_b, rpb, p_a, p_b, p_c, w_out, ln_mix_g, ln_mix_b, w_router, e_gate,
                 e_up, e_down, ln_ffn_g, ln_ffn_b, rows)
    return (_trunk(x_prompt, p), _trunk(x_sample, p))
```

```python
import functools
from typing import NamedTuple

import numpy as np
import jax
import jax.numpy as jnp
from jax import lax
from jax.experimental import pallas as pl
from jax.experimental.pallas import tpu as pltpu

F32 = jnp.float32
BF16 = jnp.bfloat16
I32 = jnp.int32

D_MODEL = 2048
DEPTH = 4
GRID_W = 64
A_HEADS = 16
A_HEAD_DIM = 64
A_WIDTH = A_HEADS * A_HEAD_DIM
A_DECAY_LORA = 64
A_ICLR_LORA = 64
A_GATE_LORA = 160
B_GROUPS = 8
B_GROUP_DIM = 64
B_WIDTH = B_GROUPS * B_GROUP_DIM
B_CHUNK = 128
C_HEADS = 8
C_HEAD_DIM = 64
C_WIDTH = C_HEADS * C_HEAD_DIM
C_WIN_ROWS = 8
C_WIN_COLS = 16
N_EXPERTS = 16
EXPERT_HIDDEN = 1024
EC_CAPACITY_FACTOR = 2
A_COLS = 3 * A_WIDTH + 2 * A_DECAY_LORA + 2 * A_ICLR_LORA + A_GATE_LORA
B_COLS = 2 * B_WIDTH
C_COLS = 3 * C_WIDTH
G_COLS = 3 * D_MODEL
DEEPNORM_ALPHA = (2 * DEPTH) ** 0.25
LN_EPS = 1e-5
GN_EPS = 64e-5

LORA_COLS = 2 * A_DECAY_LORA + 2 * A_ICLR_LORA + A_GATE_LORA
LORA_PAD = 512
COL_R, COL_K, COL_V = 0, A_WIDTH, 2 * A_WIDTH
COL_B = 3 * A_WIDTH
COL_G = COL_B + B_COLS
COL_C = COL_G + G_COLS
COL_L = COL_C + C_COLS
IN_PAD = COL_L + LORA_PAD

LANES = 128
SCAN_CHUNK = 64
MOE_TILE = 256
MOE_SUB = 64
GATHER_GROUP = 4
COMBINE_GROUP = 8
VMEM_LIMIT = 48 * 1024 * 1024


def _cparams(sem):
    return pltpu.CompilerParams(dimension_semantics=sem, vmem_limit_bytes=VMEM_LIMIT)


def _dot(a, b):
    return jnp.dot(a.astype(BF16), b.astype(BF16), preferred_element_type=F32)


def _dot_nt(a, b):
    return lax.dot_general(a.astype(BF16), b.astype(BF16), (((1,), (1,)), ((), ())),
                           preferred_element_type=F32)


def _split2(x):
    hi = x.astype(BF16)
    lo = (x - hi.astype(F32)).astype(BF16)
    return hi, lo


def _split3(x):
    hi = x.astype(BF16)
    r1 = x - hi.astype(F32)
    mid = r1.astype(BF16)
    lo = (r1 - mid.astype(F32)).astype(BF16)
    return hi, mid, lo


def _dot_exact01(m01, x):
    hi, mid, lo = _split3(x)
    m = m01.astype(BF16)
    return (jnp.dot(m, hi, preferred_element_type=F32) + jnp.dot(m, mid, preferred_element_type=F32)
            + jnp.dot(m, lo, preferred_element_type=F32))


def _dot_x01(x, m01):
    hi, mid, lo = _split3(x)
    m = m01.astype(BF16)
    return (jnp.dot(hi, m, preferred_element_type=F32) + jnp.dot(mid, m, preferred_element_type=F32)
            + jnp.dot(lo, m, preferred_element_type=F32))


def _head_sums(x, hsum, hexp):
    return _dot_x01(_dot_x01(x, hsum), hexp)


def _sigmoid(x):
    return 1.0 / (1.0 + jnp.exp(-x))


def _layer_norm_rows(z, g, b):
    mu = jnp.mean(z, axis=-1, keepdims=True)
    d = z - mu
    var = jnp.mean(d * d, axis=-1, keepdims=True)
    return d * lax.rsqrt(var + LN_EPS) * g + b


def _mm_kernel(a_ref, w_ref, o_ref):
    o_ref[...] = jnp.dot(a_ref[...], w_ref[...], preferred_element_type=F32).astype(o_ref.dtype)


def _matmul(a, w, tm, tn, out_dtype=F32):
    m, k = a.shape
    n = w.shape[1]
    tm = min(tm, m)
    return pl.pallas_call(
        _mm_kernel,
        grid=(n // tn, m // tm),
        in_specs=[pl.BlockSpec((tm, k), lambda j, i: (i, 0)),
                  pl.BlockSpec((k, tn), lambda j, i: (0, j))],
        out_specs=pl.BlockSpec((tm, tn), lambda j, i: (i, j)),
        out_shape=jax.ShapeDtypeStruct((m, n), out_dtype),
        compiler_params=_cparams(("parallel", "parallel")),
        name="in_proj",
    )(a, w)


PREP_TT = 128
V_MUP_R, V_MUP_K, V_MUP_V, V_MUN_R, V_MUN_K, V_MUN_V, V_KK, V_KA, V_RK, V_W0F, V_W0B, V_A0F, V_A0B = range(13)


def _shift(cur, prv8, nxt8, mup, mun, i, nt):
    tt = cur.shape[0]
    row = lax.broadcasted_iota(I32, cur.shape, 0)
    first = jnp.where(i > 0, prv8[7:8, :], 0.0)
    last = jnp.where(i < nt - 1, nxt8[0:1, :], 0.0)
    prev = jnp.where(row == 0, first, pltpu.roll(cur, 1, 0))
    nxt = jnp.where(row == tt - 1, last, pltpu.roll(cur, tt - 1, 0))
    return cur + mup * (prev - cur) + mun * (nxt - cur)


def _prep_kernel(rc, kc, vc, lc, rp, kp, vp, lp, rn, kn, vn, ln_, vec_ref, lvec_ref, w2_ref, a2_ref, g2_ref,
                 hs_ref, he_ref, r_o, v_o, kk_o, lwf_o, kf_o, bf_o, lwb_o, kb_o, bb_o, bonus_o, g_o):
    i = pl.program_id(1)
    nt = pl.num_programs(1)
    vec = vec_ref[...]
    row = lambda j: vec[j:j + 1, :]
    r = _shift(rc[0], rp[0], rn[0], row(V_MUP_R), row(V_MUN_R), i, nt)
    k = _shift(kc[0], kp[0], kn[0], row(V_MUP_K), row(V_MUN_K), i, nt)
    v = _shift(vc[0], vp[0], vn[0], row(V_MUP_V), row(V_MUN_V), i, nt)
    lo = _shift(lc[0], lp[0], ln_[0], lvec_ref[0:1, :], lvec_ref[1:2, :], i, nt)

    hsum, hexp = hs_ref[...], he_ref[...]
    kk = k * row(V_KK)
    kk = kk / jnp.maximum(jnp.sqrt(_head_sums(kk * kk, hsum, hexp)), 1e-12)

    tanh_lo = jnp.tanh(lo)
    ksum = jnp.zeros_like(k)
    outs = ((lwf_o, kf_o, bf_o), (lwb_o, kb_o, bb_o))
    for d in range(2):
        z = row(V_W0F + d) + _dot(tanh_lo, w2_ref[d])
        nz = -z
        softplus = jnp.maximum(nz, 0.0) + jnp.log(1.0 + jnp.exp(-jnp.abs(nz)))
        w_log = -softplus - 0.5
        lw = -jnp.exp(w_log)
        a = _sigmoid(row(V_A0F + d) + _dot(lo, a2_ref[d]))
        k_d = k * (1.0 + (a - 1.0) * row(V_KA))
        ksum = ksum + k_d
        lw_o, kd_o, bd_o = outs[d]
        lw_o[0] = lw
        kd_o[0] = k_d
        bd_o[0] = kk * a
    r_o[0] = r
    v_o[0] = v
    kk_o[0] = kk
    bonus_o[0] = _head_sums(r * ksum * row(V_RK), hsum, hexp) * v
    g_o[0] = _dot(_sigmoid(lo), g2_ref[...])


def _rwkv_prep(proj3, vec, lvec, w2f, a2f, g2f, hsum, hexp):
    b, t, _ = proj3.shape
    tt = PREP_TT
    nt = t // tt
    h8 = tt // 8
    cur = lambda cb, w: pl.BlockSpec((1, tt, w), lambda bi, i: (bi, i, cb))
    prv = lambda cb, w: pl.BlockSpec((1, 8, w), lambda bi, i: (bi, jnp.maximum(i * h8 - 1, 0), cb))
    nxt = lambda cb, w: pl.BlockSpec((1, 8, w), lambda bi, i: (bi, jnp.minimum((i + 1) * h8, t // 8 - 1), cb))
    cols = [(COL_R // A_WIDTH, A_WIDTH), (COL_K // A_WIDTH, A_WIDTH), (COL_V // A_WIDTH, A_WIDTH),
            (COL_L // LORA_PAD, LORA_PAD)]
    full = lambda shape: pl.BlockSpec(shape, lambda bi, i: (0,) * len(shape))
    in_specs = ([cur(*c) for c in cols] + [prv(*c) for c in cols] + [nxt(*c) for c in cols]
                + [full(vec.shape), full(lvec.shape), full(w2f.shape), full(a2f.shape), full(g2f.shape),
                   full(hsum.shape), full(hexp.shape)])
    out_spec = pl.BlockSpec((1, tt, A_WIDTH), lambda bi, i: (bi, i, 0))
    out_sds = jax.ShapeDtypeStruct((b, t, A_WIDTH), F32)
    return pl.pallas_call(
        _prep_kernel,
        grid=(b, nt),
        in_specs=in_specs,
        out_specs=[out_spec] * 11,
        out_shape=[out_sds] * 11,
        compiler_params=_cparams(("parallel", "parallel")),
        name="rwkv_prep",
    )(*([proj3] * 12), vec, lvec, w2f, a2f, g2f, hsum, hexp)


def _stack2(x, m0):
    return jnp.concatenate([jnp.where(m0, x, 0.0), jnp.where(m0, 0.0, x)], axis=0)


class _Chain(NamedTuple):
    r: jax.Array
    v: jax.Array
    kk: jax.Array
    lw: jax.Array
    kd: jax.Array
    bd: jax.Array
    s_ref: object
    rev: bool


def _scan_masks(c, reverse):
    ti = lax.broadcasted_iota(I32, (c, c), 0)
    si = lax.broadcasted_iota(I32, (c, c), 1)
    tri = jnp.where((si >= ti) if reverse else (si <= ti), 1.0, 0.0).astype(BF16)
    t2 = lax.broadcasted_iota(I32, (c, 2 * c), 0)
    s2 = lax.broadcasted_iota(I32, (c, 2 * c), 1)
    s2 = jnp.where(s2 >= c, s2 - c, s2)
    strict = (s2 > t2) if reverse else (s2 < t2)
    incl = (s2 >= t2) if reverse else (s2 <= t2)
    eye2 = jnp.where(s2 == t2, 1.0, 0.0)
    return tri, strict, incl, eye2


def _scan_chunk(chains):
    c = SCAN_CHUNK
    masks = {rev: _scan_masks(c, rev) for rev in (False, True)}
    m0 = lax.broadcasted_iota(I32, (c, LANES), 1) < A_HEAD_DIM
    half = lax.broadcasted_iota(I32, (c, 2 * c), 1) < c
    vi = lax.broadcasted_iota(I32, (LANES, LANES), 0)
    ki = lax.broadcasted_iota(I32, (LANES, LANES), 1)
    same_head = (vi < A_HEAD_DIM) == (ki < A_HEAD_DIM)
    bf = lambda x: x.astype(BF16)
    mm = lambda a, b: jnp.dot(a, b, preferred_element_type=F32)
    mm_nt = lambda a, b: lax.dot_general(a, b, (((1,), (1,)), ((), ())), preferred_element_type=F32)
    cat = jnp.concatenate

    cum = []
    for ch in chains:
        tri = masks[ch.rev][0]
        hi, mid, lo = _split3(ch.lw)
        cum.append(mm(tri, hi) + mm(tri, mid) + mm(tri, lo))

    ar, bk2, bkw, v2, etot = [], [], [], [], []
    for ch, cm in zip(chains, cum):
        tot = cm[0:1, :] if ch.rev else cm[c - 1:c, :]
        einv = jnp.exp(-cm)
        etail = jnp.exp(tot - cm)
        at = -ch.kk * jnp.exp(cm - ch.lw)
        rt = ch.r * jnp.exp(cm)
        ar.append(bf(cat([at, rt], axis=0)))
        bk2.append(bf(cat([_stack2(ch.bd * einv, m0), _stack2(ch.kd * einv, m0)], axis=0)))
        bkw.append(bf(cat([ch.bd * etail, ch.kd * etail], axis=0)))
        v2.append(bf(_stack2(ch.v, m0)))
        etot.append(jnp.exp(tot))

    mt = [mm_nt(a, b) for a, b in zip(ar, bk2)]
    s0 = [ch.s_ref[...] for ch in chains]
    ars = [mm_nt(a, bf(s)) for a, s in zip(ar, s0)]
    a_ab, a_ak, a_rbk = [], [], []
    for ch, m in zip(chains, mt):
        _, strict, incl, _ = masks[ch.rev]
        a_ab.append(jnp.where(strict, m[:c, :2 * c], 0.0))
        a_ak.append(bf(jnp.where(strict, m[:c, 2 * c:], 0.0)))
        a_rbk.append(bf(cat([jnp.where(incl, m[c:, :2 * c], 0.0), jnp.where(incl, m[c:, 2 * c:], 0.0)], axis=1)))
    akv = [mm(a, v) for a, v in zip(a_ak, v2)]

    tm = [masks[ch.rev][3] + a for ch, a in zip(chains, a_ab)]
    pw = [mm(bf(a), _stack2(bf(a), half)) for a in a_ab]
    steps = int(np.log2(c)) - 1
    for k in range(steps):
        blk = [_stack2(bf(x), half) for x in pw]
        tm = [t + mm(bf(t), d) for t, d in zip(tm, blk)]
        if k < steps - 1:
            pw = [mm(bf(x), d) for x, d in zip(pw, blk)]

    u = [mm(bf(t), bf(_stack2(a[:c] + k, m0))) for t, a, k in zip(tm, ars, akv)]
    y = [a[c:] + mm(ab, cat([bf(_stack2(uu, m0)), vv], axis=0)) for a, ab, uu, vv in zip(ars, a_rbk, u, v2)]
    upd = [mm(bf(cat([uu, ch.v], axis=0).T), w) for uu, ch, w in zip(u, chains, bkw)]
    for ch, s, e, up in zip(chains, s0, etot, upd):
        ch.s_ref[...] = s * e + jnp.where(same_head, up, 0.0)
    return y


SCAN_HP = 8


def _scan_kernel(rf, vf, kkf, lwf, kf, bf, rb, vb, kkb, lwb, kb, bb, yf_o, yb_o, sf_ref, sb_ref):
    @pl.when(pl.program_id(2) == 0)
    def _():
        sf_ref[...] = jnp.zeros_like(sf_ref)
        sb_ref[...] = jnp.zeros_like(sb_ref)

    chains = []
    for hp in range(SCAN_HP):
        ln = slice(hp * LANES, (hp + 1) * LANES)
        chains.append(_Chain(rf[0, :, ln], vf[0, :, ln], kkf[0, :, ln], lwf[0, :, ln], kf[0, :, ln], bf[0, :, ln],
                             sf_ref.at[hp], False))
        chains.append(_Chain(rb[0, :, ln], vb[0, :, ln], kkb[0, :, ln], lwb[0, :, ln], kb[0, :, ln], bb[0, :, ln],
                             sb_ref.at[hp], True))
    y = _scan_chunk(chains)
    for hp in range(SCAN_HP):
        ln = slice(hp * LANES, (hp + 1) * LANES)
        yf_o[0, :, ln] = y[2 * hp]
        yb_o[0, :, ln] = y[2 * hp + 1]


def _rwkv_scan(r, v, kk, lwf, kf, bf, lwb, kb, bb):
    b, t, _ = r.shape
    c = SCAN_CHUNK
    nc = t // c
    w = SCAN_HP * LANES
    fwd = pl.BlockSpec((1, c, w), lambda bi, hp, ci: (bi, ci, hp))
    bwd = pl.BlockSpec((1, c, w), lambda bi, hp, ci: (bi, nc - 1 - ci, hp))
    out_sds = jax.ShapeDtypeStruct((b, t, A_WIDTH), F32)
    return pl.pallas_call(
        _scan_kernel,
        grid=(b, A_WIDTH // w, nc),
        in_specs=[fwd] * 6 + [bwd] * 6,
        out_specs=[fwd, bwd],
        out_shape=[out_sds, out_sds],
        scratch_shapes=[pltpu.VMEM((SCAN_HP, LANES, LANES), F32), pltpu.VMEM((SCAN_HP, LANES, LANES), F32)],
        compiler_params=_cparams(("parallel", "parallel", "arbitrary")),
        name="rwkv_scan",
    )(r, v, kk, lwf, kf, bf, r, v, kk, lwb, kb, bb)


POST_TT = 512


def _post_kernel(yf, yb, bonus, g, vec_ref, hs_ref, he_ref, o_ref):
    y = yf[...] + yb[...]
    hsum, hexp = hs_ref[...], he_ref[...]
    inv = 1.0 / A_HEAD_DIM
    mu = _head_sums(y, hsum, hexp) * inv
    d = y - mu
    var = _head_sums(d * d, hsum, hexp) * inv
    yn = d * lax.rsqrt(var + GN_EPS) * vec_ref[0:1, :] + vec_ref[1:2, :]
    o_ref[...] = ((yn + bonus[...]) * g[...]).astype(o_ref.dtype)


def _rwkv_post(yf, yb, bonus, g, gnvec, hsum, hexp):
    n = yf.shape[0]
    tt = min(POST_TT, n)
    blk = pl.BlockSpec((tt, A_WIDTH), lambda i: (i, 0))
    full = lambda a: pl.BlockSpec(a.shape, lambda i: (0,) * a.ndim, pipeline_mode=pl.Buffered(1))
    return pl.pallas_call(
        _post_kernel,
        grid=(n // tt,),
        in_specs=[blk, blk, blk, blk, full(gnvec), full(hsum), full(hexp)],
        out_specs=blk,
        out_shape=jax.ShapeDtypeStruct((n, A_WIDTH), BF16),
        compiler_params=_cparams(("parallel",)),
        name="rwkv_post",
    )(yf, yb, bonus, g, gnvec, hsum, hexp)


def _gelu_tanh(x):
    return 0.5 * x * (1.0 + jnp.tanh(np.sqrt(2.0 / np.pi).astype(np.float32) * (x + 0.044715 * (x * x * x))))


def _sg_kernel(pb_ref, lnv_ref, w_ref, bias_ref, o_ref):
    z = _gelu_tanh(pb_ref[0])
    u = z[:, :B_WIDTH]
    v = _layer_norm_rows(z[:, B_WIDTH:], lnv_ref[0:1, :], lnv_ref[1:2, :])
    lane = lax.broadcasted_iota(I32, (B_CHUNK, LANES), 1)
    m0 = lane < B_GROUP_DIM
    parts = []
    for q in range(B_WIDTH // LANES):
        vq = v[:, q * LANES:(q + 1) * LANES]
        parts.append(_dot(w_ref[q], _stack2(vq, m0)))
    mixed = jnp.concatenate(parts, axis=1) + bias_ref[...]
    o_ref[0] = (u * mixed).astype(o_ref.dtype)


def _spatial_gating(proj3, lnvec, w2, bias_full):
    b, t, _ = proj3.shape
    full = lambda a: pl.BlockSpec(a.shape, lambda bi, i: (0,) * a.ndim)
    return pl.pallas_call(
        _sg_kernel,
        grid=(b, t // B_CHUNK),
        in_specs=[pl.BlockSpec((1, B_CHUNK, B_COLS), lambda bi, i: (bi, i, COL_B // B_COLS)),
                  full(lnvec), full(w2), full(bias_full)],
        out_specs=pl.BlockSpec((1, B_CHUNK, B_WIDTH), lambda bi, i: (bi, i, 0)),
        out_shape=jax.ShapeDtypeStruct((b, t, B_WIDTH), BF16),
        compiler_params=_cparams(("parallel", "parallel")),
        name="spatial_gating",
    )(proj3, lnvec, w2, bias_full)


NA_WIN = C_WIN_ROWS * GRID_W


def _na_row_start(i, rows):
    return jnp.clip(i - C_WIN_ROWS // 2, 0, rows - C_WIN_ROWS)


NA_QR = 2


def _na_kernel(q_ref, k_ref, v_ref, *rest, rows):
    bias_refs, o_ref = rest[:NA_QR], rest[NA_QR]
    kh_ref, vh_ref = rest[NA_QR + 1:]
    i = pl.program_id(1)
    scale = C_HEAD_DIM ** -0.5

    @pl.when(i == 0)
    def _():
        for h in range(C_HEADS):
            sl = slice(h * C_HEAD_DIM, (h + 1) * C_HEAD_DIM)
            kh_ref[h] = k_ref[0, :, sl].astype(BF16)
            vh_ref[h] = v_ref[0, :, sl].astype(BF16)

    qs, ks, vs, bs = [], [], [], []
    for j in range(NA_QR):
        start = pl.multiple_of(_na_row_start(i * NA_QR + j, rows) * GRID_W, GRID_W)
        q = q_ref[0, j * GRID_W:(j + 1) * GRID_W, :]
        for h in range(C_HEADS):
            sl = slice(h * C_HEAD_DIM, (h + 1) * C_HEAD_DIM)
            qs.append(q[:, sl].astype(BF16))
            ks.append(kh_ref[h, pl.ds(start, NA_WIN), :])
            vs.append(vh_ref[h, pl.ds(start, NA_WIN), :])
            bs.append(bias_refs[j][0, h])
    nt = lambda a, b: lax.dot_general(a, b, (((1,), (1,)), ((), ())), preferred_element_type=F32)
    ss = [nt(q, k) * scale + bias for q, k, bias in zip(qs, ks, bs)]
    ps = []
    for s in ss:
        e = jnp.exp(s - jnp.max(s, axis=-1, keepdims=True))
        ps.append((e / jnp.sum(e, axis=-1, keepdims=True)).astype(BF16))
    os_ = [jnp.dot(p, v, preferred_element_type=F32) for p, v in zip(ps, vs)]
    for j in range(NA_QR):
        o_ref[0, j * GRID_W:(j + 1) * GRID_W, :] = jnp.concatenate(
            os_[j * C_HEADS:(j + 1) * C_HEADS], axis=1).astype(o_ref.dtype)


def _neighbourhood_attention(proj3, bias_tab):
    b, t, _ = proj3.shape
    rows = t // GRID_W
    cq = COL_C // C_WIDTH
    qr = NA_QR
    seq = lambda cb: pl.BlockSpec((1, t, C_WIDTH), lambda bi, i: (bi, 0, cb))
    bias = lambda j: pl.BlockSpec((1, C_HEADS, GRID_W, NA_WIN),
                                  lambda bi, i: (i * qr + j - _na_row_start(i * qr + j, rows), 0, 0, 0))
    return pl.pallas_call(
        functools.partial(_na_kernel, rows=rows),
        grid=(b, rows // qr),
        in_specs=[pl.BlockSpec((1, qr * GRID_W, C_WIDTH), lambda bi, i: (bi, i, cq)), seq(cq + 1), seq(cq + 2)]
                 + [bias(j) for j in range(qr)],
        out_specs=pl.BlockSpec((1, qr * GRID_W, C_WIDTH), lambda bi, i: (bi, i, 0)),
        out_shape=jax.ShapeDtypeStruct((b, t, C_WIDTH), BF16),
        scratch_shapes=[pltpu.VMEM((C_HEADS, t, C_HEAD_DIM), BF16), pltpu.VMEM((C_HEADS, t, C_HEAD_DIM), BF16)],
        compiler_params=_cparams(("parallel", "arbitrary")),
        name="nbr_attention",
    )(proj3, proj3, proj3, *([bias_tab] * qr))


def _na_bias_table(rpb, rows):
    kc = C_WIN_COLS
    cols = np.arange(GRID_W)
    col_start = np.clip(cols - kc // 2, 0, GRID_W - kc)
    key_col = np.arange(GRID_W)
    in_win = (key_col[None, :] >= col_start[:, None]) & (key_col[None, :] < col_start[:, None] + kc)
    delta = np.arange(C_WIN_ROWS)
    row_off = np.arange(C_WIN_ROWS)[None, :] - delta[:, None] + (C_WIN_ROWS - 1)
    by_row = rpb[:, row_off]
    span = 2 * GRID_W
    lead = GRID_W - C_WIN_COLS
    padded = jnp.pad(by_row, [(0, 0)] * 3 + [(lead, span - lead - (2 * C_WIN_COLS - 1))])
    skew = jnp.tile(padded, GRID_W)[..., :GRID_W * (span - 1)].reshape(by_row.shape[:3] + (GRID_W, span - 1))
    bias = skew[..., GRID_W - 1:]
    bias = jnp.where(in_win, bias, -1e30)
    bias = jnp.transpose(bias, (1, 0, 3, 2, 4)).reshape(C_WIN_ROWS, C_HEADS, GRID_W, NA_WIN)
    return bias.astype(F32)


MERGE_TM = 512


def _merge_kernel(ya, yb, yc, ga, gb, gc, pa, pb, pc, o_ref):
    m = _sigmoid(ga[...]) * jnp.dot(ya[...], pa[...], preferred_element_type=F32)
    m = m + _sigmoid(gb[...]) * jnp.dot(yb[...], pb[...], preferred_element_type=F32)
    m = m + _sigmoid(gc[...]) * jnp.dot(yc[...], pc[...], preferred_element_type=F32)
    o_ref[...] = m.astype(o_ref.dtype)


def _merge(ya, yb, yc, proj, p_a, p_b, p_c):
    n = ya.shape[0]
    tm = min(MERGE_TM, n)
    rowblk = lambda w: pl.BlockSpec((tm, w), lambda i: (i, 0))
    gate = lambda j: pl.BlockSpec((tm, D_MODEL), lambda i: (i, COL_G // D_MODEL + j))
    full = lambda a: pl.BlockSpec(a.shape, lambda i: (0,) * a.ndim, pipeline_mode=pl.Buffered(1))
    return pl.pallas_call(
        _merge_kernel,
        grid=(n // tm,),
        in_specs=[rowblk(A_WIDTH), rowblk(B_WIDTH), rowblk(C_WIDTH), gate(0), gate(1), gate(2),
                  full(p_a), full(p_b), full(p_c)],
        out_specs=rowblk(D_MODEL),
        out_shape=jax.ShapeDtypeStruct((n, D_MODEL), BF16),
        compiler_params=_cparams(("parallel",)),
        name="branch_merge",
    )(ya, yb, yc, proj, proj, proj, p_a, p_b, p_c)


OUT_TM = 512


def _outln_kernel(m_ref, w_ref, x_ref, lnv_ref, o_ref, ob_ref):
    h = jnp.dot(m_ref[...], w_ref[...], preferred_element_type=F32)
    y = _layer_norm_rows(DEEPNORM_ALPHA * x_ref[...] + h, lnv_ref[0:1, :], lnv_ref[1:2, :])
    o_ref[...] = y
    ob_ref[...] = y.astype(BF16)


def _out_proj_ln(m, w_out, x, lnvec):
    n = m.shape[0]
    tm = min(OUT_TM, n)
    blk = pl.BlockSpec((tm, D_MODEL), lambda i: (i, 0))
    full = lambda a: pl.BlockSpec(a.shape, lambda i: (0,) * a.ndim, pipeline_mode=pl.Buffered(1))
    return pl.pallas_call(
        _outln_kernel,
        grid=(n // tm,),
        in_specs=[blk, full(w_out), blk, full(lnvec)],
        out_specs=[blk, blk],
        out_shape=[jax.ShapeDtypeStruct((n, D_MODEL), F32), jax.ShapeDtypeStruct((n, D_MODEL), BF16)],
        compiler_params=_cparams(("parallel",)),
        name="out_proj_ln",
    )(m, w_out, x, lnvec)


ROUTER_TM = 512


def _router_kernel(x_ref, wt_ref, o_ref):
    xh, xl = _split2(x_ref[...])
    wh, wl = _split2(wt_ref[...])
    nt = lambda a, b: lax.dot_general(a, b, (((1,), (1,)), ((), ())), preferred_element_type=F32)
    logits = nt(wh, xh) + nt(wl, xh) + nt(wh, xl)
    logits = logits - jnp.max(logits, axis=0, keepdims=True)
    e = jnp.exp(logits)
    o_ref[...] = e / jnp.sum(e, axis=0, keepdims=True)


def _router(x, w_router_t):
    n = x.shape[0]
    tm = min(ROUTER_TM, n)
    return pl.pallas_call(
        _router_kernel,
        grid=(n // tm,),
        in_specs=[pl.BlockSpec((tm, D_MODEL), lambda i: (i, 0)),
                  pl.BlockSpec(w_router_t.shape, lambda i: (0, 0))],
        out_specs=pl.BlockSpec((N_EXPERTS, tm), lambda i: (0, i)),
        out_shape=jax.ShapeDtypeStruct((N_EXPERTS, n), F32),
        compiler_params=_cparams(("parallel",)),
        name="router",
    )(x, w_router_t)


def _select_kernel(aff_ref, pos_ref, off_ref, *, cap):
    aff = aff_ref[...]
    e_, g_, l_ = aff.shape
    n = g_ * l_
    bits = pltpu.bitcast(aff, I32)

    def count(mask):
        c = jnp.sum(jnp.where(mask, 1.0, 0.0), axis=2, keepdims=True)
        return jnp.sum(c, axis=1, keepdims=True)

    def thr_body(_, carry):
        lo, hi = carry
        mid = lo + (hi - lo + 1) // 2
        ok = count(bits >= mid) >= cap
        return jnp.where(ok, mid, lo), jnp.where(ok, hi, mid - 1)

    lo0 = jnp.zeros((e_, 1, 1), I32)
    hi0 = jnp.full((e_, 1, 1), 0x7F800000, I32)
    thr, _ = lax.fori_loop(0, 32, thr_body, (lo0, hi0))
    gt = bits > thr
    tie = bits == thr
    need = cap - count(gt)
    idx = lax.broadcasted_iota(I32, aff.shape, 1) * l_ + lax.broadcasted_iota(I32, aff.shape, 2)

    def idx_body(_, carry):
        lo, hi = carry
        mid = (lo + hi) // 2
        ok = count(tie & (idx <= mid)) >= need
        return jnp.where(ok, lo, mid + 1), jnp.where(ok, mid, hi)

    lo1 = jnp.zeros((e_, 1, 1), I32)
    hi1 = jnp.full((e_, 1, 1), n - 1, I32)
    cut, _ = lax.fori_loop(0, int(np.ceil(np.log2(n))) + 1, idx_body, (lo1, hi1))
    sel = jnp.where(gt | (tie & (idx <= cut)), 1.0, 0.0)

    sel2 = sel.reshape(e_ * g_, l_)
    ls = lax.broadcasted_iota(I32, (l_, l_), 0)
    lt = lax.broadcasted_iota(I32, (l_, l_), 1)
    incl = jnp.dot(sel2.astype(BF16), jnp.where(ls <= lt, 1.0, 0.0).astype(BF16), preferred_element_type=F32)
    tot = jnp.dot(sel2.astype(BF16), jnp.ones((l_, l_), BF16), preferred_element_type=F32)
    gs = lax.broadcasted_iota(I32, (g_, g_), 0)
    gt_ = lax.broadcasted_iota(I32, (g_, g_), 1)
    lower = jnp.where(gt_ < gs, 1.0, 0.0).astype(BF16)
    offs = [jnp.dot(lower, tot[e * g_:(e + 1) * g_].astype(BF16), preferred_element_type=F32)
            for e in range(e_)]
    off = jnp.concatenate(offs, axis=0)
    pos = incl + off - sel2
    pos_ref[...] = jnp.where(sel2 > 0.5, pos, -1.0).astype(I32).reshape(e_, g_, l_)
    off_ref[...] = off.astype(I32).reshape(e_, g_, l_)


def _select(aff3, cap):
    full = pl.BlockSpec(aff3.shape, lambda i: (0, 0, 0))
    sds = jax.ShapeDtypeStruct(aff3.shape, I32)
    return pl.pallas_call(
        functools.partial(_select_kernel, cap=cap),
        grid=(1,),
        in_specs=[full],
        out_specs=[full, full],
        out_shape=[sds, sds],
        compiler_params=_cparams(("arbitrary",)),
        name="expert_select",
    )(aff3)


def _gather_kernel(tile_s, valid_s, e_g, slab_g, first_g, *refs):
    k_ = GATHER_GROUP
    x_refs, pos_refs, aff_refs = refs[:k_], refs[k_:2 * k_], refs[2 * k_:3 * k_]
    o_ref, g_ref = refs[3 * k_:]
    g = pl.program_id(0)
    t = MOE_TILE

    @pl.when(first_g[g] == 1)
    def _():
        o_ref[...] = jnp.zeros_like(o_ref)
        g_ref[...] = jnp.zeros_like(g_ref)

    nvalid = valid_s[g * k_]
    for k in range(1, k_):
        nvalid = nvalid + valid_s[g * k_ + k]

    @pl.when(nvalid > 0)
    def _():
        rank = lax.broadcasted_iota(I32, (t, t), 0) + slab_g[g] * t
        got = o_ref[0].astype(F32)
        gate = g_ref[0]
        for k in range(k_):
            want = jnp.where(valid_s[g * k_ + k] == 1, rank, -2)
            hit = pos_refs[k][0] == want
            got = got + jnp.dot(jnp.where(hit, 1.0, 0.0).astype(BF16), x_refs[k][...],
                                preferred_element_type=F32)
            gate = gate + jnp.sum(jnp.where(hit, aff_refs[k][0], 0.0), axis=1, keepdims=True)
        o_ref[0] = got.astype(o_ref.dtype)
        g_ref[0] = gate


def _moe_gather(x_bf, pos_en, aff_en, items, cap):
    n = x_bf.shape[0]
    t = MOE_TILE
    k_ = GATHER_GROUP
    ng = items[2].shape[0]
    xs = [pl.BlockSpec((t, D_MODEL), lambda g, tl, va, e, s, fi, k=k: (tl[g * k_ + k], 0)) for k in range(k_)]
    rows = [pl.BlockSpec((1, 1, t), lambda g, tl, va, e, s, fi, k=k: (e[g], 0, tl[g * k_ + k])) for k in range(k_)]
    grid_spec = pltpu.PrefetchScalarGridSpec(
        num_scalar_prefetch=5,
        grid=(ng,),
        in_specs=xs + rows + rows,
        out_specs=[pl.BlockSpec((1, t, D_MODEL), lambda g, tl, va, e, s, fi: (e[g], s[g], 0)),
                   pl.BlockSpec((1, t, 1), lambda g, tl, va, e, s, fi: (e[g], s[g], 0))],
    )
    pos3 = pos_en.reshape(N_EXPERTS, 1, n)
    aff3 = aff_en.reshape(N_EXPERTS, 1, n)
    return pl.pallas_call(
        _gather_kernel,
        grid_spec=grid_spec,
        out_shape=[jax.ShapeDtypeStruct((N_EXPERTS, cap, D_MODEL), BF16),
                   jax.ShapeDtypeStruct((N_EXPERTS, cap, 1), F32)],
        compiler_params=_cparams(("arbitrary",)),
        name="moe_gather",
    )(*items, *([x_bf] * k_), *([pos3] * k_), *([aff3] * k_))


FFN_TM = 512


def _ffn_kernel(x_ref, gate_ref, wg_ref, wu_ref, wd_ref, o_ref):
    x = x_ref[0]
    g = jnp.dot(x, wg_ref[0], preferred_element_type=F32)
    u = jnp.dot(x, wu_ref[0], preferred_element_type=F32)
    h = (g * _sigmoid(g)) * u
    y = jnp.dot(h.astype(BF16), wd_ref[0], preferred_element_type=F32) * gate_ref[0]
    o_ref[0] = y.astype(o_ref.dtype)


def _expert_ffn(xe, gate, wg, wu, wd):
    e_, cap, _ = xe.shape
    tm = min(FFN_TM, cap)
    wspec = lambda a: pl.BlockSpec((1,) + a.shape[1:], lambda e, i: (e, 0, 0))
    rows = lambda w: pl.BlockSpec((1, tm, w), lambda e, i: (e, i, 0))
    sds = jax.ShapeDtypeStruct((e_, cap, D_MODEL), BF16)
    return pl.pallas_call(
        _ffn_kernel,
        grid=(e_, cap // tm),
        in_specs=[rows(D_MODEL), rows(1), wspec(wg), wspec(wu), wspec(wd)],
        out_specs=rows(D_MODEL),
        out_shape=sds,
        compiler_params=_cparams(("parallel", "arbitrary")),
        name="expert_ffn",
    )(xe, gate, wg, wu, wd)


def _combine_kernel(e_s, slab_s, valid_s, tile_g, first_g, last_g, *refs):
    k_ = COMBINE_GROUP
    ye_refs = refs[:k_]
    pos_ref, x_ref, lnv_ref, o_ref, ob_ref, acc_ref = refs[k_:]
    g = pl.program_id(0)
    t = MOE_TILE

    @pl.when(first_g[g] == 1)
    def _():
        acc_ref[...] = jnp.zeros_like(acc_ref)

    nvalid = valid_s[g * k_]
    for k in range(1, k_):
        nvalid = nvalid + valid_s[g * k_ + k]

    @pl.when(nvalid > 0)
    def _():
        posf = pos_ref[...].astype(F32)
        lane = lax.broadcasted_iota(I32, (t, N_EXPERTS), 1)
        row = lax.broadcasted_iota(I32, (t, MOE_SUB), 1)
        onehots = []
        for k in range(k_):
            s = g * k_ + k
            pos = jnp.sum(jnp.where(lane == e_s[s], posf, 0.0), axis=1, keepdims=True)
            base = jnp.where(valid_s[s] == 1, slab_s[s] * MOE_SUB, -2 * MOE_SUB)
            onehots.append(jnp.where(pos == (row + base).astype(F32), 1.0, 0.0).astype(BF16))
        onehot = jnp.concatenate(onehots, axis=1)
        ye = jnp.concatenate([r[0] for r in ye_refs], axis=0)
        acc_ref[...] += jnp.dot(onehot, ye, preferred_element_type=F32)

    @pl.when(last_g[g] == 1)
    def _():
        y = _layer_norm_rows(DEEPNORM_ALPHA * x_ref[...] + acc_ref[...], lnv_ref[0:1, :], lnv_ref[1:2, :])
        o_ref[...] = y
        ob_ref[...] = y.astype(BF16)


def _moe_combine(ye, pos_ne, x, lnvec, items):
    n = x.shape[0]
    t = MOE_TILE
    k_ = COMBINE_GROUP
    ng = items[3].shape[0]
    tile_blk = lambda w: pl.BlockSpec((t, w), lambda g, e, s, va, tl, fi, la: (tl[g], 0))
    slabs = [pl.BlockSpec((1, MOE_SUB, D_MODEL), lambda g, e, s, va, tl, fi, la, k=k: (e[g * k_ + k], s[g * k_ + k], 0))
             for k in range(k_)]
    grid_spec = pltpu.PrefetchScalarGridSpec(
        num_scalar_prefetch=6,
        grid=(ng,),
        in_specs=slabs + [tile_blk(N_EXPERTS), tile_blk(D_MODEL),
                          pl.BlockSpec(lnvec.shape, lambda g, e, s, va, tl, fi, la: (0, 0))],
        out_specs=[tile_blk(D_MODEL), tile_blk(D_MODEL)],
        scratch_shapes=[pltpu.VMEM((t, D_MODEL), F32)],
    )
    return pl.pallas_call(
        _combine_kernel,
        grid_spec=grid_spec,
        out_shape=[jax.ShapeDtypeStruct((n, D_MODEL), F32), jax.ShapeDtypeStruct((n, D_MODEL), BF16)],
        compiler_params=_cparams(("arbitrary",)),
        name="moe_combine",
    )(*items, *([ye] * k_), pos_ne, x, lnvec)


def _moe_items(group_off, n, cap):
    t = MOE_TILE
    nt = n // t
    e_ = N_EXPERTS
    starts = group_off[:, ::t // LANES, 0]
    ends = jnp.concatenate([starts[:, 1:], jnp.full((e_, 1), cap, I32)], axis=1)
    big = jnp.int32(2 ** 30)

    def build(rows, by_tile, k_):
        ns = cap // rows
        e_p = jnp.broadcast_to(jnp.arange(e_, dtype=I32)[:, None], (e_, nt))
        tile_p = jnp.broadcast_to(jnp.arange(nt, dtype=I32)[None, :], (e_, nt))
        slab_p = jnp.minimum(starts // rows, ns - 1)
        valid_p = (ends > starts).astype(I32)
        bound = jnp.arange(ns, dtype=I32) * rows
        tile_s = jnp.sum((starts[:, :, None] <= bound[None, None, :]).astype(I32), axis=1) - 1
        tile_s = jnp.clip(tile_s, 0, nt - 1)
        st_s = jnp.take_along_axis(starts, tile_s, axis=1)
        en_s = jnp.take_along_axis(ends, tile_s, axis=1)
        valid_s = ((st_s < bound[None, :]) & (bound[None, :] < en_s)).astype(I32)
        e_s = jnp.broadcast_to(jnp.arange(e_, dtype=I32)[:, None], (e_, ns))
        slab_s = jnp.broadcast_to(jnp.arange(ns, dtype=I32)[None, :], (e_, ns))
        cat = lambda a, b: jnp.concatenate([a.reshape(-1), b.reshape(-1)])
        e_a, slab_a, tile_a = cat(e_p, e_s), cat(slab_p, slab_s), cat(tile_p, tile_s)
        valid_a = cat(valid_p, valid_s)
        secondary = cat(jnp.zeros_like(e_p), jnp.ones_like(e_s))
        ni = e_a.shape[0]
        ar = jnp.arange(ni, dtype=I32)
        if by_tile:
            keep = jnp.maximum(valid_a, 1 - secondary)
            key = ((tile_a * e_ + e_a) * 2 + secondary) * ns + slab_a
        else:
            keep = valid_a
            key = (e_a * ns + slab_a) * nt + tile_a
        key = jnp.where(keep == 1, key, big + ar)
        rank = jnp.sum((key[None, :] < key[:, None]).astype(I32), axis=1)
        order = jnp.sum(jnp.where(rank[None, :] == ar[:, None], ar[None, :], 0), axis=1)
        nkeep = jnp.sum(keep)
        kept = ar < nkeep
        oe, os_, otl, ov = e_a[order], slab_a[order], tile_a[order], valid_a[order]
        blk = otl if by_tile else oe * ns + os_

        nblk = nt if by_tile else e_ * ns
        ng = -(-ni // k_) + nblk
        first = jnp.concatenate([jnp.ones((1,), bool), blk[1:] != blk[:-1]])
        run_start = lax.cummax(jnp.where(first, ar, 0))
        sub = (ar - run_start) % k_
        leads = (sub == 0) & kept
        gid = jnp.cumsum(leads.astype(I32)) - 1
        slot = jnp.where(kept, gid * k_ + sub, -1)
        sar = jnp.arange(ng * k_, dtype=I32)
        hit = slot[None, :] == sar[:, None]
        has = jnp.sum(hit.astype(I32), axis=1)
        idx = jnp.sum(jnp.where(hit, ar[None, :], 0), axis=1)
        n_real = jnp.sum(leads.astype(I32))
        lead = idx.reshape(ng, k_)[:, 0]
        lead = jnp.where(jnp.arange(ng, dtype=I32) < n_real, lead, lead[jnp.maximum(n_real - 1, 0)])
        idx = jnp.where(has == 1, idx, lead[sar // k_])
        gblk = blk[lead]
        change = (gblk[1:] != gblk[:-1]).astype(I32)
        one = jnp.ones((1,), I32)
        slots = dict(e=oe[idx], slab=os_[idx], tile=otl[idx], valid=ov[idx] * has)
        groups = dict(e=oe[lead], slab=os_[lead], tile=otl[lead], first=jnp.concatenate([one, change]),
                      last=jnp.concatenate([change, one]))
        return slots, groups

    gs, gg = build(MOE_TILE, False, GATHER_GROUP)
    cs, cg = build(MOE_SUB, True, COMBINE_GROUP)
    return ((gs["tile"], gs["valid"], gg["e"], gg["slab"], gg["first"]),
            (cs["e"], cs["slab"], cs["valid"], cg["tile"], cg["first"], cg["last"]))


def _expert_choice_moe_ln(x, x_bf, w_router_t, wg, wu, wd, lnvec):
    n = x.shape[0]
    cap = EC_CAPACITY_FACTOR * n // N_EXPERTS
    aff_en = _router(x, w_router_t)
    pos3, off3 = _select(aff_en.reshape(N_EXPERTS, n // LANES, LANES), cap)
    pos_en = pos3.reshape(N_EXPERTS, n)
    g_items, c_items = _moe_items(off3, n, cap)
    xe, gate = _moe_gather(x_bf, pos_en, aff_en, g_items, cap)
    ye = _expert_ffn(xe, gate, wg, wu, wd)
    return _moe_combine(ye, pos_en.T, x, lnvec, c_items)


def _pack_cols(w):
    a_end = A_COLS
    b_end = a_end + B_COLS
    c_end = b_end + C_COLS
    pad = jnp.zeros(w.shape[:-1] + (LORA_PAD - LORA_COLS,), w.dtype)
    return jnp.concatenate([w[..., :3 * A_WIDTH], w[..., a_end:b_end], w[..., c_end:], w[..., b_end:c_end],
                            w[..., 3 * A_WIDTH:a_end], pad], axis=-1)


def _lora_rows(w, start):
    k = w.shape[-2]
    return jnp.pad(w, [(0, 0)] * (w.ndim - 2) + [(start, LORA_PAD - start - k), (0, 0)])


def _prepare(w_in, mu_prev, mu_next, decay_w0, decay_w2, iclr_a0, iclr_a2, gate_g2, k_k, k_a, r_k, gn_g, gn_b,
             sg_ln_g, sg_ln_b, sg_w, sg_b, rpb, p_a, p_b, p_c, w_out, ln_mix_g, ln_mix_b, w_router, e_gate, e_up,
             e_down, ln_ffn_g, ln_ffn_b, rows):
    l_ = w_in.shape[0]
    pad_a = lambda m: jnp.pad(m, ((0, 0), (0, LORA_PAD - LORA_COLS)))
    mup, mun = mu_prev, mu_next
    vec_rows = [mup[:, :A_WIDTH], mup[:, A_WIDTH:2 * A_WIDTH], mup[:, 2 * A_WIDTH:3 * A_WIDTH],
                mun[:, :A_WIDTH], mun[:, A_WIDTH:2 * A_WIDTH], mun[:, 2 * A_WIDTH:3 * A_WIDTH],
                k_k, k_a, r_k.reshape(l_, A_WIDTH), decay_w0[:, 0], decay_w0[:, 1], iclr_a0[:, 0], iclr_a0[:, 1]]
    vec = jnp.stack(vec_rows + [jnp.zeros_like(k_k)] * (16 - len(vec_rows)), axis=1)
    lvec = jnp.stack([pad_a(mup[:, 3 * A_WIDTH:]), pad_a(mun[:, 3 * A_WIDTH:])]
                     + [jnp.zeros((l_, LORA_PAD), F32)] * 6, axis=1)
    w2f = jnp.stack([_lora_rows(decay_w2[:, 0], 0), _lora_rows(decay_w2[:, 1], A_DECAY_LORA)], axis=1)
    a2f = jnp.stack([_lora_rows(iclr_a2[:, 0], 2 * A_DECAY_LORA),
                     _lora_rows(iclr_a2[:, 1], 2 * A_DECAY_LORA + A_ICLR_LORA)], axis=1)
    g2f = _lora_rows(gate_g2, 2 * A_DECAY_LORA + 2 * A_ICLR_LORA)
    head = np.arange(A_WIDTH) // A_HEAD_DIM
    hsum = jnp.asarray(head[:, None] == np.arange(LANES)[None, :], BF16)
    sgw2 = sg_w.reshape(l_, B_GROUPS // 2, 2, B_CHUNK, B_CHUNK).transpose(0, 1, 3, 2, 4)
    sgw2 = sgw2.reshape(l_, B_GROUPS // 2, B_CHUNK, 2 * B_CHUNK)
    sg_bias = jnp.repeat(jnp.swapaxes(sg_b, 1, 2), B_GROUP_DIM, axis=2)
    return dict(
        w_in=_pack_cols(w_in.astype(BF16)), vec=vec, lvec=lvec, w2f=w2f.astype(BF16), a2f=a2f.astype(BF16),
        g2f=g2f.astype(BF16), hsum=hsum, hexp=hsum.T, gnvec=jnp.stack([gn_g, gn_b], axis=1),
        sg_ln=jnp.stack([sg_ln_g, sg_ln_b], axis=1), sgw2=sgw2.astype(BF16), sg_bias=sg_bias,
        na_bias=jnp.stack([_na_bias_table(rpb[l], rows) for l in range(l_)]),
        p_a=p_a.astype(BF16), p_b=p_b.astype(BF16), p_c=p_c.astype(BF16), w_out=w_out.astype(BF16),
        ln_mix=jnp.stack([ln_mix_g, ln_mix_b], axis=1), w_router_t=jnp.swapaxes(w_router, 1, 2),
        e_gate=e_gate.astype(BF16), e_up=e_up.astype(BF16), e_down=e_down.astype(BF16),
        ln_ffn=jnp.stack([ln_ffn_g, ln_ffn_b], axis=1))


def _mixer(x, x_bf, p, l, b, t):
    n = b * t
    proj = _matmul(x_bf, p["w_in"][l], 1024, 1024)
    proj3 = proj.reshape(b, t, IN_PAD)
    r, v, kk, lwf, kf, bf, lwb, kb, bb, bonus, g = _rwkv_prep(
        proj3, p["vec"][l], p["lvec"][l], p["w2f"][l], p["a2f"][l], p["g2f"][l], p["hsum"], p["hexp"])
    yf, yb = _rwkv_scan(r, v, kk, lwf, kf, bf, lwb, kb, bb)
    flat = lambda a: a.reshape(n, a.shape[-1])
    ya = _rwkv_post(flat(yf), flat(yb), flat(bonus), flat(g), p["gnvec"][l], p["hsum"], p["hexp"])
    ybr = _spatial_gating(proj3, p["sg_ln"][l], p["sgw2"][l], p["sg_bias"][l])
    ycr = _neighbourhood_attention(proj3, p["na_bias"][l])
    m = _merge(ya, flat(ybr), flat(ycr), proj, p["p_a"][l], p["p_b"][l], p["p_c"][l])
    return _out_proj_ln(m, p["w_out"][l], x, p["ln_mix"][l])


def _trunk(x3, p):
    b, t, _ = x3.shape
    x = x3.reshape(b * t, D_MODEL)
    x_bf = x.astype(BF16)
    for l in range(DEPTH):
        x, x_bf = _mixer(x, x_bf, p, l, b, t)
        x, x_bf = _expert_choice_moe_ln(x, x_bf, p["w_router_t"][l], p["e_gate"][l], p["e_up"][l],
                                        p["e_down"][l], p["ln_ffn"][l])
    return x.reshape(b, t, D_MODEL)


def kernel(x_prompt, x_sample, w_in, mu_prev, mu_next, decay_w0, decay_w2, iclr_a0, iclr_a2, gate_g2, k_k, k_a, r_k, gn_g, gn_b, sg_ln_g, sg_ln_b, sg_w, sg_b, rpb, p_a, p_b, p_c, w_out, ln_mix_g, ln_mix_b, w_router, e_gate, e_up, e_down, ln_ffn_g, ln_ffn_b):
    assert x_prompt.shape[1] == x_sample.shape[1]
    rows = x_prompt.shape[1] // GRID_W
    p = _prepare(w_in, mu_prev, mu_next, decay_w0, decay_w2, iclr_a0, iclr_a2, gate_g2, k_k, k_a, r_k, gn_g, gn_b,
                 sg_ln_g, sg_ln_b, sg_w, sg_b, rpb, p_a, p_b, p_c, w_out, ln_mix_g, ln_mix_b, w_router, e_gate,
                 e_up, e_down, ln_ffn_g, ln_ffn_b, rows)
    return (_trunk(x_prompt, p), _trunk(x_sample, p))
```

```python
import functools
from typing import NamedTuple

import numpy as np
import jax
import jax.numpy as jnp
from jax import lax
from jax.experimental import pallas as pl
from jax.experimental.pallas import tpu as pltpu

F32 = jnp.float32
BF16 = jnp.bfloat16
I32 = jnp.int32

D_MODEL = 2048
DEPTH = 4
GRID_W = 64
A_HEADS = 16
A_HEAD_DIM = 64
A_WIDTH = A_HEADS * A_HEAD_DIM
A_DECAY_LORA = 64
A_ICLR_LORA = 64
A_GATE_LORA = 160
B_GROUPS = 8
B_GROUP_DIM = 64
B_WIDTH = B_GROUPS * B_GROUP_DIM
B_CHUNK = 128
C_HEADS = 8
C_HEAD_DIM = 64
C_WIDTH = C_HEADS * C_HEAD_DIM
C_WIN_ROWS = 8
C_WIN_COLS = 16
N_EXPERTS = 16
EXPERT_HIDDEN = 1024
EC_CAPACITY_FACTOR = 2
A_COLS = 3 * A_WIDTH + 2 * A_DECAY_LORA + 2 * A_ICLR_LORA + A_GATE_LORA
B_COLS = 2 * B_WIDTH
C_COLS = 3 * C_WIDTH
G_COLS = 3 * D_MODEL
DEEPNORM_ALPHA = (2 * DEPTH) ** 0.25
LN_EPS = 1e-5
GN_EPS = 64e-5

LORA_COLS = 2 * A_DECAY_LORA + 2 * A_ICLR_LORA + A_GATE_LORA
LORA_PAD = 512
COL_R, COL_K, COL_V = 0, A_WIDTH, 2 * A_WIDTH
COL_B = 3 * A_WIDTH
COL_G = COL_B + B_COLS
COL_C = COL_G + G_COLS
COL_L = COL_C + C_COLS
IN_PAD = COL_L + LORA_PAD

LANES = 128
MXU_DIM = 256
SCAN_CHUNK = 64
MOE_TILE = 256
MOE_SUB = 64
GATHER_TOK = 128
GATHER_GROUP = 8
COMBINE_GROUP = 8
VMEM_LIMIT = 48 * 1024 * 1024


def _cparams(sem):
    return pltpu.CompilerParams(dimension_semantics=sem, vmem_limit_bytes=VMEM_LIMIT)


def _dot(a, b):
    return jnp.dot(a.astype(BF16), b.astype(BF16), preferred_element_type=F32)


def _dot_nt(a, b):
    return lax.dot_general(a.astype(BF16), b.astype(BF16), (((1,), (1,)), ((), ())),
                           preferred_element_type=F32)


def _split2(x):
    hi = x.astype(BF16)
    lo = (x - hi.astype(F32)).astype(BF16)
    return hi, lo


def _split3(x):
    hi = x.astype(BF16)
    r1 = x - hi.astype(F32)
    mid = r1.astype(BF16)
    lo = (r1 - mid.astype(F32)).astype(BF16)
    return hi, mid, lo


def _dot_exact01(m01, x):
    hi, mid, lo = _split3(x)
    m = m01.astype(BF16)
    return (jnp.dot(m, hi, preferred_element_type=F32) + jnp.dot(m, mid, preferred_element_type=F32)
            + jnp.dot(m, lo, preferred_element_type=F32))


def _dot_x01(x, m01):
    hi, mid, lo = _split3(x)
    m = m01.astype(BF16)
    return (jnp.dot(hi, m, preferred_element_type=F32) + jnp.dot(mid, m, preferred_element_type=F32)
            + jnp.dot(lo, m, preferred_element_type=F32))


def _head_sums(x, hsum, hexp):
    return _dot_x01(_dot_x01(x, hsum), hexp)


def _sigmoid(x):
    return 1.0 / (1.0 + jnp.exp(-x))


def _layer_norm_rows(z, g, b):
    mu = jnp.mean(z, axis=-1, keepdims=True)
    d = z - mu
    var = jnp.mean(d * d, axis=-1, keepdims=True)
    return d * lax.rsqrt(var + LN_EPS) * g + b


def _mm_kernel(a_ref, w_ref, o_ref):
    o_ref[...] = jnp.dot(a_ref[...], w_ref[...], preferred_element_type=F32).astype(o_ref.dtype)


def _matmul(a, w, layer, tm, tn, out_dtype=F32):
    m, k = a.shape
    n = w.shape[2]
    tm = min(tm, m)
    return pl.pallas_call(
        _mm_kernel,
        grid=(n // tn, m // tm),
        in_specs=[pl.BlockSpec((tm, k), lambda j, i: (i, 0)),
                  pl.BlockSpec((None, k, tn), lambda j, i: (layer, 0, j))],
        out_specs=pl.BlockSpec((tm, tn), lambda j, i: (i, j)),
        out_shape=jax.ShapeDtypeStruct((m, n), out_dtype),
        compiler_params=_cparams(("parallel", "parallel")),
        name="in_proj",
    )(a, w)


PREP_TT = 128
V_MUP_R, V_MUP_K, V_MUP_V, V_MUN_R, V_MUN_K, V_MUN_V, V_KK, V_KA, V_RK, V_W0F, V_W0B, V_A0F, V_A0B = range(13)


def _shift(cur, prv8, nxt8, mup, mun, i, nt):
    tt = cur.shape[0]
    row = lax.broadcasted_iota(I32, cur.shape, 0)
    first = jnp.where(i > 0, prv8[7:8, :], 0.0)
    last = jnp.where(i < nt - 1, nxt8[0:1, :], 0.0)
    prev = jnp.where(row == 0, first, pltpu.roll(cur, 1, 0))
    nxt = jnp.where(row == tt - 1, last, pltpu.roll(cur, tt - 1, 0))
    return cur + mup * (prev - cur) + mun * (nxt - cur)


def _prep_kernel(rc, kc, vc, lc, rp, kp, vp, lp, rn, kn, vn, ln_, vec_ref, lvec_ref, w2_ref, a2_ref, g2_ref,
                 hs_ref, he_ref, r_o, v_o, kk_o, lwf_o, kf_o, bf_o, lwb_o, kb_o, bb_o, bonus_o, g_o):
    i = pl.program_id(1)
    nt = pl.num_programs(1)
    vec = vec_ref[...]
    row = lambda j: vec[j:j + 1, :]
    r = _shift(rc[0], rp[0], rn[0], row(V_MUP_R), row(V_MUN_R), i, nt)
    k = _shift(kc[0], kp[0], kn[0], row(V_MUP_K), row(V_MUN_K), i, nt)
    v = _shift(vc[0], vp[0], vn[0], row(V_MUP_V), row(V_MUN_V), i, nt)
    lo = _shift(lc[0], lp[0], ln_[0], lvec_ref[0:1, :], lvec_ref[1:2, :], i, nt)

    hsum, hexp = hs_ref[...], he_ref[...]
    kk = k * row(V_KK)
    kk = kk / jnp.maximum(jnp.sqrt(_head_sums(kk * kk, hsum, hexp)), 1e-12)

    tanh_lo = jnp.tanh(lo)
    ksum = jnp.zeros_like(k)
    outs = ((lwf_o, kf_o, bf_o), (lwb_o, kb_o, bb_o))
    for d in range(2):
        z = row(V_W0F + d) + _dot(tanh_lo, w2_ref[d])
        nz = -z
        softplus = jnp.maximum(nz, 0.0) + jnp.log(1.0 + jnp.exp(-jnp.abs(nz)))
        w_log = -softplus - 0.5
        lw = -jnp.exp(w_log)
        a = _sigmoid(row(V_A0F + d) + _dot(lo, a2_ref[d]))
        k_d = k * (1.0 + (a - 1.0) * row(V_KA))
        ksum = ksum + k_d
        lw_o, kd_o, bd_o = outs[d]
        lw_o[0] = lw
        kd_o[0] = k_d
        bd_o[0] = kk * a
    r_o[0] = r
    v_o[0] = v
    kk_o[0] = kk
    bonus_o[0] = _head_sums(r * ksum * row(V_RK), hsum, hexp) * v
    g_o[0] = _dot(_sigmoid(lo), g2_ref[...])


def _rwkv_prep(proj3, vec, lvec, w2f, a2f, g2f, hsum, hexp):
    b, t, _ = proj3.shape
    tt = PREP_TT
    nt = t // tt
    h8 = tt // 8
    cur = lambda cb, w: pl.BlockSpec((1, tt, w), lambda bi, i: (bi, i, cb))
    prv = lambda cb, w: pl.BlockSpec((1, 8, w), lambda bi, i: (bi, jnp.maximum(i * h8 - 1, 0), cb))
    nxt = lambda cb, w: pl.BlockSpec((1, 8, w), lambda bi, i: (bi, jnp.minimum((i + 1) * h8, t // 8 - 1), cb))
    cols = [(COL_R // A_WIDTH, A_WIDTH), (COL_K // A_WIDTH, A_WIDTH), (COL_V // A_WIDTH, A_WIDTH),
            (COL_L // LORA_PAD, LORA_PAD)]
    full = lambda shape: pl.BlockSpec(shape, lambda bi, i: (0,) * len(shape))
    in_specs = ([cur(*c) for c in cols] + [prv(*c) for c in cols] + [nxt(*c) for c in cols]
                + [full(vec.shape), full(lvec.shape), full(w2f.shape), full(a2f.shape), full(g2f.shape),
                   full(hsum.shape), full(hexp.shape)])
    out_spec = pl.BlockSpec((1, tt, A_WIDTH), lambda bi, i: (bi, i, 0))
    out_sds = jax.ShapeDtypeStruct((b, t, A_WIDTH), F32)
    return pl.pallas_call(
        _prep_kernel,
        grid=(b, nt),
        in_specs=in_specs,
        out_specs=[out_spec] * 11,
        out_shape=[out_sds] * 11,
        compiler_params=_cparams(("parallel", "parallel")),
        name="rwkv_prep",
    )(*([proj3] * 12), vec, lvec, w2f, a2f, g2f, hsum, hexp)


def _stack2(x, m0):
    return jnp.concatenate([jnp.where(m0, x, 0.0), jnp.where(m0, 0.0, x)], axis=0)


class _Chain(NamedTuple):
    r: jax.Array
    v: jax.Array
    kk: jax.Array
    lw: jax.Array
    kd: jax.Array
    bd: jax.Array
    s_ref: object
    rev: bool


def _scan_masks(c, reverse):
    ti = lax.broadcasted_iota(I32, (c, c), 0)
    si = lax.broadcasted_iota(I32, (c, c), 1)
    tri = jnp.where((si >= ti) if reverse else (si <= ti), 1.0, 0.0).astype(BF16)
    t2 = lax.broadcasted_iota(I32, (c, 2 * c), 0)
    s2 = lax.broadcasted_iota(I32, (c, 2 * c), 1)
    s2 = jnp.where(s2 >= c, s2 - c, s2)
    strict = (s2 > t2) if reverse else (s2 < t2)
    incl = (s2 >= t2) if reverse else (s2 <= t2)
    eye2 = jnp.where(s2 == t2, 1.0, 0.0)
    return tri, strict, incl, eye2


def _scan_chunk(chains):
    c = SCAN_CHUNK
    masks = {rev: _scan_masks(c, rev) for rev in (False, True)}
    m0 = lax.broadcasted_iota(I32, (c, LANES), 1) < A_HEAD_DIM
    half = lax.broadcasted_iota(I32, (c, 2 * c), 1) < c
    vi = lax.broadcasted_iota(I32, (LANES, LANES), 0)
    ki = lax.broadcasted_iota(I32, (LANES, LANES), 1)
    same_head = (vi < A_HEAD_DIM) == (ki < A_HEAD_DIM)
    bf = lambda x: x.astype(BF16)
    mm = lambda a, b: jnp.dot(a, b, preferred_element_type=F32)
    mm_nt = lambda a, b: lax.dot_general(a, b, (((1,), (1,)), ((), ())), preferred_element_type=F32)
    cat = jnp.concatenate

    cum = []
    for ch in chains:
        tri = masks[ch.rev][0]
        parts = mm(tri, cat(_split3(ch.lw), axis=1))
        cum.append(parts[:, :LANES] + parts[:, LANES:2 * LANES] + parts[:, 2 * LANES:])

    ar, bk2, bkw, v2, etot = [], [], [], [], []
    for ch, cm in zip(chains, cum):
        tot = cm[0:1, :] if ch.rev else cm[c - 1:c, :]
        einv = jnp.exp(-cm)
        etail = jnp.exp(tot - cm)
        at = -ch.kk * jnp.exp(cm - ch.lw)
        rt = ch.r * jnp.exp(cm)
        ar.append(bf(cat([at, rt], axis=0)))
        bk2.append(bf(cat([_stack2(ch.bd * einv, m0), _stack2(ch.kd * einv, m0)], axis=0)))
        bkw.append(bf(cat([ch.bd * etail, ch.kd * etail], axis=0)))
        v2.append(bf(_stack2(ch.v, m0)))
        etot.append(jnp.exp(tot))

    s0 = [ch.s_ref[...] for ch in chains]
    both = [mm_nt(a, cat([b, bf(s)], axis=0)) for a, b, s in zip(ar, bk2, s0)]
    mt = [r[:, :4 * c] for r in both]
    ars = [r[:, 4 * c:] for r in both]
    a_ab, a_xk, a_rb = [], [], []
    for ch, m in zip(chains, mt):
        _, strict, incl, _ = masks[ch.rev]
        a_ab.append(jnp.where(strict, m[:c, :2 * c], 0.0))
        a_xk.append(bf(cat([jnp.where(strict, m[:c, 2 * c:], 0.0), jnp.where(incl, m[c:, 2 * c:], 0.0)], axis=0)))
        a_rb.append(bf(jnp.where(incl, m[c:, :2 * c], 0.0)))
    xkv = [mm(a, v) for a, v in zip(a_xk, v2)]

    tm = [masks[ch.rev][3] + a for ch, a in zip(chains, a_ab)]
    pw = [mm(bf(a), _stack2(bf(a), half)) for a in a_ab]
    steps = int(np.log2(c)) - 1
    for k in range(steps):
        blk = [_stack2(bf(x), half) for x in pw]
        if k < steps - 1:
            both = [mm(bf(cat([t, x], axis=0)), d) for t, x, d in zip(tm, pw, blk)]
            tm = [t + r[:c] for t, r in zip(tm, both)]
            pw = [r[c:] for r in both]
        else:
            tm = [t + mm(bf(t), d) for t, d in zip(tm, blk)]

    u = [mm(bf(t), bf(_stack2(a[:c] + k[:c], m0))) for t, a, k in zip(tm, ars, xkv)]
    y = [a[c:] + k[c:] + mm(ab, bf(_stack2(uu, m0))) for a, k, ab, uu in zip(ars, xkv, a_rb, u)]
    upd = [mm(bf(cat([uu, ch.v], axis=0).T), w) for uu, ch, w in zip(u, chains, bkw)]
    for ch, s, e, up in zip(chains, s0, etot, upd):
        ch.s_ref[...] = s * e + jnp.where(same_head, up, 0.0)
    return y


SCAN_HP = 8


def _scan_kernel(rf, vf, kkf, lwf, kf, bf, rb, vb, kkb, lwb, kb, bb, yf_o, yb_o, sf_ref, sb_ref):
    @pl.when(pl.program_id(2) == 0)
    def _():
        sf_ref[...] = jnp.zeros_like(sf_ref)
        sb_ref[...] = jnp.zeros_like(sb_ref)

    chains = []
    for hp in range(SCAN_HP):
        ln = slice(hp * LANES, (hp + 1) * LANES)
        chains.append(_Chain(rf[0, :, ln], vf[0, :, ln], kkf[0, :, ln], lwf[0, :, ln], kf[0, :, ln], bf[0, :, ln],
                             sf_ref.at[hp], False))
        chains.append(_Chain(rb[0, :, ln], vb[0, :, ln], kkb[0, :, ln], lwb[0, :, ln], kb[0, :, ln], bb[0, :, ln],
                             sb_ref.at[hp], True))
    y = _scan_chunk(chains)
    for hp in range(SCAN_HP):
        ln = slice(hp * LANES, (hp + 1) * LANES)
        yf_o[0, :, ln] = y[2 * hp]
        yb_o[0, :, ln] = y[2 * hp + 1]


def _rwkv_scan(r, v, kk, lwf, kf, bf, lwb, kb, bb):
    b, t, _ = r.shape
    c = SCAN_CHUNK
    nc = t // c
    w = SCAN_HP * LANES
    fwd = pl.BlockSpec((1, c, w), lambda bi, hp, ci: (bi, ci, hp))
    bwd = pl.BlockSpec((1, c, w), lambda bi, hp, ci: (bi, nc - 1 - ci, hp))
    out_sds = jax.ShapeDtypeStruct((b, t, A_WIDTH), F32)
    return pl.pallas_call(
        _scan_kernel,
        grid=(b, A_WIDTH // w, nc),
        in_specs=[fwd] * 6 + [bwd] * 6,
        out_specs=[fwd, bwd],
        out_shape=[out_sds, out_sds],
        scratch_shapes=[pltpu.VMEM((SCAN_HP, LANES, LANES), F32), pltpu.VMEM((SCAN_HP, LANES, LANES), F32)],
        compiler_params=_cparams(("parallel", "parallel", "arbitrary")),
        name="rwkv_scan",
    )(r, v, kk, lwf, kf, bf, r, v, kk, lwb, kb, bb)


POST_TT = 256


def _post_kernel(yf, yb, bonus, g, vec_ref, hs_ref, he_ref, o_ref):
    y = yf[...] + yb[...]
    hsum, hexp = hs_ref[...], he_ref[...]
    inv = 1.0 / A_HEAD_DIM
    mu = _head_sums(y, hsum, hexp) * inv
    d = y - mu
    var = _head_sums(d * d, hsum, hexp) * inv
    yn = d * lax.rsqrt(var + GN_EPS) * vec_ref[0:1, :] + vec_ref[1:2, :]
    o_ref[...] = ((yn + bonus[...]) * g[...]).astype(o_ref.dtype)


def _rwkv_post(yf, yb, bonus, g, gnvec, hsum, hexp):
    n = yf.shape[0]
    tt = min(POST_TT, n)
    blk = pl.BlockSpec((tt, A_WIDTH), lambda i: (i, 0))
    full = lambda a: pl.BlockSpec(a.shape, lambda i: (0,) * a.ndim, pipeline_mode=pl.Buffered(1))
    return pl.pallas_call(
        _post_kernel,
        grid=(n // tt,),
        in_specs=[blk, blk, blk, blk, full(gnvec), full(hsum), full(hexp)],
        out_specs=blk,
        out_shape=jax.ShapeDtypeStruct((n, A_WIDTH), BF16),
        compiler_params=_cparams(("parallel",)),
        name="rwkv_post",
    )(yf, yb, bonus, g, gnvec, hsum, hexp)


def _gelu_tanh(x):
    return 0.5 * x * (1.0 + jnp.tanh(np.sqrt(2.0 / np.pi).astype(np.float32) * (x + 0.044715 * (x * x * x))))


def _sg_kernel(pb_ref, lnv_ref, w_ref, bias_ref, o_ref):
    z = _gelu_tanh(pb_ref[0])
    u = z[:, :B_WIDTH]
    v = _layer_norm_rows(z[:, B_WIDTH:], lnv_ref[0:1, :], lnv_ref[1:2, :])
    lane = lax.broadcasted_iota(I32, (B_CHUNK, LANES), 1)
    m0 = lane < B_GROUP_DIM
    parts = []
    for q in range(B_WIDTH // LANES):
        vq = v[:, q * LANES:(q + 1) * LANES]
        parts.append(_dot(w_ref[q], _stack2(vq, m0)))
    mixed = jnp.concatenate(parts, axis=1) + bias_ref[...]
    o_ref[0] = (u * mixed).astype(o_ref.dtype)


def _spatial_gating(proj3, lnvec, w2, bias_full):
    b, t, _ = proj3.shape
    full = lambda a: pl.BlockSpec(a.shape, lambda bi, i: (0,) * a.ndim)
    return pl.pallas_call(
        _sg_kernel,
        grid=(b, t // B_CHUNK),
        in_specs=[pl.BlockSpec((1, B_CHUNK, B_COLS), lambda bi, i: (bi, i, COL_B // B_COLS)),
                  full(lnvec), full(w2), full(bias_full)],
        out_specs=pl.BlockSpec((1, B_CHUNK, B_WIDTH), lambda bi, i: (bi, i, 0)),
        out_shape=jax.ShapeDtypeStruct((b, t, B_WIDTH), BF16),
        compiler_params=_cparams(("parallel", "parallel")),
        name="spatial_gating",
    )(proj3, lnvec, w2, bias_full)


NA_WIN = C_WIN_ROWS * GRID_W


def _na_row_start(i, rows):
    return jnp.clip(i - C_WIN_ROWS // 2, 0, rows - C_WIN_ROWS)


NA_QR = 2


def _na_kernel(q_ref, k_ref, v_ref, *rest, rows):
    bias_refs, o_ref = rest[:NA_QR], rest[NA_QR]
    kh_ref, vh_ref = rest[NA_QR + 1:]
    i = pl.program_id(1)
    scale = C_HEAD_DIM ** -0.5

    @pl.when(i == 0)
    def _():
        for h in range(C_HEADS):
            sl = slice(h * C_HEAD_DIM, (h + 1) * C_HEAD_DIM)
            kh_ref[h] = k_ref[0, :, sl].astype(BF16)
            vh_ref[h] = v_ref[0, :, sl].astype(BF16)

    qs, ks, vs, bs = [], [], [], []
    for j in range(NA_QR):
        start = pl.multiple_of(_na_row_start(i * NA_QR + j, rows) * GRID_W, GRID_W)
        q = q_ref[0, j * GRID_W:(j + 1) * GRID_W, :]
        for h in range(C_HEADS):
            sl = slice(h * C_HEAD_DIM, (h + 1) * C_HEAD_DIM)
            qs.append(q[:, sl].astype(BF16))
            ks.append(kh_ref[h, pl.ds(start, NA_WIN), :])
            vs.append(vh_ref[h, pl.ds(start, NA_WIN), :])
            bs.append(bias_refs[j][0, h])
    nt = lambda a, b: lax.dot_general(a, b, (((1,), (1,)), ((), ())), preferred_element_type=F32)
    ss = [nt(q, k) * scale + bias for q, k, bias in zip(qs, ks, bs)]
    ps = []
    for s in ss:
        e = jnp.exp(s - jnp.max(s, axis=-1, keepdims=True))
        ps.append((e / jnp.sum(e, axis=-1, keepdims=True)).astype(BF16))
    os_ = [jnp.dot(p, v, preferred_element_type=F32) for p, v in zip(ps, vs)]
    for j in range(NA_QR):
        o_ref[0, j * GRID_W:(j + 1) * GRID_W, :] = jnp.concatenate(
            os_[j * C_HEADS:(j + 1) * C_HEADS], axis=1).astype(o_ref.dtype)


def _neighbourhood_attention(proj3, bias_tab):
    b, t, _ = proj3.shape
    rows = t // GRID_W
    cq = COL_C // C_WIDTH
    qr = NA_QR
    seq = lambda cb: pl.BlockSpec((1, t, C_WIDTH), lambda bi, i: (bi, 0, cb))
    bias = lambda j: pl.BlockSpec((1, C_HEADS, GRID_W, NA_WIN),
                                  lambda bi, i: (i * qr + j - _na_row_start(i * qr + j, rows), 0, 0, 0))
    return pl.pallas_call(
        functools.partial(_na_kernel, rows=rows),
        grid=(b, rows // qr),
        in_specs=[pl.BlockSpec((1, qr * GRID_W, C_WIDTH), lambda bi, i: (bi, i, cq)), seq(cq + 1), seq(cq + 2)]
                 + [bias(j) for j in range(qr)],
        out_specs=pl.BlockSpec((1, qr * GRID_W, C_WIDTH), lambda bi, i: (bi, i, 0)),
        out_shape=jax.ShapeDtypeStruct((b, t, C_WIDTH), BF16),
        scratch_shapes=[pltpu.VMEM((C_HEADS, t, C_HEAD_DIM), BF16), pltpu.VMEM((C_HEADS, t, C_HEAD_DIM), BF16)],
        compiler_params=_cparams(("parallel", "arbitrary")),
        name="nbr_attention",
    )(proj3, proj3, proj3, *([bias_tab] * qr))


def _na_bias_table(rpb, rows):
    kc = C_WIN_COLS
    cols = np.arange(GRID_W)
    col_start = np.clip(cols - kc // 2, 0, GRID_W - kc)
    key_col = np.arange(GRID_W)
    in_win = (key_col[None, :] >= col_start[:, None]) & (key_col[None, :] < col_start[:, None] + kc)
    delta = np.arange(C_WIN_ROWS)
    row_off = np.arange(C_WIN_ROWS)[None, :] - delta[:, None] + (C_WIN_ROWS - 1)
    by_row = rpb[:, row_off]
    span = 2 * GRID_W
    lead = GRID_W - C_WIN_COLS
    padded = jnp.pad(by_row, [(0, 0)] * 3 + [(lead, span - lead - (2 * C_WIN_COLS - 1))])
    skew = jnp.tile(padded, GRID_W)[..., :GRID_W * (span - 1)].reshape(by_row.shape[:3] + (GRID_W, span - 1))
    bias = skew[..., GRID_W - 1:]
    bias = jnp.where(in_win, bias, -1e30)
    bias = jnp.transpose(bias, (1, 0, 3, 2, 4)).reshape(C_WIN_ROWS, C_HEADS, GRID_W, NA_WIN)
    return bias.astype(F32)


MERGE_TM = 512


def _merge_kernel(ya, yb, yc, ga, gb, gc, pa, pb, pc, o_ref):
    m = _sigmoid(ga[...]) * jnp.dot(ya[...], pa[...], preferred_element_type=F32)
    m = m + _sigmoid(gb[...]) * jnp.dot(yb[...], pb[...], preferred_element_type=F32)
    m = m + _sigmoid(gc[...]) * jnp.dot(yc[...], pc[...], preferred_element_type=F32)
    o_ref[...] = m.astype(o_ref.dtype)


def _merge(ya, yb, yc, proj, p_a, p_b, p_c):
    n = ya.shape[0]
    tm = min(MERGE_TM, n)
    rowblk = lambda w: pl.BlockSpec((tm, w), lambda i: (i, 0))
    gate = lambda j: pl.BlockSpec((tm, D_MODEL), lambda i: (i, COL_G // D_MODEL + j))
    full = lambda a: pl.BlockSpec(a.shape, lambda i: (0,) * a.ndim, pipeline_mode=pl.Buffered(1))
    return pl.pallas_call(
        _merge_kernel,
        grid=(n // tm,),
        in_specs=[rowblk(A_WIDTH), rowblk(B_WIDTH), rowblk(C_WIDTH), gate(0), gate(1), gate(2),
                  full(p_a), full(p_b), full(p_c)],
        out_specs=rowblk(D_MODEL),
        out_shape=jax.ShapeDtypeStruct((n, D_MODEL), BF16),
        compiler_params=_cparams(("parallel",)),
        name="branch_merge",
    )(ya, yb, yc, proj, proj, proj, p_a, p_b, p_c)


OUT_TM = 512


def _outln_kernel(m_ref, w_ref, x_ref, lnv_ref, o_ref, ob_ref):
    h = jnp.dot(m_ref[...], w_ref[...], preferred_element_type=F32)
    y = _layer_norm_rows(DEEPNORM_ALPHA * x_ref[...] + h, lnv_ref[0:1, :], lnv_ref[1:2, :])
    o_ref[...] = y
    ob_ref[...] = y.astype(BF16)


def _out_proj_ln(m, w_out, x, lnvec):
    n = m.shape[0]
    tm = min(OUT_TM, n)
    blk = pl.BlockSpec((tm, D_MODEL), lambda i: (i, 0))
    full = lambda a: pl.BlockSpec(a.shape, lambda i: (0,) * a.ndim, pipeline_mode=pl.Buffered(1))
    return pl.pallas_call(
        _outln_kernel,
        grid=(n // tm,),
        in_specs=[blk, full(w_out), blk, full(lnvec)],
        out_specs=[blk, blk],
        out_shape=[jax.ShapeDtypeStruct((n, D_MODEL), F32), jax.ShapeDtypeStruct((n, D_MODEL), BF16)],
        compiler_params=_cparams(("parallel",)),
        name="out_proj_ln",
    )(m, w_out, x, lnvec)


ROUTER_TM = 512


def _router_kernel(x_ref, wt_ref, o_ref):
    xh, xl = _split2(x_ref[...])
    wh, wl = _split2(wt_ref[...])
    nt = lambda a, b: lax.dot_general(a, b, (((1,), (1,)), ((), ())), preferred_element_type=F32)
    logits = nt(wh, xh) + nt(wl, xh) + nt(wh, xl)
    logits = logits - jnp.max(logits, axis=0, keepdims=True)
    e = jnp.exp(logits)
    o_ref[...] = e / jnp.sum(e, axis=0, keepdims=True)


def _router(x, w_router_t):
    n = x.shape[0]
    tm = min(ROUTER_TM, n)
    return pl.pallas_call(
        _router_kernel,
        grid=(n // tm,),
        in_specs=[pl.BlockSpec((tm, D_MODEL), lambda i: (i, 0)),
                  pl.BlockSpec(w_router_t.shape, lambda i: (0, 0))],
        out_specs=pl.BlockSpec((N_EXPERTS, tm), lambda i: (0, i)),
        out_shape=jax.ShapeDtypeStruct((N_EXPERTS, n), F32),
        compiler_params=_cparams(("parallel",)),
        name="router",
    )(x, w_router_t)


def _select_kernel(aff_ref, pos_ref, off_ref, *, cap):
    aff = aff_ref[...]
    e_, g_, l_ = aff.shape
    n = g_ * l_
    bits = pltpu.bitcast(aff, I32)

    def count(mask):
        c = jnp.sum(jnp.where(mask, 1.0, 0.0), axis=2, keepdims=True)
        return jnp.sum(c, axis=1, keepdims=True)

    def thr_body(_, carry):
        lo, hi = carry
        mid = lo + (hi - lo + 1) // 2
        ok = count(bits >= mid) >= cap
        return jnp.where(ok, mid, lo), jnp.where(ok, hi, mid - 1)

    lo0 = jnp.zeros((e_, 1, 1), I32)
    hi0 = jnp.full((e_, 1, 1), 0x7F800000, I32)
    thr, _ = lax.fori_loop(0, 32, thr_body, (lo0, hi0))
    gt = bits > thr
    tie = bits == thr
    need = cap - count(gt)
    idx = lax.broadcasted_iota(I32, aff.shape, 1) * l_ + lax.broadcasted_iota(I32, aff.shape, 2)

    def idx_body(_, carry):
        lo, hi = carry
        mid = (lo + hi) // 2
        ok = count(tie & (idx <= mid)) >= need
        return jnp.where(ok, lo, mid + 1), jnp.where(ok, mid, hi)

    lo1 = jnp.zeros((e_, 1, 1), I32)
    hi1 = jnp.full((e_, 1, 1), n - 1, I32)
    cut, _ = lax.fori_loop(0, int(np.ceil(np.log2(n))) + 1, idx_body, (lo1, hi1))
    sel = jnp.where(gt | (tie & (idx <= cut)), 1.0, 0.0)

    sel2 = sel.reshape(e_ * g_, l_)
    ls = lax.broadcasted_iota(I32, (l_, l_), 0)
    lt = lax.broadcasted_iota(I32, (l_, l_), 1)
    incl = jnp.dot(sel2.astype(BF16), jnp.where(ls <= lt, 1.0, 0.0).astype(BF16), preferred_element_type=F32)
    tot = jnp.dot(sel2.astype(BF16), jnp.ones((l_, l_), BF16), preferred_element_type=F32)
    gs = lax.broadcasted_iota(I32, (g_, g_), 0)
    gt_ = lax.broadcasted_iota(I32, (g_, g_), 1)
    lower = jnp.where(gt_ < gs, 1.0, 0.0).astype(BF16)
    offs = [jnp.dot(lower, tot[e * g_:(e + 1) * g_].astype(BF16), preferred_element_type=F32)
            for e in range(e_)]
    off = jnp.concatenate(offs, axis=0)
    pos = incl + off - sel2
    pos_ref[...] = jnp.where(sel2 > 0.5, pos, -1.0).astype(I32).reshape(e_, g_, l_)
    off_ref[...] = off.astype(I32).reshape(e_, g_, l_)


def _select(aff3, cap):
    full = pl.BlockSpec(aff3.shape, lambda i: (0, 0, 0))
    sds = jax.ShapeDtypeStruct(aff3.shape, I32)
    return pl.pallas_call(
        functools.partial(_select_kernel, cap=cap),
        grid=(1,),
        in_specs=[full],
        out_specs=[full, full],
        out_shape=[sds, sds],
        compiler_params=_cparams(("arbitrary",)),
        name="expert_select",
    )(aff3)


def _gather_kernel(tile_s, valid_s, e_g, slab_g, first_g, *refs):
    k_ = GATHER_GROUP
    x_refs, pos_refs, aff_refs = refs[:k_], refs[k_:2 * k_], refs[2 * k_:3 * k_]
    o_ref, g_ref = refs[3 * k_:]
    g = pl.program_id(0)
    t = MOE_TILE

    @pl.when(first_g[g] == 1)
    def _():
        o_ref[...] = jnp.zeros_like(o_ref)
        g_ref[...] = jnp.zeros_like(g_ref)

    nvalid = valid_s[g * k_]
    for k in range(1, k_):
        nvalid = nvalid + valid_s[g * k_ + k]

    @pl.when(nvalid > 0)
    def _():
        rank = lax.broadcasted_iota(I32, (t, GATHER_TOK), 0) + slab_g[g] * t
        got = o_ref[0].astype(F32)
        gate = g_ref[0]
        per_dot = MXU_DIM // GATHER_TOK
        for k0 in range(0, k_, per_dot):
            onehots, xs = [], []
            for k in range(k0, k0 + per_dot):
                want = jnp.where(valid_s[g * k_ + k] == 1, rank, -2)
                hit = pos_refs[k][0] == want
                onehots.append(jnp.where(hit, 1.0, 0.0).astype(BF16))
                xs.append(x_refs[k][...])
                gate = gate + jnp.sum(jnp.where(hit, aff_refs[k][0], 0.0), axis=1, keepdims=True)
            got = got + jnp.dot(jnp.concatenate(onehots, axis=1), jnp.concatenate(xs, axis=0),
                                preferred_element_type=F32)
        o_ref[0] = got.astype(o_ref.dtype)
        g_ref[0] = gate


def _moe_gather(x_bf, pos_en, aff_en, items, cap):
    n = x_bf.shape[0]
    t = MOE_TILE
    k_ = GATHER_GROUP
    ng = items[2].shape[0]
    tk = GATHER_TOK
    xs = [pl.BlockSpec((tk, D_MODEL), lambda g, tl, va, e, s, fi, k=k: (tl[g * k_ + k], 0)) for k in range(k_)]
    rows = [pl.BlockSpec((1, 1, tk), lambda g, tl, va, e, s, fi, k=k: (e[g], 0, tl[g * k_ + k])) for k in range(k_)]
    grid_spec = pltpu.PrefetchScalarGridSpec(
        num_scalar_prefetch=5,
        grid=(ng,),
        in_specs=xs + rows + rows,
        out_specs=[pl.BlockSpec((1, t, D_MODEL), lambda g, tl, va, e, s, fi: (e[g], s[g], 0)),
                   pl.BlockSpec((1, t, 1), lambda g, tl, va, e, s, fi: (e[g], s[g], 0))],
    )
    pos3 = pos_en.reshape(N_EXPERTS, 1, n)
    aff3 = aff_en.reshape(N_EXPERTS, 1, n)
    return pl.pallas_call(
        _gather_kernel,
        grid_spec=grid_spec,
        out_shape=[jax.ShapeDtypeStruct((N_EXPERTS, cap, D_MODEL), BF16),
                   jax.ShapeDtypeStruct((N_EXPERTS, cap, 1), F32)],
        compiler_params=_cparams(("arbitrary",)),
        name="moe_gather",
    )(*items, *([x_bf] * k_), *([pos3] * k_), *([aff3] * k_))


FFN_TM = 512


def _ffn_kernel(x_ref, gate_ref, wg_ref, wu_ref, wd_ref, o_ref):
    x = x_ref[0]
    g = jnp.dot(x, wg_ref[0], preferred_element_type=F32)
    u = jnp.dot(x, wu_ref[0], preferred_element_type=F32)
    h = (g * _sigmoid(g)) * u
    y = jnp.dot(h.astype(BF16), wd_ref[0], preferred_element_type=F32) * gate_ref[0]
    o_ref[0] = y.astype(o_ref.dtype)


def _expert_ffn(xe, gate, wg, wu, wd, layer):
    e_, cap, _ = xe.shape
    tm = min(FFN_TM, cap)
    wspec = lambda a: pl.BlockSpec((None, 1) + a.shape[2:], lambda e, i: (layer, e, 0, 0))
    rows = lambda w: pl.BlockSpec((1, tm, w), lambda e, i: (e, i, 0))
    sds = jax.ShapeDtypeStruct((e_, cap, D_MODEL), BF16)
    return pl.pallas_call(
        _ffn_kernel,
        grid=(e_, cap // tm),
        in_specs=[rows(D_MODEL), rows(1), wspec(wg), wspec(wu), wspec(wd)],
        out_specs=rows(D_MODEL),
        out_shape=sds,
        compiler_params=_cparams(("parallel", "arbitrary")),
        name="expert_ffn",
    )(xe, gate, wg, wu, wd)


def _combine_kernel(e_s, slab_s, valid_s, tile_g, first_g, last_g, *refs):
    k_ = COMBINE_GROUP
    ye_refs = refs[:k_]
    pos_ref, x_ref, lnv_ref, o_ref, ob_ref, acc_ref = refs[k_:]
    g = pl.program_id(0)
    t = MOE_TILE

    @pl.when(first_g[g] == 1)
    def _():
        acc_ref[...] = jnp.zeros_like(acc_ref)

    nvalid = valid_s[g * k_]
    for k in range(1, k_):
        nvalid = nvalid + valid_s[g * k_ + k]

    @pl.when(nvalid > 0)
    def _():
        posf = pos_ref[...].astype(F32)
        lane = lax.broadcasted_iota(I32, (t, N_EXPERTS), 1)
        row = lax.broadcasted_iota(I32, (t, MOE_SUB), 1)
        onehots = []
        for k in range(k_):
            s = g * k_ + k
            pos = jnp.sum(jnp.where(lane == e_s[s], posf, 0.0), axis=1, keepdims=True)
            base = jnp.where(valid_s[s] == 1, slab_s[s] * MOE_SUB, -2 * MOE_SUB)
            onehots.append(jnp.where(pos == (row + base).astype(F32), 1.0, 0.0).astype(BF16))
        onehot = jnp.concatenate(onehots, axis=1)
        ye = jnp.concatenate([r[0] for r in ye_refs], axis=0)
        acc_ref[...] += jnp.dot(onehot, ye, preferred_element_type=F32)

    @pl.when(last_g[g] == 1)
    def _():
        y = _layer_norm_rows(DEEPNORM_ALPHA * x_ref[...] + acc_ref[...], lnv_ref[0:1, :], lnv_ref[1:2, :])
        o_ref[...] = y
        ob_ref[...] = y.astype(BF16)


def _moe_combine(ye, pos_ne, x, lnvec, items):
    n = x.shape[0]
    t = MOE_TILE
    k_ = COMBINE_GROUP
    ng = items[3].shape[0]
    tile_blk = lambda w: pl.BlockSpec((t, w), lambda g, e, s, va, tl, fi, la: (tl[g], 0))
    slabs = [pl.BlockSpec((1, MOE_SUB, D_MODEL), lambda g, e, s, va, tl, fi, la, k=k: (e[g * k_ + k], s[g * k_ + k], 0))
             for k in range(k_)]
    grid_spec = pltpu.PrefetchScalarGridSpec(
        num_scalar_prefetch=6,
        grid=(ng,),
        in_specs=slabs + [tile_blk(N_EXPERTS), tile_blk(D_MODEL),
                          pl.BlockSpec(lnvec.shape, lambda g, e, s, va, tl, fi, la: (0, 0))],
        out_specs=[tile_blk(D_MODEL), tile_blk(D_MODEL)],
        scratch_shapes=[pltpu.VMEM((t, D_MODEL), F32)],
    )
    return pl.pallas_call(
        _combine_kernel,
        grid_spec=grid_spec,
        out_shape=[jax.ShapeDtypeStruct((n, D_MODEL), F32), jax.ShapeDtypeStruct((n, D_MODEL), BF16)],
        compiler_params=_cparams(("arbitrary",)),
        name="moe_combine",
    )(*items, *([ye] * k_), pos_ne, x, lnvec)


def _moe_items(group_off, n, cap):
    e_ = N_EXPERTS
    big = jnp.int32(2 ** 30)

    def build(rows, by_tile, k_, t):
        ns = cap // rows
        nt = n // t
        starts = group_off[:, ::t // LANES, 0]
        ends = jnp.concatenate([starts[:, 1:], jnp.full((e_, 1), cap, I32)], axis=1)
        e_p = jnp.broadcast_to(jnp.arange(e_, dtype=I32)[:, None], (e_, nt))
        tile_p = jnp.broadcast_to(jnp.arange(nt, dtype=I32)[None, :], (e_, nt))
        slab_p = jnp.minimum(starts // rows, ns - 1)
        valid_p = (ends > starts).astype(I32)
        bound = jnp.arange(ns, dtype=I32) * rows
        tile_s = jnp.sum((starts[:, :, None] <= bound[None, None, :]).astype(I32), axis=1) - 1
        tile_s = jnp.clip(tile_s, 0, nt - 1)
        at_tile = tile_s[:, :, None] == jnp.arange(nt, dtype=I32)[None, None, :]
        st_s = jnp.sum(jnp.where(at_tile, starts[:, None, :], 0), axis=2)
        en_s = jnp.sum(jnp.where(at_tile, ends[:, None, :], 0), axis=2)
        valid_s = ((st_s < bound[None, :]) & (bound[None, :] < en_s)).astype(I32)
        e_s = jnp.broadcast_to(jnp.arange(e_, dtype=I32)[:, None], (e_, ns))
        slab_s = jnp.broadcast_to(jnp.arange(ns, dtype=I32)[None, :], (e_, ns))
        cat = lambda a, b: jnp.concatenate([a.reshape(-1), b.reshape(-1)])
        e_a, slab_a, tile_a = cat(e_p, e_s), cat(slab_p, slab_s), cat(tile_p, tile_s)
        valid_a = cat(valid_p, valid_s)
        secondary = cat(jnp.zeros_like(e_p), jnp.ones_like(e_s))
        ni = e_a.shape[0]
        ar = jnp.arange(ni, dtype=I32)
        if by_tile:
            keep = jnp.maximum(valid_a, 1 - secondary)
            key = ((tile_a * e_ + e_a) * 2 + secondary) * ns + slab_a
        else:
            keep = valid_a
            key = (e_a * ns + slab_a) * nt + tile_a
        key = jnp.where(keep == 1, key, big + ar)
        rank = jnp.sum((key[None, :] < key[:, None]).astype(I32), axis=1)
        code = ((e_a * ns + slab_a) * nt + tile_a) * 2 + valid_a
        decode = lambda cd: (cd // (2 * nt * ns), (cd // (2 * nt)) % ns, (cd // 2) % nt, cd % 2)
        ocode = jnp.sum(jnp.where(rank[None, :] == ar[:, None], code[None, :], 0), axis=1)
        nkeep = jnp.sum(keep)
        kept = ar < nkeep
        oe, os_, otl, _ = decode(ocode)
        blk = otl if by_tile else oe * ns + os_

        nblk = nt if by_tile else e_ * ns
        ng = -(-ni // k_) + nblk
        first = jnp.concatenate([jnp.ones((1,), bool), blk[1:] != blk[:-1]])
        run_start = lax.cummax(jnp.where(first, ar, 0))
        sub = (ar - run_start) % k_
        leads = (sub == 0) & kept
        gid = jnp.cumsum(leads.astype(I32)) - 1
        slot = jnp.where(kept, gid * k_ + sub, -1)
        sar = jnp.arange(ng * k_, dtype=I32)
        hit = slot[None, :] == sar[:, None]
        has = jnp.sum(hit.astype(I32), axis=1)
        scode = jnp.sum(jnp.where(hit, ocode[None, :], 0), axis=1)
        n_real = jnp.sum(leads.astype(I32))
        gar = jnp.arange(ng, dtype=I32)
        lead = scode.reshape(ng, k_)[:, 0]
        lead = jnp.where(gar < n_real, lead, jnp.sum(jnp.where(gar == n_real - 1, lead, 0)))
        scode = jnp.where(has == 1, scode, jnp.repeat(lead, k_))
        se, ss, stl, sv = decode(scode)
        ge, gsl, gtl, _ = decode(lead)
        gblk = gtl if by_tile else ge * ns + gsl
        change = (gblk[1:] != gblk[:-1]).astype(I32)
        one = jnp.ones((1,), I32)
        slots = dict(e=se, slab=ss, tile=stl, valid=sv * has)
        groups = dict(e=ge, slab=gsl, tile=gtl, first=jnp.concatenate([one, change]),
                      last=jnp.concatenate([change, one]))
        return slots, groups

    gs, gg = build(MOE_TILE, False, GATHER_GROUP, GATHER_TOK)
    cs, cg = build(MOE_SUB, True, COMBINE_GROUP, MOE_TILE)
    return ((gs["tile"], gs["valid"], gg["e"], gg["slab"], gg["first"]),
            (cs["e"], cs["slab"], cs["valid"], cg["tile"], cg["first"], cg["last"]))


def _expert_choice_moe_ln(x, x_bf, w_router_t, wg, wu, wd, layer, lnvec):
    n = x.shape[0]
    cap = EC_CAPACITY_FACTOR * n // N_EXPERTS
    aff_en = _router(x, w_router_t)
    pos3, off3 = _select(aff_en.reshape(N_EXPERTS, n // LANES, LANES), cap)
    pos_en = pos3.reshape(N_EXPERTS, n)
    g_items, c_items = _moe_items(off3, n, cap)
    xe, gate = _moe_gather(x_bf, pos_en, aff_en, g_items, cap)
    ye = _expert_ffn(xe, gate, wg, wu, wd, layer)
    return _moe_combine(ye, pos_en.T, x, lnvec, c_items)


def _pack_cols(w):
    a_end = A_COLS
    b_end = a_end + B_COLS
    c_end = b_end + C_COLS
    pad = jnp.zeros(w.shape[:-1] + (LORA_PAD - LORA_COLS,), w.dtype)
    return jnp.concatenate([w[..., :3 * A_WIDTH], w[..., a_end:b_end], w[..., c_end:], w[..., b_end:c_end],
                            w[..., 3 * A_WIDTH:a_end], pad], axis=-1)


def _lora_rows(w, start):
    k = w.shape[-2]
    return jnp.pad(w, [(0, 0)] * (w.ndim - 2) + [(start, LORA_PAD - start - k), (0, 0)])


def _prepare(w_in, mu_prev, mu_next, decay_w0, decay_w2, iclr_a0, iclr_a2, gate_g2, k_k, k_a, r_k, gn_g, gn_b,
             sg_ln_g, sg_ln_b, sg_w, sg_b, rpb, p_a, p_b, p_c, w_out, ln_mix_g, ln_mix_b, w_router, e_gate, e_up,
             e_down, ln_ffn_g, ln_ffn_b, rows):
    l_ = w_in.shape[0]
    pad_a = lambda m: jnp.pad(m, ((0, 0), (0, LORA_PAD - LORA_COLS)))
    mup, mun = mu_prev, mu_next
    vec_rows = [mup[:, :A_WIDTH], mup[:, A_WIDTH:2 * A_WIDTH], mup[:, 2 * A_WIDTH:3 * A_WIDTH],
                mun[:, :A_WIDTH], mun[:, A_WIDTH:2 * A_WIDTH], mun[:, 2 * A_WIDTH:3 * A_WIDTH],
                k_k, k_a, r_k.reshape(l_, A_WIDTH), decay_w0[:, 0], decay_w0[:, 1], iclr_a0[:, 0], iclr_a0[:, 1]]
    vec = jnp.stack(vec_rows + [jnp.zeros_like(k_k)] * (16 - len(vec_rows)), axis=1)
    lvec = jnp.stack([pad_a(mup[:, 3 * A_WIDTH:]), pad_a(mun[:, 3 * A_WIDTH:])]
                     + [jnp.zeros((l_, LORA_PAD), F32)] * 6, axis=1)
    w2f = jnp.stack([_lora_rows(decay_w2[:, 0], 0), _lora_rows(decay_w2[:, 1], A_DECAY_LORA)], axis=1)
    a2f = jnp.stack([_lora_rows(iclr_a2[:, 0], 2 * A_DECAY_LORA),
                     _lora_rows(iclr_a2[:, 1], 2 * A_DECAY_LORA + A_ICLR_LORA)], axis=1)
    g2f = _lora_rows(gate_g2, 2 * A_DECAY_LORA + 2 * A_ICLR_LORA)
    head = np.arange(A_WIDTH) // A_HEAD_DIM
    hsum = jnp.asarray(head[:, None] == np.arange(LANES)[None, :], BF16)
    sgw2 = sg_w.reshape(l_, B_GROUPS // 2, 2, B_CHUNK, B_CHUNK).transpose(0, 1, 3, 2, 4)
    sgw2 = sgw2.reshape(l_, B_GROUPS // 2, B_CHUNK, 2 * B_CHUNK)
    sg_bias = jnp.repeat(jnp.swapaxes(sg_b, 1, 2), B_GROUP_DIM, axis=2)
    return dict(
        w_in=_pack_cols(w_in.astype(BF16)), vec=vec, lvec=lvec, w2f=w2f.astype(BF16), a2f=a2f.astype(BF16),
        g2f=g2f.astype(BF16), hsum=hsum, hexp=hsum.T, gnvec=jnp.stack([gn_g, gn_b], axis=1),
        sg_ln=jnp.stack([sg_ln_g, sg_ln_b], axis=1), sgw2=sgw2.astype(BF16), sg_bias=sg_bias,
        na_bias=jnp.stack([_na_bias_table(rpb[l], rows) for l in range(l_)]),
        p_a=p_a.astype(BF16), p_b=p_b.astype(BF16), p_c=p_c.astype(BF16), w_out=w_out.astype(BF16),
        ln_mix=jnp.stack([ln_mix_g, ln_mix_b], axis=1), w_router_t=jnp.swapaxes(w_router, 1, 2),
        e_gate=e_gate.astype(BF16), e_up=e_up.astype(BF16), e_down=e_down.astype(BF16),
        ln_ffn=jnp.stack([ln_ffn_g, ln_ffn_b], axis=1))


def _mixer(x, x_bf, p, l, b, t):
    n = b * t
    proj = _matmul(x_bf, p["w_in"], l, 1024, 1024)
    proj3 = proj.reshape(b, t, IN_PAD)
    r, v, kk, lwf, kf, bf, lwb, kb, bb, bonus, g = _rwkv_prep(
        proj3, p["vec"][l], p["lvec"][l], p["w2f"][l], p["a2f"][l], p["g2f"][l], p["hsum"], p["hexp"])
    yf, yb = _rwkv_scan(r, v, kk, lwf, kf, bf, lwb, kb, bb)
    flat = lambda a: a.reshape(n, a.shape[-1])
    ya = _rwkv_post(flat(yf), flat(yb), flat(bonus), flat(g), p["gnvec"][l], p["hsum"], p["hexp"])
    ybr = _spatial_gating(proj3, p["sg_ln"][l], p["sgw2"][l], p["sg_bias"][l])
    ycr = _neighbourhood_attention(proj3, p["na_bias"][l])
    m = _merge(ya, flat(ybr), flat(ycr), proj, p["p_a"][l], p["p_b"][l], p["p_c"][l])
    return _out_proj_ln(m, p["w_out"][l], x, p["ln_mix"][l])


def _trunk(x3, p):
    b, t, _ = x3.shape
    x = x3.reshape(b * t, D_MODEL)
    x_bf = x.astype(BF16)
    for l in range(DEPTH):
        x, x_bf = _mixer(x, x_bf, p, l, b, t)
        x, x_bf = _expert_choice_moe_ln(x, x_bf, p["w_router_t"][l], p["e_gate"], p["e_up"], p["e_down"], l,
                                        p["ln_ffn"][l])
    return x.reshape(b, t, D_MODEL)


def kernel(x_prompt, x_sample, w_in, mu_prev, mu_next, decay_w0, decay_w2, iclr_a0, iclr_a2, gate_g2, k_k, k_a, r_k, gn_g, gn_b, sg_ln_g, sg_ln_b, sg_w, sg_b, rpb, p_a, p_b, p_c, w_out, ln_mix_g, ln_mix_b, w_router, e_gate, e_up, e_down, ln_ffn_g, ln_ffn_b):
    assert x_prompt.shape[1] == x_sample.shape[1]
    rows = x_prompt.shape[1] // GRID_W
    p = _prepare(w_in, mu_prev, mu_next, decay_w0, decay_w2, iclr_a0, iclr_a2, gate_g2, k_k, k_a, r_k, gn_g, gn_b,
                 sg_ln_g, sg_ln_b, sg_w, sg_b, rpb, p_a, p_b, p_c, w_out, ln_mix_g, ln_mix_b, w_router, e_gate,
                 e_up, e_down, ln_ffn_g, ln_ffn_b, rows)
    return (_trunk(x_prompt, p), _trunk(x_sample, p))
```

```python
import functools
from typing import NamedTuple

import numpy as np
import jax
import jax.numpy as jnp
from jax import lax
from jax.experimental import pallas as pl
from jax.experimental.pallas import tpu as pltpu

F32 = jnp.float32
BF16 = jnp.bfloat16
I32 = jnp.int32

D_MODEL = 2048
DEPTH = 4
GRID_W = 64
A_HEADS = 16
A_HEAD_DIM = 64
A_WIDTH = A_HEADS * A_HEAD_DIM
A_DECAY_LORA = 64
A_ICLR_LORA = 64
A_GATE_LORA = 160
B_GROUPS = 8
B_GROUP_DIM = 64
B_WIDTH = B_GROUPS * B_GROUP_DIM
B_CHUNK = 128
C_HEADS = 8
C_HEAD_DIM = 64
C_WIDTH = C_HEADS * C_HEAD_DIM
C_WIN_ROWS = 8
C_WIN_COLS = 16
N_EXPERTS = 16
EXPERT_HIDDEN = 1024
EC_CAPACITY_FACTOR = 2
A_COLS = 3 * A_WIDTH + 2 * A_DECAY_LORA + 2 * A_ICLR_LORA + A_GATE_LORA
B_COLS = 2 * B_WIDTH
C_COLS = 3 * C_WIDTH
G_COLS = 3 * D_MODEL
DEEPNORM_ALPHA = (2 * DEPTH) ** 0.25
LN_EPS = 1e-5
GN_EPS = 64e-5

LORA_COLS = 2 * A_DECAY_LORA + 2 * A_ICLR_LORA + A_GATE_LORA
LORA_PAD = 512
COL_R, COL_K, COL_V = 0, A_WIDTH, 2 * A_WIDTH
COL_B = 3 * A_WIDTH
COL_L = COL_B + B_COLS
F_COLS = COL_L + LORA_PAD
COL_G = 0
COL_C = G_COLS
H_COLS = COL_C + C_COLS
IN_TN = 1536

LANES = 128
MXU_DIM = 256
SCAN_CHUNK = 64
MOE_TILE = 256
MOE_SUB = 64
GATHER_TOK = 128
GATHER_GROUP = 8
COMBINE_GROUP = 8
VMEM_LIMIT = 48 * 1024 * 1024


def _cparams(sem):
    return pltpu.CompilerParams(dimension_semantics=sem, vmem_limit_bytes=VMEM_LIMIT)


def _dot(a, b):
    return jnp.dot(a.astype(BF16), b.astype(BF16), preferred_element_type=F32)


def _dot_nt(a, b):
    return lax.dot_general(a.astype(BF16), b.astype(BF16), (((1,), (1,)), ((), ())),
                           preferred_element_type=F32)


def _split2(x):
    hi = x.astype(BF16)
    lo = (x - hi.astype(F32)).astype(BF16)
    return hi, lo


def _split3(x):
    hi = x.astype(BF16)
    r1 = x - hi.astype(F32)
    mid = r1.astype(BF16)
    lo = (r1 - mid.astype(F32)).astype(BF16)
    return hi, mid, lo


def _dot_x01(x, m01):
    hi, lo = _split2(x)
    m = m01.astype(BF16)
    return jnp.dot(hi, m, preferred_element_type=F32) + jnp.dot(lo, m, preferred_element_type=F32)


def _head_sums(x, hsum, hexp):
    return _dot_x01(_dot_x01(x, hsum), hexp)


def _sigmoid(x):
    return 1.0 / (1.0 + jnp.exp(-x))


def _layer_norm_rows(z, g, b):
    mu = jnp.mean(z, axis=-1, keepdims=True)
    d = z - mu
    var = jnp.mean(d * d, axis=-1, keepdims=True)
    return d * lax.rsqrt(var + LN_EPS) * g + b


def _mm_kernel(a_ref, w_ref, o_ref):
    o_ref[...] = jnp.dot(a_ref[...], w_ref[...], preferred_element_type=F32).astype(o_ref.dtype)


def _matmul(a, w, layer, tm, tn, out_dtype=F32):
    m, k = a.shape
    n = w.shape[2]
    tm = min(tm, m)
    return pl.pallas_call(
        _mm_kernel,
        grid=(n // tn, m // tm),
        in_specs=[pl.BlockSpec((tm, k), lambda j, i: (i, 0)),
                  pl.BlockSpec((None, k, tn), lambda j, i: (layer, 0, j))],
        out_specs=pl.BlockSpec((tm, tn), lambda j, i: (i, j)),
        out_shape=jax.ShapeDtypeStruct((m, n), out_dtype),
        compiler_params=_cparams(("parallel", "parallel")),
        name="in_proj_" + jnp.dtype(out_dtype).name,
    )(a, w)


PREP_TT = 128
V_MUP_R, V_MUP_K, V_MUP_V, V_MUN_R, V_MUN_K, V_MUN_V, V_KK, V_KA, V_RK, V_W0F, V_W0B, V_A0F, V_A0B = range(13)


def _shift(cur, prv8, nxt8, mup, mun, i, nt):
    tt = cur.shape[0]
    row = lax.broadcasted_iota(I32, cur.shape, 0)
    first = jnp.where(i > 0, prv8[7:8, :], 0.0)
    last = jnp.where(i < nt - 1, nxt8[0:1, :], 0.0)
    prev = jnp.where(row == 0, first, pltpu.roll(cur, 1, 0))
    nxt = jnp.where(row == tt - 1, last, pltpu.roll(cur, tt - 1, 0))
    return cur * (1.0 - mup - mun) + mup * prev + mun * nxt


def _prep_kernel(rc, kc, vc, lc, rp, kp, vp, lp, rn, kn, vn, ln_, vec_ref, lvec_ref, w2_ref, a2_ref, g2_ref,
                 hs_ref, he_ref, r_o, v_o, kk_o, lwf_o, kf_o, bf_o, lwb_o, kb_o, bb_o, bonus_o, g_o):
    i = pl.program_id(1)
    nt = pl.num_programs(1)
    vec = vec_ref[...]
    row = lambda j: vec[j:j + 1, :]
    r = _shift(rc[0], rp[0], rn[0], row(V_MUP_R), row(V_MUN_R), i, nt)
    k = _shift(kc[0], kp[0], kn[0], row(V_MUP_K), row(V_MUN_K), i, nt)
    v = _shift(vc[0], vp[0], vn[0], row(V_MUP_V), row(V_MUN_V), i, nt)
    lo = _shift(lc[0], lp[0], ln_[0], lvec_ref[0:1, :], lvec_ref[1:2, :], i, nt)

    hsum, hexp = hs_ref[...], he_ref[...]
    kk = k * row(V_KK)
    kk = kk / jnp.maximum(jnp.sqrt(_head_sums(kk * kk, hsum, hexp)), 1e-12)

    tanh_lo = jnp.tanh(lo)
    ksum = jnp.zeros_like(k)
    outs = ((lwf_o, kf_o, bf_o), (lwb_o, kb_o, bb_o))
    for d in range(2):
        z = row(V_W0F + d) + _dot(tanh_lo, w2_ref[d])
        nz = -z
        softplus = jnp.maximum(nz, 0.0) + jnp.log(1.0 + jnp.exp(-jnp.abs(nz)))
        w_log = -softplus - 0.5
        lw = -jnp.exp(w_log)
        a = _sigmoid(row(V_A0F + d) + _dot(lo, a2_ref[d]))
        k_d = k * (1.0 + (a - 1.0) * row(V_KA))
        ksum = ksum + k_d
        lw_o, kd_o, bd_o = outs[d]
        lw_o[0] = lw
        kd_o[0] = k_d
        bd_o[0] = kk * a
    r_o[0] = r
    v_o[0] = v
    kk_o[0] = kk
    bonus_o[0] = _head_sums(r * ksum * row(V_RK), hsum, hexp) * v
    g_o[0] = _dot(_sigmoid(lo), g2_ref[...])


def _rwkv_prep(proj3, vec, lvec, w2f, a2f, g2f, hsum, hexp):
    b, t, _ = proj3.shape
    tt = PREP_TT
    nt = t // tt
    h8 = tt // 8
    cur = lambda cb, w: pl.BlockSpec((1, tt, w), lambda bi, i: (bi, i, cb))
    prv = lambda cb, w: pl.BlockSpec((1, 8, w), lambda bi, i: (bi, jnp.maximum(i * h8 - 1, 0), cb))
    nxt = lambda cb, w: pl.BlockSpec((1, 8, w), lambda bi, i: (bi, jnp.minimum((i + 1) * h8, t // 8 - 1), cb))
    cols = [(COL_R // A_WIDTH, A_WIDTH), (COL_K // A_WIDTH, A_WIDTH), (COL_V // A_WIDTH, A_WIDTH),
            (COL_L // LORA_PAD, LORA_PAD)]
    full = lambda shape: pl.BlockSpec(shape, lambda bi, i: (0,) * len(shape))
    in_specs = ([cur(*c) for c in cols] + [prv(*c) for c in cols] + [nxt(*c) for c in cols]
                + [full(vec.shape), full(lvec.shape), full(w2f.shape), full(a2f.shape), full(g2f.shape),
                   full(hsum.shape), full(hexp.shape)])
    out_spec = pl.BlockSpec((1, tt, A_WIDTH), lambda bi, i: (bi, i, 0))
    out_sds = jax.ShapeDtypeStruct((b, t, A_WIDTH), F32)
    return pl.pallas_call(
        _prep_kernel,
        grid=(b, nt),
        in_specs=in_specs,
        out_specs=[out_spec] * 11,
        out_shape=[out_sds] * 11,
        compiler_params=_cparams(("parallel", "parallel")),
        name="rwkv_prep",
    )(*([proj3] * 12), vec, lvec, w2f, a2f, g2f, hsum, hexp)


def _stack2(x, m0):
    return jnp.concatenate([jnp.where(m0, x, 0.0), jnp.where(m0, 0.0, x)], axis=0)


class _Chain(NamedTuple):
    r: jax.Array
    v: jax.Array
    kk: jax.Array
    lw: jax.Array
    kd: jax.Array
    bd: jax.Array
    s_ref: object
    rev: bool


def _scan_masks(c, reverse):
    ti = lax.broadcasted_iota(I32, (c, c), 0)
    si = lax.broadcasted_iota(I32, (c, c), 1)
    tri = jnp.where((si >= ti) if reverse else (si <= ti), 1.0, 0.0).astype(BF16)
    t2 = lax.broadcasted_iota(I32, (c, 2 * c), 0)
    s2 = lax.broadcasted_iota(I32, (c, 2 * c), 1)
    s2 = jnp.where(s2 >= c, s2 - c, s2)
    strict = (s2 > t2) if reverse else (s2 < t2)
    incl = (s2 >= t2) if reverse else (s2 <= t2)
    eye2 = jnp.where(s2 == t2, 1.0, 0.0)
    return tri, strict, incl, eye2


def _scan_chunk(chains):
    c = SCAN_CHUNK
    masks = {rev: _scan_masks(c, rev) for rev in (False, True)}
    m0 = lax.broadcasted_iota(I32, (c, LANES), 1) < A_HEAD_DIM
    half = lax.broadcasted_iota(I32, (c, 2 * c), 1) < c
    vi = lax.broadcasted_iota(I32, (LANES, LANES), 0)
    ki = lax.broadcasted_iota(I32, (LANES, LANES), 1)
    same_head = (vi < A_HEAD_DIM) == (ki < A_HEAD_DIM)
    bf = lambda x: x.astype(BF16)
    mm = lambda a, b: jnp.dot(a, b, preferred_element_type=F32)
    mm_nt = lambda a, b: lax.dot_general(a, b, (((1,), (1,)), ((), ())), preferred_element_type=F32)
    cat = jnp.concatenate

    cum = []
    for ch in chains:
        tri = masks[ch.rev][0]
        parts = mm(tri, cat(_split3(ch.lw), axis=1))
        cum.append(parts[:, :LANES] + parts[:, LANES:2 * LANES] + parts[:, 2 * LANES:])

    ar, bk2, bkw, v2, etot = [], [], [], [], []
    for ch, cm in zip(chains, cum):
        tot = cm[0:1, :] if ch.rev else cm[c - 1:c, :]
        einv = jnp.exp(-cm)
        etail = jnp.exp(tot - cm)
        at = -ch.kk * jnp.exp(cm - ch.lw)
        rt = ch.r * jnp.exp(cm)
        ar.append(bf(cat([at, rt], axis=0)))
        bk2.append(bf(cat([_stack2(ch.bd * einv, m0), _stack2(ch.kd * einv, m0)], axis=0)))
        bkw.append(bf(cat([ch.bd * etail, ch.kd * etail], axis=0)))
        v2.append(bf(_stack2(ch.v, m0)))
        etot.append(jnp.exp(tot))

    s0 = [ch.s_ref[...] for ch in chains]
    both = [mm_nt(a, cat([b, bf(s)], axis=0)) for a, b, s in zip(ar, bk2, s0)]
    mt = [r[:, :4 * c] for r in both]
    ars = [r[:, 4 * c:] for r in both]
    a_ab, a_xk, a_rb = [], [], []
    for ch, m in zip(chains, mt):
        _, strict, incl, _ = masks[ch.rev]
        a_ab.append(jnp.where(strict, m[:c, :2 * c], 0.0))
        a_xk.append(bf(cat([jnp.where(strict, m[:c, 2 * c:], 0.0), jnp.where(incl, m[c:, 2 * c:], 0.0)], axis=0)))
        a_rb.append(bf(jnp.where(incl, m[c:, :2 * c], 0.0)))
    xkv = [mm(a, v) for a, v in zip(a_xk, v2)]

    tm = [masks[ch.rev][3] + a for ch, a in zip(chains, a_ab)]
    pw = [mm(bf(a), _stack2(bf(a), half)) for a in a_ab]
    steps = int(np.log2(c)) - 1
    for k in range(steps):
        blk = [_stack2(bf(x), half) for x in pw]
        if k < steps - 1:
            both = [mm(bf(cat([t, x], axis=0)), d) for t, x, d in zip(tm, pw, blk)]
            tm = [t + r[:c] for t, r in zip(tm, both)]
            pw = [r[c:] for r in both]
        else:
            tm = [t + mm(bf(t), d) for t, d in zip(tm, blk)]

    u = [mm(bf(t), bf(_stack2(a[:c] + k[:c], m0))) for t, a, k in zip(tm, ars, xkv)]
    y = [a[c:] + k[c:] + mm(ab, bf(_stack2(uu, m0))) for a, k, ab, uu in zip(ars, xkv, a_rb, u)]
    upd = [mm(bf(cat([uu, ch.v], axis=0).T), w) for uu, ch, w in zip(u, chains, bkw)]
    for ch, s, e, up in zip(chains, s0, etot, upd):
        ch.s_ref[...] = s * e + jnp.where(same_head, up, 0.0)
    return y


SCAN_HP = 8


def _scan_kernel(rf, vf, kkf, lwf, kf, bf, rb, vb, kkb, lwb, kb, bb, yf_o, yb_o, sf_ref, sb_ref):
    @pl.when(pl.program_id(2) == 0)
    def _():
        sf_ref[...] = jnp.zeros_like(sf_ref)
        sb_ref[...] = jnp.zeros_like(sb_ref)

    chains = []
    for hp in range(SCAN_HP):
        ln = slice(hp * LANES, (hp + 1) * LANES)
        chains.append(_Chain(rf[0, :, ln], vf[0, :, ln], kkf[0, :, ln], lwf[0, :, ln], kf[0, :, ln], bf[0, :, ln],
                             sf_ref.at[hp], False))
        chains.append(_Chain(rb[0, :, ln], vb[0, :, ln], kkb[0, :, ln], lwb[0, :, ln], kb[0, :, ln], bb[0, :, ln],
                             sb_ref.at[hp], True))
    y = _scan_chunk(chains)
    for hp in range(SCAN_HP):
        ln = slice(hp * LANES, (hp + 1) * LANES)
        yf_o[0, :, ln] = y[2 * hp]
        yb_o[0, :, ln] = y[2 * hp + 1]


def _rwkv_scan(r, v, kk, lwf, kf, bf, lwb, kb, bb):
    b, t, _ = r.shape
    c = SCAN_CHUNK
    nc = t // c
    w = SCAN_HP * LANES
    fwd = pl.BlockSpec((1, c, w), lambda bi, hp, ci: (bi, ci, hp))
    bwd = pl.BlockSpec((1, c, w), lambda bi, hp, ci: (bi, nc - 1 - ci, hp))
    out_sds = jax.ShapeDtypeStruct((b, t, A_WIDTH), F32)
    return pl.pallas_call(
        _scan_kernel,
        grid=(b, A_WIDTH // w, nc),
        in_specs=[fwd] * 6 + [bwd] * 6,
        out_specs=[fwd, bwd],
        out_shape=[out_sds, out_sds],
        scratch_shapes=[pltpu.VMEM((SCAN_HP, LANES, LANES), F32), pltpu.VMEM((SCAN_HP, LANES, LANES), F32)],
        compiler_params=_cparams(("parallel", "parallel", "arbitrary")),
        name="rwkv_scan",
    )(r, v, kk, lwf, kf, bf, r, v, kk, lwb, kb, bb)


POST_TT = 256


def _post_kernel(yf, yb, bonus, g, vec_ref, hs_ref, he_ref, o_ref):
    y = yf[...] + yb[...]
    hsum, hexp = hs_ref[...], he_ref[...]
    inv = 1.0 / A_HEAD_DIM
    mu = _head_sums(y, hsum, hexp) * inv
    d = y - mu
    var = _head_sums(d * d, hsum, hexp) * inv
    yn = d * lax.rsqrt(var + GN_EPS) * vec_ref[0:1, :] + vec_ref[1:2, :]
    o_ref[...] = ((yn + bonus[...]) * g[...]).astype(o_ref.dtype)


def _rwkv_post(yf, yb, bonus, g, gnvec, hsum, hexp):
    n = yf.shape[0]
    tt = min(POST_TT, n)
    blk = pl.BlockSpec((tt, A_WIDTH), lambda i: (i, 0))
    full = lambda a: pl.BlockSpec(a.shape, lambda i: (0,) * a.ndim, pipeline_mode=pl.Buffered(1))
    return pl.pallas_call(
        _post_kernel,
        grid=(n // tt,),
        in_specs=[blk, blk, blk, blk, full(gnvec), full(hsum), full(hexp)],
        out_specs=blk,
        out_shape=jax.ShapeDtypeStruct((n, A_WIDTH), BF16),
        compiler_params=_cparams(("parallel",)),
        name="rwkv_post",
    )(yf, yb, bonus, g, gnvec, hsum, hexp)


def _gelu_tanh(x):
    return 0.5 * x * (1.0 + jnp.tanh(np.sqrt(2.0 / np.pi).astype(np.float32) * (x + 0.044715 * (x * x * x))))


def _sg_kernel(pb_ref, lnv_ref, w_ref, bias_ref, o_ref):
    z = _gelu_tanh(pb_ref[0])
    u = z[:, :B_WIDTH]
    v = _layer_norm_rows(z[:, B_WIDTH:], lnv_ref[0:1, :], lnv_ref[1:2, :])
    lane = lax.broadcasted_iota(I32, (B_CHUNK, LANES), 1)
    m0 = lane < B_GROUP_DIM
    parts = []
    for q in range(B_WIDTH // LANES):
        vq = v[:, q * LANES:(q + 1) * LANES]
        parts.append(_dot(w_ref[q], _stack2(vq, m0)))
    mixed = jnp.concatenate(parts, axis=1) + bias_ref[...]
    o_ref[0] = (u * mixed).astype(o_ref.dtype)


def _spatial_gating(proj3, lnvec, w2, bias_full):
    b, t, _ = proj3.shape
    full = lambda a: pl.BlockSpec(a.shape, lambda bi, i: (0,) * a.ndim)
    return pl.pallas_call(
        _sg_kernel,
        grid=(b, t // B_CHUNK),
        in_specs=[pl.BlockSpec((1, B_CHUNK, B_COLS), lambda bi, i: (bi, i, COL_B // B_COLS)),
                  full(lnvec), full(w2), full(bias_full)],
        out_specs=pl.BlockSpec((1, B_CHUNK, B_WIDTH), lambda bi, i: (bi, i, 0)),
        out_shape=jax.ShapeDtypeStruct((b, t, B_WIDTH), BF16),
        compiler_params=_cparams(("parallel", "parallel")),
        name="spatial_gating",
    )(proj3, lnvec, w2, bias_full)


NA_WIN = C_WIN_ROWS * GRID_W


def _na_row_start(i, rows):
    return jnp.clip(i - C_WIN_ROWS // 2, 0, rows - C_WIN_ROWS)


NA_QR = 2


def _na_kernel(q_ref, k_ref, v_ref, *rest, rows):
    bias_refs, o_ref = rest[:NA_QR], rest[NA_QR]
    kh_ref, vh_ref = rest[NA_QR + 1:]
    i = pl.program_id(1)
    scale = C_HEAD_DIM ** -0.5

    @pl.when(i == 0)
    def _():
        for h in range(C_HEADS):
            sl = slice(h * C_HEAD_DIM, (h + 1) * C_HEAD_DIM)
            kh_ref[h] = k_ref[0, :, sl]
            vh_ref[h] = v_ref[0, :, sl]

    qs, ks, vs, bs = [], [], [], []
    for j in range(NA_QR):
        start = pl.multiple_of(_na_row_start(i * NA_QR + j, rows) * GRID_W, GRID_W)
        q = q_ref[0, j * GRID_W:(j + 1) * GRID_W, :]
        for h in range(C_HEADS):
            sl = slice(h * C_HEAD_DIM, (h + 1) * C_HEAD_DIM)
            qs.append(q[:, sl])
            ks.append(kh_ref[h, pl.ds(start, NA_WIN), :])
            vs.append(vh_ref[h, pl.ds(start, NA_WIN), :])
            bs.append(bias_refs[j][0, h])
    nt = lambda a, b: lax.dot_general(a, b, (((1,), (1,)), ((), ())), preferred_element_type=F32)
    ss = [nt(q, k) * scale + bias for q, k, bias in zip(qs, ks, bs)]
    ps = []
    for s in ss:
        e = jnp.exp(s - jnp.max(s, axis=-1, keepdims=True))
        ps.append((e / jnp.sum(e, axis=-1, keepdims=True)).astype(BF16))
    os_ = [jnp.dot(p, v, preferred_element_type=F32) for p, v in zip(ps, vs)]
    for j in range(NA_QR):
        o_ref[0, j * GRID_W:(j + 1) * GRID_W, :] = jnp.concatenate(
            os_[j * C_HEADS:(j + 1) * C_HEADS], axis=1).astype(o_ref.dtype)


def _neighbourhood_attention(proj3, bias_tab):
    b, t, _ = proj3.shape
    rows = t // GRID_W
    cq = COL_C // C_WIDTH
    qr = NA_QR
    seq = lambda cb: pl.BlockSpec((1, t, C_WIDTH), lambda bi, i: (bi, 0, cb))
    bias = lambda j: pl.BlockSpec((1, C_HEADS, GRID_W, NA_WIN),
                                  lambda bi, i: (i * qr + j - _na_row_start(i * qr + j, rows), 0, 0, 0))
    return pl.pallas_call(
        functools.partial(_na_kernel, rows=rows),
        grid=(b, rows // qr),
        in_specs=[pl.BlockSpec((1, qr * GRID_W, C_WIDTH), lambda bi, i: (bi, i, cq)), seq(cq + 1), seq(cq + 2)]
                 + [bias(j) for j in range(qr)],
        out_specs=pl.BlockSpec((1, qr * GRID_W, C_WIDTH), lambda bi, i: (bi, i, 0)),
        out_shape=jax.ShapeDtypeStruct((b, t, C_WIDTH), BF16),
        scratch_shapes=[pltpu.VMEM((C_HEADS, t, C_HEAD_DIM), BF16), pltpu.VMEM((C_HEADS, t, C_HEAD_DIM), BF16)],
        compiler_params=_cparams(("parallel", "arbitrary")),
        name="nbr_attention",
    )(proj3, proj3, proj3, *([bias_tab] * qr))


def _na_bias_table(rpb, rows):
    kc = C_WIN_COLS
    cols = np.arange(GRID_W)
    col_start = np.clip(cols - kc // 2, 0, GRID_W - kc)
    key_col = np.arange(GRID_W)
    in_win = (key_col[None, :] >= col_start[:, None]) & (key_col[None, :] < col_start[:, None] + kc)
    delta = np.arange(C_WIN_ROWS)
    row_off = np.arange(C_WIN_ROWS)[None, :] - delta[:, None] + (C_WIN_ROWS - 1)
    by_row = rpb[:, row_off]
    span = 2 * GRID_W
    lead = GRID_W - C_WIN_COLS
    padded = jnp.pad(by_row, [(0, 0)] * 3 + [(lead, span - lead - (2 * C_WIN_COLS - 1))])
    skew = jnp.tile(padded, GRID_W)[..., :GRID_W * (span - 1)].reshape(by_row.shape[:3] + (GRID_W, span - 1))
    bias = skew[..., GRID_W - 1:]
    bias = jnp.where(in_win, bias, -1e30)
    bias = jnp.transpose(bias, (1, 0, 3, 2, 4)).reshape(C_WIN_ROWS, C_HEADS, GRID_W, NA_WIN)
    return bias.astype(F32)


MERGE_TM = 512


def _merge_kernel(ya, yb, yc, ga, gb, gc, pa, pb, pc, o_ref):
    m = _sigmoid(ga[...].astype(F32)) * jnp.dot(ya[...], pa[...], preferred_element_type=F32)
    m = m + _sigmoid(gb[...].astype(F32)) * jnp.dot(yb[...], pb[...], preferred_element_type=F32)
    m = m + _sigmoid(gc[...].astype(F32)) * jnp.dot(yc[...], pc[...], preferred_element_type=F32)
    o_ref[...] = m.astype(o_ref.dtype)


def _merge(ya, yb, yc, proj, p_a, p_b, p_c):
    n = ya.shape[0]
    tm = min(MERGE_TM, n)
    rowblk = lambda w: pl.BlockSpec((tm, w), lambda i: (i, 0))
    gate = lambda j: pl.BlockSpec((tm, D_MODEL), lambda i: (i, COL_G // D_MODEL + j))
    full = lambda a: pl.BlockSpec(a.shape, lambda i: (0,) * a.ndim, pipeline_mode=pl.Buffered(1))
    return pl.pallas_call(
        _merge_kernel,
        grid=(n // tm,),
        in_specs=[rowblk(A_WIDTH), rowblk(B_WIDTH), rowblk(C_WIDTH), gate(0), gate(1), gate(2),
                  full(p_a), full(p_b), full(p_c)],
        out_specs=rowblk(D_MODEL),
        out_shape=jax.ShapeDtypeStruct((n, D_MODEL), BF16),
        compiler_params=_cparams(("parallel",)),
        name="branch_merge",
    )(ya, yb, yc, proj, proj, proj, p_a, p_b, p_c)


OUT_TM = 512


def _outln_kernel(m_ref, w_ref, x_ref, lnv_ref, o_ref, ob_ref):
    h = jnp.dot(m_ref[...], w_ref[...], preferred_element_type=F32)
    y = _layer_norm_rows(DEEPNORM_ALPHA * x_ref[...] + h, lnv_ref[0:1, :], lnv_ref[1:2, :])
    o_ref[...] = y
    ob_ref[...] = y.astype(BF16)


def _out_proj_ln(m, w_out, x, lnvec):
    n = m.shape[0]
    tm = min(OUT_TM, n)
    blk = pl.BlockSpec((tm, D_MODEL), lambda i: (i, 0))
    full = lambda a: pl.BlockSpec(a.shape, lambda i: (0,) * a.ndim, pipeline_mode=pl.Buffered(1))
    return pl.pallas_call(
        _outln_kernel,
        grid=(n // tm,),
        in_specs=[blk, full(w_out), blk, full(lnvec)],
        out_specs=[blk, blk],
        out_shape=[jax.ShapeDtypeStruct((n, D_MODEL), F32), jax.ShapeDtypeStruct((n, D_MODEL), BF16)],
        compiler_params=_cparams(("parallel",)),
        name="out_proj_ln",
    )(m, w_out, x, lnvec)


ROUTER_TM = 512


def _router_kernel(x_ref, wt_ref, o_ref):
    xh, xl = _split2(x_ref[...])
    wh, wl = _split2(wt_ref[...])
    nt = lambda a, b: lax.dot_general(a, b, (((1,), (1,)), ((), ())), preferred_element_type=F32)
    logits = nt(wh, xh) + nt(wl, xh) + nt(wh, xl)
    logits = logits - jnp.max(logits, axis=0, keepdims=True)
    e = jnp.exp(logits)
    o_ref[...] = e / jnp.sum(e, axis=0, keepdims=True)


def _router(x, w_router_t):
    n = x.shape[0]
    tm = min(ROUTER_TM, n)
    return pl.pallas_call(
        _router_kernel,
        grid=(n // tm,),
        in_specs=[pl.BlockSpec((tm, D_MODEL), lambda i: (i, 0)),
                  pl.BlockSpec(w_router_t.shape, lambda i: (0, 0))],
        out_specs=pl.BlockSpec((N_EXPERTS, tm), lambda i: (0, i)),
        out_shape=jax.ShapeDtypeStruct((N_EXPERTS, n), F32),
        compiler_params=_cparams(("parallel",)),
        name="router",
    )(x, w_router_t)


def _select_kernel(aff_ref, pos_ref, off_ref, *, cap):
    aff = aff_ref[...]
    e_, g_, l_ = aff.shape
    n = g_ * l_
    bits = pltpu.bitcast(aff, I32)

    def count(mask):
        c = jnp.sum(jnp.where(mask, 1.0, 0.0), axis=2, keepdims=True)
        return jnp.sum(c, axis=1, keepdims=True)

    def thr_body(_, carry):
        lo, hi = carry
        mid = lo + (hi - lo + 1) // 2
        ok = count(bits >= mid) >= cap
        return jnp.where(ok, mid, lo), jnp.where(ok, hi, mid - 1)

    lo0 = jnp.zeros((e_, 1, 1), I32)
    hi0 = jnp.full((e_, 1, 1), 0x7F800000, I32)
    thr, _ = lax.fori_loop(0, 32, thr_body, (lo0, hi0))
    gt = bits > thr
    tie = bits == thr
    need = cap - count(gt)
    idx = lax.broadcasted_iota(I32, aff.shape, 1) * l_ + lax.broadcasted_iota(I32, aff.shape, 2)

    def idx_body(_, carry):
        lo, hi = carry
        mid = (lo + hi) // 2
        ok = count(tie & (idx <= mid)) >= need
        return jnp.where(ok, lo, mid + 1), jnp.where(ok, mid, hi)

    lo1 = jnp.zeros((e_, 1, 1), I32)
    hi1 = jnp.full((e_, 1, 1), n - 1, I32)
    cut, _ = lax.fori_loop(0, int(np.ceil(np.log2(n))) + 1, idx_body, (lo1, hi1))
    sel = jnp.where(gt | (tie & (idx <= cut)), 1.0, 0.0)

    sel2 = sel.reshape(e_ * g_, l_)
    ls = lax.broadcasted_iota(I32, (l_, l_), 0)
    lt = lax.broadcasted_iota(I32, (l_, l_), 1)
    incl = jnp.dot(sel2.astype(BF16), jnp.where(ls <= lt, 1.0, 0.0).astype(BF16), preferred_element_type=F32)
    tot = jnp.dot(sel2.astype(BF16), jnp.ones((l_, l_), BF16), preferred_element_type=F32)
    gs = lax.broadcasted_iota(I32, (g_, g_), 0)
    gt_ = lax.broadcasted_iota(I32, (g_, g_), 1)
    lower = jnp.where(gt_ < gs, 1.0, 0.0).astype(BF16)
    offs = [jnp.dot(lower, tot[e * g_:(e + 1) * g_].astype(BF16), preferred_element_type=F32)
            for e in range(e_)]
    off = jnp.concatenate(offs, axis=0)
    pos = incl + off - sel2
    pos_ref[...] = jnp.where(sel2 > 0.5, pos, -1.0).astype(I32).reshape(e_, g_, l_)
    off_ref[...] = off.astype(I32).reshape(e_, g_, l_)


def _select(aff3, cap):
    full = pl.BlockSpec(aff3.shape, lambda i: (0, 0, 0))
    sds = jax.ShapeDtypeStruct(aff3.shape, I32)
    return pl.pallas_call(
        functools.partial(_select_kernel, cap=cap),
        grid=(1,),
        in_specs=[full],
        out_specs=[full, full],
        out_shape=[sds, sds],
        compiler_params=_cparams(("arbitrary",)),
        name="expert_select",
    )(aff3)


def _gather_kernel(tile_s, valid_s, e_g, slab_g, first_g, *refs):
    k_ = GATHER_GROUP
    x_refs, pos_refs, aff_refs = refs[:k_], refs[k_:2 * k_], refs[2 * k_:3 * k_]
    o_ref, g_ref = refs[3 * k_:]
    g = pl.program_id(0)
    t = MOE_TILE

    @pl.when(first_g[g] == 1)
    def _():
        o_ref[...] = jnp.zeros_like(o_ref)
        g_ref[...] = jnp.zeros_like(g_ref)

    nvalid = valid_s[g * k_]
    for k in range(1, k_):
        nvalid = nvalid + valid_s[g * k_ + k]

    @pl.when(nvalid > 0)
    def _():
        rank = lax.broadcasted_iota(I32, (t, GATHER_TOK), 0) + slab_g[g] * t
        got = o_ref[0].astype(F32)
        gate = g_ref[0]
        per_dot = MXU_DIM // GATHER_TOK
        for k0 in range(0, k_, per_dot):
            onehots, xs = [], []
            for k in range(k0, k0 + per_dot):
                want = jnp.where(valid_s[g * k_ + k] == 1, rank, -2)
                hit = pos_refs[k][0] == want
                onehots.append(jnp.where(hit, 1.0, 0.0).astype(BF16))
                xs.append(x_refs[k][...])
                gate = gate + jnp.sum(jnp.where(hit, aff_refs[k][0], 0.0), axis=1, keepdims=True)
            got = got + jnp.dot(jnp.concatenate(onehots, axis=1), jnp.concatenate(xs, axis=0),
                                preferred_element_type=F32)
        o_ref[0] = got.astype(o_ref.dtype)
        g_ref[0] = gate


def _moe_gather(x_bf, pos_en, aff_en, items, cap):
    n = x_bf.shape[0]
    t = MOE_TILE
    k_ = GATHER_GROUP
    ng = items[2].shape[0]
    tk = GATHER_TOK
    xs = [pl.BlockSpec((tk, D_MODEL), lambda g, tl, va, e, s, fi, k=k: (tl[g * k_ + k], 0)) for k in range(k_)]
    rows = [pl.BlockSpec((1, 1, tk), lambda g, tl, va, e, s, fi, k=k: (e[g], 0, tl[g * k_ + k])) for k in range(k_)]
    grid_spec = pltpu.PrefetchScalarGridSpec(
        num_scalar_prefetch=5,
        grid=(ng,),
        in_specs=xs + rows + rows,
        out_specs=[pl.BlockSpec((1, t, D_MODEL), lambda g, tl, va, e, s, fi: (e[g], s[g], 0)),
                   pl.BlockSpec((1, t, 1), lambda g, tl, va, e, s, fi: (e[g], s[g], 0))],
    )
    pos3 = pos_en.reshape(N_EXPERTS, 1, n)
    aff3 = aff_en.reshape(N_EXPERTS, 1, n)
    return pl.pallas_call(
        _gather_kernel,
        grid_spec=grid_spec,
        out_shape=[jax.ShapeDtypeStruct((N_EXPERTS, cap, D_MODEL), BF16),
                   jax.ShapeDtypeStruct((N_EXPERTS, cap, 1), F32)],
        compiler_params=_cparams(("arbitrary",)),
        name="moe_gather",
    )(*items, *([x_bf] * k_), *([pos3] * k_), *([aff3] * k_))


FFN_TM = 512


def _ffn_kernel(x_ref, gate_ref, wg_ref, wu_ref, wd_ref, o_ref):
    x = x_ref[0]
    g = jnp.dot(x, wg_ref[0], preferred_element_type=F32)
    u = jnp.dot(x, wu_ref[0], preferred_element_type=F32)
    h = (g * _sigmoid(g)) * u
    y = jnp.dot(h.astype(BF16), wd_ref[0], preferred_element_type=F32) * gate_ref[0]
    o_ref[0] = y.astype(o_ref.dtype)


def _expert_ffn(xe, gate, wg, wu, wd, layer):
    e_, cap, _ = xe.shape
    tm = min(FFN_TM, cap)
    wspec = lambda a: pl.BlockSpec((None, 1) + a.shape[2:], lambda e, i: (layer, e, 0, 0))
    rows = lambda w: pl.BlockSpec((1, tm, w), lambda e, i: (e, i, 0))
    sds = jax.ShapeDtypeStruct((e_, cap, D_MODEL), BF16)
    return pl.pallas_call(
        _ffn_kernel,
        grid=(e_, cap // tm),
        in_specs=[rows(D_MODEL), rows(1), wspec(wg), wspec(wu), wspec(wd)],
        out_specs=rows(D_MODEL),
        out_shape=sds,
        compiler_params=_cparams(("parallel", "arbitrary")),
        name="expert_ffn",
    )(xe, gate, wg, wu, wd)


def _combine_kernel(e_s, slab_s, valid_s, tile_g, first_g, last_g, *refs):
    k_ = COMBINE_GROUP
    ye_refs = refs[:k_]
    pos_ref, x_ref, lnv_ref, o_ref, ob_ref, acc_ref = refs[k_:]
    g = pl.program_id(0)
    t = MOE_TILE

    @pl.when(first_g[g] == 1)
    def _():
        acc_ref[...] = jnp.zeros_like(acc_ref)

    nvalid = valid_s[g * k_]
    for k in range(1, k_):
        nvalid = nvalid + valid_s[g * k_ + k]

    @pl.when(nvalid > 0)
    def _():
        posf = pos_ref[...].astype(F32)
        lane = lax.broadcasted_iota(I32, (t, N_EXPERTS), 1)
        row = lax.broadcasted_iota(I32, (t, MOE_SUB), 1)
        onehots = []
        for k in range(k_):
            s = g * k_ + k
            pos = jnp.sum(jnp.where(lane == e_s[s], posf, 0.0), axis=1, keepdims=True)
            base = jnp.where(valid_s[s] == 1, slab_s[s] * MOE_SUB, -2 * MOE_SUB)
            onehots.append(jnp.where(pos == (row + base).astype(F32), 1.0, 0.0).astype(BF16))
        onehot = jnp.concatenate(onehots, axis=1)
        ye = jnp.concatenate([r[0] for r in ye_refs], axis=0)
        acc_ref[...] += jnp.dot(onehot, ye, preferred_element_type=F32)

    @pl.when(last_g[g] == 1)
    def _():
        y = _layer_norm_rows(DEEPNORM_ALPHA * x_ref[...] + acc_ref[...], lnv_ref[0:1, :], lnv_ref[1:2, :])
        o_ref[...] = y
        ob_ref[...] = y.astype(BF16)


def _moe_combine(ye, pos_ne, x, lnvec, items):
    n = x.shape[0]
    t = MOE_TILE
    k_ = COMBINE_GROUP
    ng = items[3].shape[0]
    tile_blk = lambda w: pl.BlockSpec((t, w), lambda g, e, s, va, tl, fi, la: (tl[g], 0))
    slabs = [pl.BlockSpec((1, MOE_SUB, D_MODEL), lambda g, e, s, va, tl, fi, la, k=k: (e[g * k_ + k], s[g * k_ + k], 0))
             for k in range(k_)]
    grid_spec = pltpu.PrefetchScalarGridSpec(
        num_scalar_prefetch=6,
        grid=(ng,),
        in_specs=slabs + [tile_blk(N_EXPERTS), tile_blk(D_MODEL),
                          pl.BlockSpec(lnvec.shape, lambda g, e, s, va, tl, fi, la: (0, 0))],
        out_specs=[tile_blk(D_MODEL), tile_blk(D_MODEL)],
        scratch_shapes=[pltpu.VMEM((t, D_MODEL), F32)],
    )
    return pl.pallas_call(
        _combine_kernel,
        grid_spec=grid_spec,
        out_shape=[jax.ShapeDtypeStruct((n, D_MODEL), F32), jax.ShapeDtypeStruct((n, D_MODEL), BF16)],
        compiler_params=_cparams(("arbitrary",)),
        name="moe_combine",
    )(*items, *([ye] * k_), pos_ne, x, lnvec)


def _moe_items(group_off, n, cap):
    e_ = N_EXPERTS
    big = jnp.int32(2 ** 30)

    def build(rows, by_tile, k_, t):
        ns = cap // rows
        nt = n // t
        starts = group_off[:, ::t // LANES, 0]
        ends = jnp.concatenate([starts[:, 1:], jnp.full((e_, 1), cap, I32)], axis=1)
        e_p = jnp.broadcast_to(jnp.arange(e_, dtype=I32)[:, None], (e_, nt))
        tile_p = jnp.broadcast_to(jnp.arange(nt, dtype=I32)[None, :], (e_, nt))
        slab_p = jnp.minimum(starts // rows, ns - 1)
        valid_p = (ends > starts).astype(I32)
        bound = jnp.arange(ns, dtype=I32) * rows
        tile_s = jnp.sum((starts[:, :, None] <= bound[None, None, :]).astype(I32), axis=1) - 1
        tile_s = jnp.clip(tile_s, 0, nt - 1)
        at_tile = tile_s[:, :, None] == jnp.arange(nt, dtype=I32)[None, None, :]
        st_s = jnp.sum(jnp.where(at_tile, starts[:, None, :], 0), axis=2)
        en_s = jnp.sum(jnp.where(at_tile, ends[:, None, :], 0), axis=2)
        valid_s = ((st_s < bound[None, :]) & (bound[None, :] < en_s)).astype(I32)
        e_s = jnp.broadcast_to(jnp.arange(e_, dtype=I32)[:, None], (e_, ns))
        slab_s = jnp.broadcast_to(jnp.arange(ns, dtype=I32)[None, :], (e_, ns))
        cat = lambda a, b: jnp.concatenate([a.reshape(-1), b.reshape(-1)])
        e_a, slab_a, tile_a = cat(e_p, e_s), cat(slab_p, slab_s), cat(tile_p, tile_s)
        valid_a = cat(valid_p, valid_s)
        secondary = cat(jnp.zeros_like(e_p), jnp.ones_like(e_s))
        ni = e_a.shape[0]
        ar = jnp.arange(ni, dtype=I32)
        if by_tile:
            keep = jnp.maximum(valid_a, 1 - secondary)
            key = ((tile_a * e_ + e_a) * 2 + secondary) * ns + slab_a
        else:
            keep = valid_a
            key = (e_a * ns + slab_a) * nt + tile_a
        key = jnp.where(keep == 1, key, big + ar)
        rank = jnp.sum((key[None, :] < key[:, None]).astype(I32), axis=1)
        code = ((e_a * ns + slab_a) * nt + tile_a) * 2 + valid_a
        decode = lambda cd: (cd // (2 * nt * ns), (cd // (2 * nt)) % ns, (cd // 2) % nt, cd % 2)
        ocode = jnp.sum(jnp.where(rank[None, :] == ar[:, None], code[None, :], 0), axis=1)
        nkeep = jnp.sum(keep)
        kept = ar < nkeep
        oe, os_, otl, _ = decode(ocode)
        blk = otl if by_tile else oe * ns + os_

        nblk = nt if by_tile else e_ * ns
        ng = -(-ni // k_) + nblk
        first = jnp.concatenate([jnp.ones((1,), bool), blk[1:] != blk[:-1]])
        run_start = lax.cummax(jnp.where(first, ar, 0))
        sub = (ar - run_start) % k_
        leads = (sub == 0) & kept
        gid = jnp.cumsum(leads.astype(I32)) - 1
        slot = jnp.where(kept, gid * k_ + sub, -1)
        sar = jnp.arange(ng * k_, dtype=I32)
        hit = slot[None, :] == sar[:, None]
        has = jnp.sum(hit.astype(I32), axis=1)
        scode = jnp.sum(jnp.where(hit, ocode[None, :], 0), axis=1)
        n_real = jnp.sum(leads.astype(I32))
        gar = jnp.arange(ng, dtype=I32)
        lead = scode.reshape(ng, k_)[:, 0]
        lead = jnp.where(gar < n_real, lead, jnp.sum(jnp.where(gar == n_real - 1, lead, 0)))
        scode = jnp.where(has == 1, scode, jnp.repeat(lead, k_))
        se, ss, stl, sv = decode(scode)
        ge, gsl, gtl, _ = decode(lead)
        gblk = gtl if by_tile else ge * ns + gsl
        change = (gblk[1:] != gblk[:-1]).astype(I32)
        one = jnp.ones((1,), I32)
        slots = dict(e=se, slab=ss, tile=stl, valid=sv * has)
        groups = dict(e=ge, slab=gsl, tile=gtl, first=jnp.concatenate([one, change]),
                      last=jnp.concatenate([change, one]))
        return slots, groups

    gs, gg = build(MOE_TILE, False, GATHER_GROUP, GATHER_TOK)
    cs, cg = build(MOE_SUB, True, COMBINE_GROUP, MOE_TILE)
    return ((gs["tile"], gs["valid"], gg["e"], gg["slab"], gg["first"]),
            (cs["e"], cs["slab"], cs["valid"], cg["tile"], cg["first"], cg["last"]))


def _expert_choice_moe_ln(x, x_bf, w_router_t, wg, wu, wd, layer, lnvec):
    n = x.shape[0]
    cap = EC_CAPACITY_FACTOR * n // N_EXPERTS
    aff_en = _router(x, w_router_t)
    pos3, off3 = _select(aff_en.reshape(N_EXPERTS, n // LANES, LANES), cap)
    pos_en = pos3.reshape(N_EXPERTS, n)
    g_items, c_items = _moe_items(off3, n, cap)
    xe, gate = _moe_gather(x_bf, pos_en, aff_en, g_items, cap)
    ye = _expert_ffn(xe, gate, wg, wu, wd, layer)
    return _moe_combine(ye, pos_en.T, x, lnvec, c_items)


def _pack_cols(w):
    a_end = A_COLS
    b_end = a_end + B_COLS
    c_end = b_end + C_COLS
    pad = jnp.zeros(w.shape[:-1] + (LORA_PAD - LORA_COLS,), w.dtype)
    return (jnp.concatenate([w[..., :3 * A_WIDTH], w[..., a_end:b_end], w[..., 3 * A_WIDTH:a_end], pad], axis=-1),
            jnp.concatenate([w[..., c_end:], w[..., b_end:c_end]], axis=-1))


def _lora_rows(w, start):
    k = w.shape[-2]
    return jnp.pad(w, [(0, 0)] * (w.ndim - 2) + [(start, LORA_PAD - start - k), (0, 0)])


def _prepare(w_in, mu_prev, mu_next, decay_w0, decay_w2, iclr_a0, iclr_a2, gate_g2, k_k, k_a, r_k, gn_g, gn_b,
             sg_ln_g, sg_ln_b, sg_w, sg_b, rpb, p_a, p_b, p_c, w_out, ln_mix_g, ln_mix_b, w_router, e_gate, e_up,
             e_down, ln_ffn_g, ln_ffn_b, rows):
    l_ = w_in.shape[0]
    pad_a = lambda m: jnp.pad(m, ((0, 0), (0, LORA_PAD - LORA_COLS)))
    mup, mun = mu_prev, mu_next
    vec_rows = [mup[:, :A_WIDTH], mup[:, A_WIDTH:2 * A_WIDTH], mup[:, 2 * A_WIDTH:3 * A_WIDTH],
                mun[:, :A_WIDTH], mun[:, A_WIDTH:2 * A_WIDTH], mun[:, 2 * A_WIDTH:3 * A_WIDTH],
                k_k, k_a, r_k.reshape(l_, A_WIDTH), decay_w0[:, 0], decay_w0[:, 1], iclr_a0[:, 0], iclr_a0[:, 1]]
    vec = jnp.stack(vec_rows + [jnp.zeros_like(k_k)] * (16 - len(vec_rows)), axis=1)
    lvec = jnp.stack([pad_a(mup[:, 3 * A_WIDTH:]), pad_a(mun[:, 3 * A_WIDTH:])]
                     + [jnp.zeros((l_, LORA_PAD), F32)] * 6, axis=1)
    w2f = jnp.stack([_lora_rows(decay_w2[:, 0], 0), _lora_rows(decay_w2[:, 1], A_DECAY_LORA)], axis=1)
    a2f = jnp.stack([_lora_rows(iclr_a2[:, 0], 2 * A_DECAY_LORA),
                     _lora_rows(iclr_a2[:, 1], 2 * A_DECAY_LORA + A_ICLR_LORA)], axis=1)
    g2f = _lora_rows(gate_g2, 2 * A_DECAY_LORA + 2 * A_ICLR_LORA)
    head = np.arange(A_WIDTH) // A_HEAD_DIM
    hsum = jnp.asarray(head[:, None] == np.arange(LANES)[None, :], BF16)
    sgw2 = sg_w.reshape(l_, B_GROUPS // 2, 2, B_CHUNK, B_CHUNK).transpose(0, 1, 3, 2, 4)
    sgw2 = sgw2.reshape(l_, B_GROUPS // 2, B_CHUNK, 2 * B_CHUNK)
    sg_bias = jnp.repeat(jnp.swapaxes(sg_b, 1, 2), B_GROUP_DIM, axis=2)
    return dict(
        w_in=_pack_cols(w_in.astype(BF16)), vec=vec, lvec=lvec, w2f=w2f.astype(BF16), a2f=a2f.astype(BF16),
        g2f=g2f.astype(BF16), hsum=hsum, hexp=hsum.T, gnvec=jnp.stack([gn_g, gn_b], axis=1),
        sg_ln=jnp.stack([sg_ln_g, sg_ln_b], axis=1), sgw2=sgw2.astype(BF16), sg_bias=sg_bias,
        na_bias=jnp.stack([_na_bias_table(rpb[l], rows) for l in range(l_)]),
        p_a=p_a.astype(BF16), p_b=p_b.astype(BF16), p_c=p_c.astype(BF16), w_out=w_out.astype(BF16),
        ln_mix=jnp.stack([ln_mix_g, ln_mix_b], axis=1), w_router_t=jnp.swapaxes(w_router, 1, 2),
        e_gate=e_gate.astype(BF16), e_up=e_up.astype(BF16), e_down=e_down.astype(BF16),
        ln_ffn=jnp.stack([ln_ffn_g, ln_ffn_b], axis=1))


def _mixer(x, x_bf, p, l, b, t):
    n = b * t
    w_f, w_h = p["w_in"]
    proj = _matmul(x_bf, w_f, l, 1024, IN_TN)
    proj_h = _matmul(x_bf, w_h, l, 1024, IN_TN, BF16)
    proj3 = proj.reshape(b, t, F_COLS)
    proj3_h = proj_h.reshape(b, t, H_COLS)
    r, v, kk, lwf, kf, bf, lwb, kb, bb, bonus, g = _rwkv_prep(
        proj3, p["vec"][l], p["lvec"][l], p["w2f"][l], p["a2f"][l], p["g2f"][l], p["hsum"], p["hexp"])
    yf, yb = _rwkv_scan(r, v, kk, lwf, kf, bf, lwb, kb, bb)
    flat = lambda a: a.reshape(n, a.shape[-1])
    ya = _rwkv_post(flat(yf), flat(yb), flat(bonus), flat(g), p["gnvec"][l], p["hsum"], p["hexp"])
    ybr = _spatial_gating(proj3, p["sg_ln"][l], p["sgw2"][l], p["sg_bias"][l])
    ycr = _neighbourhood_attention(proj3_h, p["na_bias"][l])
    m = _merge(ya, flat(ybr), flat(ycr), proj_h, p["p_a"][l], p["p_b"][l], p["p_c"][l])
    return _out_proj_ln(m, p["w_out"][l], x, p["ln_mix"][l])


def _trunk(x3, p):
    b, t, _ = x3.shape
    x = x3.reshape(b * t, D_MODEL)
    x_bf = x.astype(BF16)
    for l in range(DEPTH):
        x, x_bf = _mixer(x, x_bf, p, l, b, t)
        x, x_bf = _expert_choice_moe_ln(x, x_bf, p["w_router_t"][l], p["e_gate"], p["e_up"], p["e_down"], l,
                                        p["ln_ffn"][l])
    return x.reshape(b, t, D_MODEL)


def kernel(x_prompt, x_sample, w_in, mu_prev, mu_next, decay_w0, decay_w2, iclr_a0, iclr_a2, gate_g2, k_k, k_a, r_k, gn_g, gn_b, sg_ln_g, sg_ln_b, sg_w, sg_b, rpb, p_a, p_b, p_c, w_out, ln_mix_g, ln_mix_b, w_router, e_gate, e_up, e_down, ln_ffn_g, ln_ffn_b):
    assert x_prompt.shape[1] == x_sample.shape[1]
    rows = x_prompt.shape[1] // GRID_W
    p = _prepare(w_in, mu_prev, mu_next, decay_w0, decay_w2, iclr_a0, iclr_a2, gate_g2, k_k, k_a, r_k, gn_g, gn_b,
                 sg_ln_g, sg_ln_b, sg_w, sg_b, rpb, p_a, p_b, p_c, w_out, ln_mix_g, ln_mix_b, w_router, e_gate,
                 e_up, e_down, ln_ffn_g, ln_ffn_b, rows)
    return (_trunk(x_prompt, p), _trunk(x_sample, p))
```

```python
import functools
from typing import NamedTuple

import numpy as np
import jax
import jax.numpy as jnp
from jax import lax
from jax.experimental import pallas as pl
from jax.experimental.pallas import tpu as pltpu

F32 = jnp.float32
BF16 = jnp.bfloat16
I32 = jnp.int32

D_MODEL = 2048
DEPTH = 4
GRID_W = 64
A_HEADS = 16
A_HEAD_DIM = 64
A_WIDTH = A_HEADS * A_HEAD_DIM
A_DECAY_LORA = 64
A_ICLR_LORA = 64
A_GATE_LORA = 160
B_GROUPS = 8
B_GROUP_DIM = 64
B_WIDTH = B_GROUPS * B_GROUP_DIM
B_CHUNK = 128
C_HEADS = 8
C_HEAD_DIM = 64
C_WIDTH = C_HEADS * C_HEAD_DIM
C_WIN_ROWS = 8
C_WIN_COLS = 16
N_EXPERTS = 16
EXPERT_HIDDEN = 1024
EC_CAPACITY_FACTOR = 2
A_COLS = 3 * A_WIDTH + 2 * A_DECAY_LORA + 2 * A_ICLR_LORA + A_GATE_LORA
B_COLS = 2 * B_WIDTH
C_COLS = 3 * C_WIDTH
G_COLS = 3 * D_MODEL
DEEPNORM_ALPHA = (2 * DEPTH) ** 0.25
LN_EPS = 1e-5
GN_EPS = 64e-5

LORA_COLS = 2 * A_DECAY_LORA + 2 * A_ICLR_LORA + A_GATE_LORA
LORA_PAD = 512
COL_R, COL_K, COL_V = 0, A_WIDTH, 2 * A_WIDTH
COL_B = 3 * A_WIDTH
COL_L = COL_B + B_COLS
F_COLS = COL_L + LORA_PAD
COL_G = 0
COL_C = G_COLS
H_COLS = COL_C + C_COLS
IN_TN = 1536

LANES = 128
MXU_DIM = 256
SCAN_CHUNK = 64
MOE_TILE = 256
MOE_SUB = 64
GATHER_TOK = 128
GATHER_GROUP = 8
COMBINE_GROUP = 8
VMEM_LIMIT = 48 * 1024 * 1024


def _cparams(sem):
    return pltpu.CompilerParams(dimension_semantics=sem, vmem_limit_bytes=VMEM_LIMIT)


def _dot(a, b):
    return jnp.dot(a.astype(BF16), b.astype(BF16), preferred_element_type=F32)


def _dot_nt(a, b):
    return lax.dot_general(a.astype(BF16), b.astype(BF16), (((1,), (1,)), ((), ())),
                           preferred_element_type=F32)


def _split2(x):
    hi = x.astype(BF16)
    lo = (x - hi.astype(F32)).astype(BF16)
    return hi, lo


def _split3(x):
    hi = x.astype(BF16)
    r1 = x - hi.astype(F32)
    mid = r1.astype(BF16)
    lo = (r1 - mid.astype(F32)).astype(BF16)
    return hi, mid, lo


def _dot_x01(x, m01):
    hi, lo = _split2(x)
    m = m01.astype(BF16)
    return jnp.dot(hi, m, preferred_element_type=F32) + jnp.dot(lo, m, preferred_element_type=F32)


def _head_sums(x, hsum, hexp):
    return _dot_x01(_dot_x01(x, hsum), hexp)


def _sigmoid(x):
    return 1.0 / (1.0 + jnp.exp(-x))


def _layer_norm_rows(z, g, b):
    mu = jnp.mean(z, axis=-1, keepdims=True)
    d = z - mu
    var = jnp.mean(d * d, axis=-1, keepdims=True)
    return d * lax.rsqrt(var + LN_EPS) * g + b


def _mm_kernel(a_ref, w_ref, o_ref):
    o_ref[...] = jnp.dot(a_ref[...], w_ref[...], preferred_element_type=F32).astype(o_ref.dtype)


def _matmul(a, w, layer, tm, tn, out_dtype=F32):
    m, k = a.shape
    n = w.shape[2]
    tm = min(tm, m)
    return pl.pallas_call(
        _mm_kernel,
        grid=(n // tn, m // tm),
        in_specs=[pl.BlockSpec((tm, k), lambda j, i: (i, 0)),
                  pl.BlockSpec((None, k, tn), lambda j, i: (layer, 0, j))],
        out_specs=pl.BlockSpec((tm, tn), lambda j, i: (i, j)),
        out_shape=jax.ShapeDtypeStruct((m, n), out_dtype),
        compiler_params=_cparams(("parallel", "parallel")),
        name="in_proj_" + jnp.dtype(out_dtype).name,
    )(a, w)


PREP_TT = 128
V_MUP_R, V_MUP_K, V_MUP_V, V_MUN_R, V_MUN_K, V_MUN_V, V_KK, V_KA, V_RK, V_W0F, V_W0B, V_A0F, V_A0B = range(13)


def _shift(cur, prv8, nxt8, mup, mun, i, nt):
    tt = cur.shape[0]
    row = lax.broadcasted_iota(I32, cur.shape, 0)
    first = jnp.where(i > 0, prv8[7:8, :], 0.0)
    last = jnp.where(i < nt - 1, nxt8[0:1, :], 0.0)
    prev = jnp.where(row == 0, first, pltpu.roll(cur, 1, 0))
    nxt = jnp.where(row == tt - 1, last, pltpu.roll(cur, tt - 1, 0))
    return cur * (1.0 - mup - mun) + mup * prev + mun * nxt


def _prep_kernel(rc, kc, vc, lc, rp, kp, vp, lp, rn, kn, vn, ln_, vec_ref, lvec_ref, w2_ref, a2_ref, g2_ref,
                 hs_ref, he_ref, r_o, v_o, kk_o, lwf_o, kf_o, bf_o, lwb_o, kb_o, bb_o, bonus_o, g_o):
    i = pl.program_id(1)
    nt = pl.num_programs(1)
    vec = vec_ref[...]
    row = lambda j: vec[j:j + 1, :]
    r = _shift(rc[0], rp[0], rn[0], row(V_MUP_R), row(V_MUN_R), i, nt)
    k = _shift(kc[0], kp[0], kn[0], row(V_MUP_K), row(V_MUN_K), i, nt)
    v = _shift(vc[0], vp[0], vn[0], row(V_MUP_V), row(V_MUN_V), i, nt)
    lo = _shift(lc[0], lp[0], ln_[0], lvec_ref[0:1, :], lvec_ref[1:2, :], i, nt)

    hsum, hexp = hs_ref[...], he_ref[...]
    kk = k * row(V_KK)
    kk = kk / jnp.maximum(jnp.sqrt(_head_sums(kk * kk, hsum, hexp)), 1e-12)

    tanh_lo = jnp.tanh(lo)
    ksum = jnp.zeros_like(k)
    outs = ((lwf_o, kf_o, bf_o), (lwb_o, kb_o, bb_o))
    for d in range(2):
        z = row(V_W0F + d) + _dot(tanh_lo, w2_ref[d])
        nz = -z
        softplus = jnp.maximum(nz, 0.0) + jnp.log(1.0 + jnp.exp(-jnp.abs(nz)))
        w_log = -softplus - 0.5
        lw = -jnp.exp(w_log)
        a = _sigmoid(row(V_A0F + d) + _dot(lo, a2_ref[d]))
        k_d = k * (1.0 + (a - 1.0) * row(V_KA))
        ksum = ksum + k_d
        lw_o, kd_o, bd_o = outs[d]
        lw_o[0] = lw
        kd_o[0] = k_d
        bd_o[0] = kk * a
    r_o[0] = r
    v_o[0] = v
    kk_o[0] = kk
    bonus_o[0] = _head_sums(r * ksum * row(V_RK), hsum, hexp) * v
    g_o[0] = _dot(_sigmoid(lo), g2_ref[...])


def _rwkv_prep(proj3, vec, lvec, w2f, a2f, g2f, hsum, hexp):
    b, t, _ = proj3.shape
    tt = PREP_TT
    nt = t // tt
    h8 = tt // 8
    cur = lambda cb, w: pl.BlockSpec((1, tt, w), lambda bi, i: (bi, i, cb))
    prv = lambda cb, w: pl.BlockSpec((1, 8, w), lambda bi, i: (bi, jnp.maximum(i * h8 - 1, 0), cb))
    nxt = lambda cb, w: pl.BlockSpec((1, 8, w), lambda bi, i: (bi, jnp.minimum((i + 1) * h8, t // 8 - 1), cb))
    cols = [(COL_R // A_WIDTH, A_WIDTH), (COL_K // A_WIDTH, A_WIDTH), (COL_V // A_WIDTH, A_WIDTH),
            (COL_L // LORA_PAD, LORA_PAD)]
    full = lambda shape: pl.BlockSpec(shape, lambda bi, i: (0,) * len(shape))
    in_specs = ([cur(*c) for c in cols] + [prv(*c) for c in cols] + [nxt(*c) for c in cols]
                + [full(vec.shape), full(lvec.shape), full(w2f.shape), full(a2f.shape), full(g2f.shape),
                   full(hsum.shape), full(hexp.shape)])
    out_spec = pl.BlockSpec((1, tt, A_WIDTH), lambda bi, i: (bi, i, 0))
    out_sds = jax.ShapeDtypeStruct((b, t, A_WIDTH), F32)
    return pl.pallas_call(
        _prep_kernel,
        grid=(b, nt),
        in_specs=in_specs,
        out_specs=[out_spec] * 11,
        out_shape=[out_sds] * 11,
        compiler_params=_cparams(("parallel", "parallel")),
        name="rwkv_prep",
    )(*([proj3] * 12), vec, lvec, w2f, a2f, g2f, hsum, hexp)


def _stack2(x, m0):
    return jnp.concatenate([jnp.where(m0, x, 0.0), jnp.where(m0, 0.0, x)], axis=0)


class _Chain(NamedTuple):
    r: jax.Array
    v: jax.Array
    kk: jax.Array
    lw: jax.Array
    kd: jax.Array
    bd: jax.Array
    s_ref: object
    rev: bool


def _scan_masks(c, reverse):
    ti = lax.broadcasted_iota(I32, (c, c), 0)
    si = lax.broadcasted_iota(I32, (c, c), 1)
    tri = jnp.where((si >= ti) if reverse else (si <= ti), 1.0, 0.0).astype(BF16)
    t2 = lax.broadcasted_iota(I32, (c, 2 * c), 0)
    s2 = lax.broadcasted_iota(I32, (c, 2 * c), 1)
    s2 = jnp.where(s2 >= c, s2 - c, s2)
    strict = (s2 > t2) if reverse else (s2 < t2)
    incl = (s2 >= t2) if reverse else (s2 <= t2)
    eye2 = jnp.where(s2 == t2, 1.0, 0.0)
    return tri, strict, incl, eye2


def _scan_chunk(chains):
    c = SCAN_CHUNK
    masks = {rev: _scan_masks(c, rev) for rev in (False, True)}
    m0 = lax.broadcasted_iota(I32, (c, LANES), 1) < A_HEAD_DIM
    half = lax.broadcasted_iota(I32, (c, 2 * c), 1) < c
    vi = lax.broadcasted_iota(I32, (LANES, LANES), 0)
    ki = lax.broadcasted_iota(I32, (LANES, LANES), 1)
    same_head = (vi < A_HEAD_DIM) == (ki < A_HEAD_DIM)
    bf = lambda x: x.astype(BF16)
    mm = lambda a, b: jnp.dot(a, b, preferred_element_type=F32)
    mm_nt = lambda a, b: lax.dot_general(a, b, (((1,), (1,)), ((), ())), preferred_element_type=F32)
    cat = jnp.concatenate

    cum = []
    for ch in chains:
        tri = masks[ch.rev][0]
        parts = mm(tri, cat(_split3(ch.lw), axis=1))
        cum.append(parts[:, :LANES] + parts[:, LANES:2 * LANES] + parts[:, 2 * LANES:])

    ar, bk2, bkw, v2, etot = [], [], [], [], []
    for ch, cm in zip(chains, cum):
        tot = cm[0:1, :] if ch.rev else cm[c - 1:c, :]
        einv = jnp.exp(-cm)
        etail = jnp.exp(tot - cm)
        at = -ch.kk * jnp.exp(cm - ch.lw)
        rt = ch.r * jnp.exp(cm)
        ar.append(bf(cat([at, rt], axis=0)))
        bk2.append(bf(cat([_stack2(ch.bd * einv, m0), _stack2(ch.kd * einv, m0)], axis=0)))
        bkw.append(bf(cat([ch.bd * etail, ch.kd * etail], axis=0)))
        v2.append(bf(_stack2(ch.v, m0)))
        etot.append(jnp.exp(tot))

    s0 = [ch.s_ref[...] for ch in chains]
    both = [mm_nt(a, cat([b, bf(s)], axis=0)) for a, b, s in zip(ar, bk2, s0)]
    mt = [r[:, :4 * c] for r in both]
    ars = [r[:, 4 * c:] for r in both]
    a_ab, a_xk, a_rb = [], [], []
    for ch, m in zip(chains, mt):
        _, strict, incl, _ = masks[ch.rev]
        a_ab.append(jnp.where(strict, m[:c, :2 * c], 0.0))
        a_xk.append(bf(cat([jnp.where(strict, m[:c, 2 * c:], 0.0), jnp.where(incl, m[c:, 2 * c:], 0.0)], axis=0)))
        a_rb.append(bf(jnp.where(incl, m[c:, :2 * c], 0.0)))
    xkv = [mm(a, v) for a, v in zip(a_xk, v2)]

    tm = [masks[ch.rev][3] + a for ch, a in zip(chains, a_ab)]
    pw = [mm(bf(a), _stack2(bf(a), half)) for a in a_ab]
    steps = int(np.log2(c)) - 1
    for k in range(steps):
        blk = [_stack2(bf(x), half) for x in pw]
        if k < steps - 1:
            both = [mm(bf(cat([t, x], axis=0)), d) for t, x, d in zip(tm, pw, blk)]
            tm = [t + r[:c] for t, r in zip(tm, both)]
            pw = [r[c:] for r in both]
        else:
            tm = [t + mm(bf(t), d) for t, d in zip(tm, blk)]

    u = [mm(bf(t), bf(_stack2(a[:c] + k[:c], m0))) for t, a, k in zip(tm, ars, xkv)]
    y = [a[c:] + k[c:] + mm(ab, bf(_stack2(uu, m0))) for a, k, ab, uu in zip(ars, xkv, a_rb, u)]
    upd = [mm(bf(cat([uu, ch.v], axis=0).T), w) for uu, ch, w in zip(u, chains, bkw)]
    for ch, s, e, up in zip(chains, s0, etot, upd):
        ch.s_ref[...] = s * e + jnp.where(same_head, up, 0.0)
    return y


SCAN_HP = 8


def _scan_kernel(rf, vf, kkf, lwf, kf, bf, rb, vb, kkb, lwb, kb, bb, yf_o, yb_o, sf_ref, sb_ref):
    @pl.when(pl.program_id(2) == 0)
    def _():
        sf_ref[...] = jnp.zeros_like(sf_ref)
        sb_ref[...] = jnp.zeros_like(sb_ref)

    chains = []
    for hp in range(SCAN_HP):
        ln = slice(hp * LANES, (hp + 1) * LANES)
        chains.append(_Chain(rf[0, :, ln], vf[0, :, ln], kkf[0, :, ln], lwf[0, :, ln], kf[0, :, ln], bf[0, :, ln],
                             sf_ref.at[hp], False))
        chains.append(_Chain(rb[0, :, ln], vb[0, :, ln], kkb[0, :, ln], lwb[0, :, ln], kb[0, :, ln], bb[0, :, ln],
                             sb_ref.at[hp], True))
    y = _scan_chunk(chains)
    for hp in range(SCAN_HP):
        ln = slice(hp * LANES, (hp + 1) * LANES)
        yf_o[0, :, ln] = y[2 * hp]
        yb_o[0, :, ln] = y[2 * hp + 1]


def _rwkv_scan(r, v, kk, lwf, kf, bf, lwb, kb, bb):
    b, t, _ = r.shape
    c = SCAN_CHUNK
    nc = t // c
    w = SCAN_HP * LANES
    fwd = pl.BlockSpec((1, c, w), lambda bi, hp, ci: (bi, ci, hp))
    bwd = pl.BlockSpec((1, c, w), lambda bi, hp, ci: (bi, nc - 1 - ci, hp))
    out_sds = jax.ShapeDtypeStruct((b, t, A_WIDTH), F32)
    return pl.pallas_call(
        _scan_kernel,
        grid=(b, A_WIDTH // w, nc),
        in_specs=[fwd] * 6 + [bwd] * 6,
        out_specs=[fwd, bwd],
        out_shape=[out_sds, out_sds],
        scratch_shapes=[pltpu.VMEM((SCAN_HP, LANES, LANES), F32), pltpu.VMEM((SCAN_HP, LANES, LANES), F32)],
        compiler_params=_cparams(("parallel", "parallel", "arbitrary")),
        name="rwkv_scan",
    )(r, v, kk, lwf, kf, bf, r, v, kk, lwb, kb, bb)


POST_TT = 256


def _post_kernel(yf, yb, bonus, g, vec_ref, hs_ref, he_ref, o_ref):
    y = yf[...] + yb[...]
    hsum, hexp = hs_ref[...], he_ref[...]
    inv = 1.0 / A_HEAD_DIM
    mu = _head_sums(y, hsum, hexp) * inv
    d = y - mu
    var = _head_sums(d * d, hsum, hexp) * inv
    yn = d * lax.rsqrt(var + GN_EPS) * vec_ref[0:1, :] + vec_ref[1:2, :]
    o_ref[...] = ((yn + bonus[...]) * g[...]).astype(o_ref.dtype)


def _rwkv_post(yf, yb, bonus, g, gnvec, hsum, hexp):
    n = yf.shape[0]
    tt = min(POST_TT, n)
    blk = pl.BlockSpec((tt, A_WIDTH), lambda i: (i, 0))
    full = lambda a: pl.BlockSpec(a.shape, lambda i: (0,) * a.ndim, pipeline_mode=pl.Buffered(1))
    return pl.pallas_call(
        _post_kernel,
        grid=(n // tt,),
        in_specs=[blk, blk, blk, blk, full(gnvec), full(hsum), full(hexp)],
        out_specs=blk,
        out_shape=jax.ShapeDtypeStruct((n, A_WIDTH), BF16),
        compiler_params=_cparams(("parallel",)),
        name="rwkv_post",
    )(yf, yb, bonus, g, gnvec, hsum, hexp)


def _gelu_tanh(x):
    return 0.5 * x * (1.0 + jnp.tanh(np.sqrt(2.0 / np.pi).astype(np.float32) * (x + 0.044715 * (x * x * x))))


def _sg_kernel(pb_ref, lnv_ref, w_ref, bias_ref, o_ref):
    z = _gelu_tanh(pb_ref[0])
    u = z[:, :B_WIDTH]
    v = _layer_norm_rows(z[:, B_WIDTH:], lnv_ref[0:1, :], lnv_ref[1:2, :])
    lane = lax.broadcasted_iota(I32, (B_CHUNK, LANES), 1)
    m0 = lane < B_GROUP_DIM
    parts = []
    for q in range(B_WIDTH // LANES):
        vq = v[:, q * LANES:(q + 1) * LANES]
        parts.append(_dot(w_ref[q], _stack2(vq, m0)))
    mixed = jnp.concatenate(parts, axis=1) + bias_ref[...]
    o_ref[0] = (u * mixed).astype(o_ref.dtype)


def _spatial_gating(proj3, lnvec, w2, bias_full):
    b, t, _ = proj3.shape
    full = lambda a: pl.BlockSpec(a.shape, lambda bi, i: (0,) * a.ndim)
    return pl.pallas_call(
        _sg_kernel,
        grid=(b, t // B_CHUNK),
        in_specs=[pl.BlockSpec((1, B_CHUNK, B_COLS), lambda bi, i: (bi, i, COL_B // B_COLS)),
                  full(lnvec), full(w2), full(bias_full)],
        out_specs=pl.BlockSpec((1, B_CHUNK, B_WIDTH), lambda bi, i: (bi, i, 0)),
        out_shape=jax.ShapeDtypeStruct((b, t, B_WIDTH), BF16),
        compiler_params=_cparams(("parallel", "parallel")),
        name="spatial_gating",
    )(proj3, lnvec, w2, bias_full)


NA_WIN = C_WIN_ROWS * GRID_W


def _na_row_start(i, rows):
    return jnp.clip(i - C_WIN_ROWS // 2, 0, rows - C_WIN_ROWS)


NA_QR = 4


def _na_kernel(q_ref, k_ref, v_ref, *rest, rows):
    bias_refs, o_ref = rest[:NA_QR], rest[NA_QR]
    kh_ref, vh_ref = rest[NA_QR + 1:]
    i = pl.program_id(1)
    scale = C_HEAD_DIM ** -0.5

    @pl.when(i == 0)
    def _():
        for h in range(C_HEADS):
            sl = slice(h * C_HEAD_DIM, (h + 1) * C_HEAD_DIM)
            kh_ref[h] = k_ref[0, :, sl]
            vh_ref[h] = v_ref[0, :, sl]

    qs, ks, vs, bs = [], [], [], []
    for j in range(NA_QR):
        start = pl.multiple_of(_na_row_start(i * NA_QR + j, rows) * GRID_W, GRID_W)
        q = q_ref[0, j * GRID_W:(j + 1) * GRID_W, :]
        for h in range(C_HEADS):
            sl = slice(h * C_HEAD_DIM, (h + 1) * C_HEAD_DIM)
            qs.append(q[:, sl])
            ks.append(kh_ref[h, pl.ds(start, NA_WIN), :])
            vs.append(vh_ref[h, pl.ds(start, NA_WIN), :])
            bs.append(bias_refs[j][0, h])
    nt = lambda a, b: lax.dot_general(a, b, (((1,), (1,)), ((), ())), preferred_element_type=F32)
    ss = [nt(q, k) * scale + bias for q, k, bias in zip(qs, ks, bs)]
    ps = []
    for s in ss:
        e = jnp.exp(s - jnp.max(s, axis=-1, keepdims=True))
        ps.append((e / jnp.sum(e, axis=-1, keepdims=True)).astype(BF16))
    os_ = [jnp.dot(p, v, preferred_element_type=F32) for p, v in zip(ps, vs)]
    for j in range(NA_QR):
        o_ref[0, j * GRID_W:(j + 1) * GRID_W, :] = jnp.concatenate(
            os_[j * C_HEADS:(j + 1) * C_HEADS], axis=1).astype(o_ref.dtype)


def _neighbourhood_attention(proj3, bias_tab):
    b, t, _ = proj3.shape
    rows = t // GRID_W
    cq = COL_C // C_WIDTH
    qr = NA_QR
    seq = lambda cb: pl.BlockSpec((1, t, C_WIDTH), lambda bi, i: (bi, 0, cb))
    bias = lambda j: pl.BlockSpec((1, C_HEADS, GRID_W, NA_WIN),
                                  lambda bi, i: (i * qr + j - _na_row_start(i * qr + j, rows), 0, 0, 0))
    return pl.pallas_call(
        functools.partial(_na_kernel, rows=rows),
        grid=(b, rows // qr),
        in_specs=[pl.BlockSpec((1, qr * GRID_W, C_WIDTH), lambda bi, i: (bi, i, cq)), seq(cq + 1), seq(cq + 2)]
                 + [bias(j) for j in range(qr)],
        out_specs=pl.BlockSpec((1, qr * GRID_W, C_WIDTH), lambda bi, i: (bi, i, 0)),
        out_shape=jax.ShapeDtypeStruct((b, t, C_WIDTH), BF16),
        scratch_shapes=[pltpu.VMEM((C_HEADS, t, C_HEAD_DIM), BF16), pltpu.VMEM((C_HEADS, t, C_HEAD_DIM), BF16)],
        compiler_params=_cparams(("parallel", "arbitrary")),
        name="nbr_attention",
    )(proj3, proj3, proj3, *([bias_tab] * qr))


def _na_bias_table(rpb, rows):
    kc = C_WIN_COLS
    cols = np.arange(GRID_W)
    col_start = np.clip(cols - kc // 2, 0, GRID_W - kc)
    key_col = np.arange(GRID_W)
    in_win = (key_col[None, :] >= col_start[:, None]) & (key_col[None, :] < col_start[:, None] + kc)
    delta = np.arange(C_WIN_ROWS)
    row_off = np.arange(C_WIN_ROWS)[None, :] - delta[:, None] + (C_WIN_ROWS - 1)
    by_row = rpb[:, row_off]
    span = 2 * GRID_W
    lead = GRID_W - C_WIN_COLS
    padded = jnp.pad(by_row, [(0, 0)] * 3 + [(lead, span - lead - (2 * C_WIN_COLS - 1))])
    skew = jnp.tile(padded, GRID_W)[..., :GRID_W * (span - 1)].reshape(by_row.shape[:3] + (GRID_W, span - 1))
    bias = skew[..., GRID_W - 1:]
    bias = jnp.where(in_win, bias, -1e30)
    bias = jnp.transpose(bias, (1, 0, 3, 2, 4)).reshape(C_WIN_ROWS, C_HEADS, GRID_W, NA_WIN)
    return bias.astype(F32)


MERGE_TM = 512


def _merge_kernel(ya, yb, yc, ga, gb, gc, pa, pb, pc, o_ref):
    m = _sigmoid(ga[...].astype(F32)) * jnp.dot(ya[...], pa[...], preferred_element_type=F32)
    m = m + _sigmoid(gb[...].astype(F32)) * jnp.dot(yb[...], pb[...], preferred_element_type=F32)
    m = m + _sigmoid(gc[...].astype(F32)) * jnp.dot(yc[...], pc[...], preferred_element_type=F32)
    o_ref[...] = m.astype(o_ref.dtype)


def _merge(ya, yb, yc, proj, p_a, p_b, p_c):
    n = ya.shape[0]
    tm = min(MERGE_TM, n)
    rowblk = lambda w: pl.BlockSpec((tm, w), lambda i: (i, 0))
    gate = lambda j: pl.BlockSpec((tm, D_MODEL), lambda i: (i, COL_G // D_MODEL + j))
    full = lambda a: pl.BlockSpec(a.shape, lambda i: (0,) * a.ndim, pipeline_mode=pl.Buffered(1))
    return pl.pallas_call(
        _merge_kernel,
        grid=(n // tm,),
        in_specs=[rowblk(A_WIDTH), rowblk(B_WIDTH), rowblk(C_WIDTH), gate(0), gate(1), gate(2),
                  full(p_a), full(p_b), full(p_c)],
        out_specs=rowblk(D_MODEL),
        out_shape=jax.ShapeDtypeStruct((n, D_MODEL), BF16),
        compiler_params=_cparams(("parallel",)),
        name="branch_merge",
    )(ya, yb, yc, proj, proj, proj, p_a, p_b, p_c)


OUT_TM = 512


def _outln_kernel(m_ref, w_ref, x_ref, lnv_ref, o_ref, ob_ref):
    h = jnp.dot(m_ref[...], w_ref[...], preferred_element_type=F32)
    y = _layer_norm_rows(DEEPNORM_ALPHA * x_ref[...] + h, lnv_ref[0:1, :], lnv_ref[1:2, :])
    o_ref[...] = y
    ob_ref[...] = y.astype(BF16)


def _out_proj_ln(m, w_out, x, lnvec):
    n = m.shape[0]
    tm = min(OUT_TM, n)
    blk = pl.BlockSpec((tm, D_MODEL), lambda i: (i, 0))
    full = lambda a: pl.BlockSpec(a.shape, lambda i: (0,) * a.ndim, pipeline_mode=pl.Buffered(1))
    return pl.pallas_call(
        _outln_kernel,
        grid=(n // tm,),
        in_specs=[blk, full(w_out), blk, full(lnvec)],
        out_specs=[blk, blk],
        out_shape=[jax.ShapeDtypeStruct((n, D_MODEL), F32), jax.ShapeDtypeStruct((n, D_MODEL), BF16)],
        compiler_params=_cparams(("parallel",)),
        name="out_proj_ln",
    )(m, w_out, x, lnvec)


ROUTER_TM = 512


def _router_kernel(x_ref, wt_ref, o_ref):
    xh, xl = _split2(x_ref[...])
    wh, wl = _split2(wt_ref[...])
    nt = lambda a, b: lax.dot_general(a, b, (((1,), (1,)), ((), ())), preferred_element_type=F32)
    logits = nt(wh, xh) + nt(wl, xh) + nt(wh, xl)
    logits = logits - jnp.max(logits, axis=0, keepdims=True)
    e = jnp.exp(logits)
    o_ref[...] = e / jnp.sum(e, axis=0, keepdims=True)


def _router(x, w_router_t):
    n = x.shape[0]
    tm = min(ROUTER_TM, n)
    return pl.pallas_call(
        _router_kernel,
        grid=(n // tm,),
        in_specs=[pl.BlockSpec((tm, D_MODEL), lambda i: (i, 0)),
                  pl.BlockSpec(w_router_t.shape, lambda i: (0, 0))],
        out_specs=pl.BlockSpec((N_EXPERTS, tm), lambda i: (0, i)),
        out_shape=jax.ShapeDtypeStruct((N_EXPERTS, n), F32),
        compiler_params=_cparams(("parallel",)),
        name="router",
    )(x, w_router_t)


def _select_kernel(aff_ref, pos_ref, off_ref, *, cap):
    aff = aff_ref[...]
    e_, g_, l_ = aff.shape
    n = g_ * l_
    bits = pltpu.bitcast(aff, I32)

    def count(mask):
        c = jnp.sum(jnp.where(mask, 1.0, 0.0), axis=2, keepdims=True)
        return jnp.sum(c, axis=1, keepdims=True)

    def thr_body(_, carry):
        lo, hi = carry
        mid = lo + (hi - lo + 1) // 2
        ok = count(bits >= mid) >= cap
        return jnp.where(ok, mid, lo), jnp.where(ok, hi, mid - 1)

    lo0 = jnp.zeros((e_, 1, 1), I32)
    hi0 = jnp.full((e_, 1, 1), 0x7F800000, I32)
    thr, _ = lax.fori_loop(0, 32, thr_body, (lo0, hi0))
    gt = bits > thr
    tie = bits == thr
    need = cap - count(gt)
    idx = lax.broadcasted_iota(I32, aff.shape, 1) * l_ + lax.broadcasted_iota(I32, aff.shape, 2)

    def idx_body(_, carry):
        lo, hi = carry
        mid = (lo + hi) // 2
        ok = count(tie & (idx <= mid)) >= need
        return jnp.where(ok, lo, mid + 1), jnp.where(ok, mid, hi)

    lo1 = jnp.zeros((e_, 1, 1), I32)
    hi1 = jnp.full((e_, 1, 1), n - 1, I32)
    cut, _ = lax.fori_loop(0, int(np.ceil(np.log2(n))) + 1, idx_body, (lo1, hi1))
    sel = jnp.where(gt | (tie & (idx <= cut)), 1.0, 0.0)

    sel2 = sel.reshape(e_ * g_, l_)
    ls = lax.broadcasted_iota(I32, (l_, l_), 0)
    lt = lax.broadcasted_iota(I32, (l_, l_), 1)
    incl = jnp.dot(sel2.astype(BF16), jnp.where(ls <= lt, 1.0, 0.0).astype(BF16), preferred_element_type=F32)
    tot = jnp.dot(sel2.astype(BF16), jnp.ones((l_, l_), BF16), preferred_element_type=F32)
    gs = lax.broadcasted_iota(I32, (g_, g_), 0)
    gt_ = lax.broadcasted_iota(I32, (g_, g_), 1)
    lower = jnp.where(gt_ < gs, 1.0, 0.0).astype(BF16)
    offs = [jnp.dot(lower, tot[e * g_:(e + 1) * g_].astype(BF16), preferred_element_type=F32)
            for e in range(e_)]
    off = jnp.concatenate(offs, axis=0)
    pos = incl + off - sel2
    pos_ref[...] = jnp.where(sel2 > 0.5, pos, -1.0).astype(I32).reshape(e_, g_, l_)
    off_ref[...] = off.astype(I32).reshape(e_, g_, l_)


def _select(aff3, cap):
    full = pl.BlockSpec(aff3.shape, lambda i: (0, 0, 0))
    sds = jax.ShapeDtypeStruct(aff3.shape, I32)
    return pl.pallas_call(
        functools.partial(_select_kernel, cap=cap),
        grid=(1,),
        in_specs=[full],
        out_specs=[full, full],
        out_shape=[sds, sds],
        compiler_params=_cparams(("arbitrary",)),
        name="expert_select",
    )(aff3)


def _gather_kernel(tile_s, valid_s, e_g, slab_g, first_g, *refs):
    k_ = GATHER_GROUP
    x_refs, pos_refs, aff_refs = refs[:k_], refs[k_:2 * k_], refs[2 * k_:3 * k_]
    o_ref, g_ref = refs[3 * k_:]
    g = pl.program_id(0)
    t = MOE_TILE

    @pl.when(first_g[g] == 1)
    def _():
        o_ref[...] = jnp.zeros_like(o_ref)
        g_ref[...] = jnp.zeros_like(g_ref)

    nvalid = valid_s[g * k_]
    for k in range(1, k_):
        nvalid = nvalid + valid_s[g * k_ + k]

    @pl.when(nvalid > 0)
    def _():
        rank = lax.broadcasted_iota(I32, (t, GATHER_TOK), 0) + slab_g[g] * t
        got = o_ref[0].astype(F32)
        gate = g_ref[0]
        per_dot = MXU_DIM // GATHER_TOK
        for k0 in range(0, k_, per_dot):
            onehots, xs = [], []
            for k in range(k0, k0 + per_dot):
                want = jnp.where(valid_s[g * k_ + k] == 1, rank, -2)
                hit = pos_refs[k][0] == want
                onehots.append(jnp.where(hit, 1.0, 0.0).astype(BF16))
                xs.append(x_refs[k][...])
                gate = gate + jnp.sum(jnp.where(hit, aff_refs[k][0], 0.0), axis=1, keepdims=True)
            got = got + jnp.dot(jnp.concatenate(onehots, axis=1), jnp.concatenate(xs, axis=0),
                                preferred_element_type=F32)
        o_ref[0] = got.astype(o_ref.dtype)
        g_ref[0] = gate


def _moe_gather(x_bf, pos_en, aff_en, items, cap):
    n = x_bf.shape[0]
    t = MOE_TILE
    k_ = GATHER_GROUP
    ng = items[2].shape[0]
    tk = GATHER_TOK
    xs = [pl.BlockSpec((tk, D_MODEL), lambda g, tl, va, e, s, fi, k=k: (tl[g * k_ + k], 0)) for k in range(k_)]
    rows = [pl.BlockSpec((1, 1, tk), lambda g, tl, va, e, s, fi, k=k: (e[g], 0, tl[g * k_ + k])) for k in range(k_)]
    grid_spec = pltpu.PrefetchScalarGridSpec(
        num_scalar_prefetch=5,
        grid=(ng,),
        in_specs=xs + rows + rows,
        out_specs=[pl.BlockSpec((1, t, D_MODEL), lambda g, tl, va, e, s, fi: (e[g], s[g], 0)),
                   pl.BlockSpec((1, t, 1), lambda g, tl, va, e, s, fi: (e[g], s[g], 0))],
    )
    pos3 = pos_en.reshape(N_EXPERTS, 1, n)
    aff3 = aff_en.reshape(N_EXPERTS, 1, n)
    return pl.pallas_call(
        _gather_kernel,
        grid_spec=grid_spec,
        out_shape=[jax.ShapeDtypeStruct((N_EXPERTS, cap, D_MODEL), BF16),
                   jax.ShapeDtypeStruct((N_EXPERTS, cap, 1), F32)],
        compiler_params=_cparams(("arbitrary",)),
        name="moe_gather",
    )(*items, *([x_bf] * k_), *([pos3] * k_), *([aff3] * k_))


FFN_TM = 512


def _ffn_kernel(x_ref, gate_ref, wg_ref, wu_ref, wd_ref, o_ref):
    x = x_ref[0]
    g = jnp.dot(x, wg_ref[0], preferred_element_type=F32)
    u = jnp.dot(x, wu_ref[0], preferred_element_type=F32)
    h = (g * _sigmoid(g)) * u
    y = jnp.dot(h.astype(BF16), wd_ref[0], preferred_element_type=F32) * gate_ref[0]
    o_ref[0] = y.astype(o_ref.dtype)


def _expert_ffn(xe, gate, wg, wu, wd, layer):
    e_, cap, _ = xe.shape
    tm = min(FFN_TM, cap)
    wspec = lambda a: pl.BlockSpec((None, 1) + a.shape[2:], lambda e, i: (layer, e, 0, 0))
    rows = lambda w: pl.BlockSpec((1, tm, w), lambda e, i: (e, i, 0))
    sds = jax.ShapeDtypeStruct((e_, cap, D_MODEL), BF16)
    return pl.pallas_call(
        _ffn_kernel,
        grid=(e_, cap // tm),
        in_specs=[rows(D_MODEL), rows(1), wspec(wg), wspec(wu), wspec(wd)],
        out_specs=rows(D_MODEL),
        out_shape=sds,
        compiler_params=_cparams(("parallel", "arbitrary")),
        name="expert_ffn",
    )(xe, gate, wg, wu, wd)


def _combine_kernel(e_s, slab_s, valid_s, tile_g, first_g, last_g, *refs):
    k_ = COMBINE_GROUP
    ye_refs = refs[:k_]
    pos_ref, x_ref, lnv_ref, o_ref, ob_ref, acc_ref = refs[k_:]
    g = pl.program_id(0)
    t = MOE_TILE

    @pl.when(first_g[g] == 1)
    def _():
        acc_ref[...] = jnp.zeros_like(acc_ref)

    nvalid = valid_s[g * k_]
    for k in range(1, k_):
        nvalid = nvalid + valid_s[g * k_ + k]

    @pl.when(nvalid > 0)
    def _():
        w = k_ * MOE_SUB
        p_hi, p_lo = _split2(pos_ref[...].astype(F32))
        expert = lax.broadcasted_iota(I32, (N_EXPERTS, w), 0)
        col_e = lax.broadcasted_iota(I32, (N_EXPERTS, w), 1)
        col = lax.broadcasted_iota(I32, (1, w), 1)
        pick = jnp.zeros((N_EXPERTS, w), F32)
        want = jnp.zeros((1, w), I32)
        for k in range(k_):
            s = g * k_ + k
            in_slot = (col >= k * MOE_SUB) & (col < (k + 1) * MOE_SUB)
            in_slot_e = (col_e >= k * MOE_SUB) & (col_e < (k + 1) * MOE_SUB)
            pick = jnp.where(in_slot_e & (expert == e_s[s]), 1.0, pick)
            base = jnp.where(valid_s[s] == 1, slab_s[s] * MOE_SUB, -2 * MOE_SUB)
            want = jnp.where(in_slot, col - k * MOE_SUB + base, want)
        pick = pick.astype(BF16)
        pos = jnp.dot(p_hi, pick, preferred_element_type=F32) + jnp.dot(p_lo, pick, preferred_element_type=F32)
        onehot = jnp.where(pos == want.astype(F32), 1.0, 0.0).astype(BF16)
        ye = jnp.concatenate([r[0] for r in ye_refs], axis=0)
        acc_ref[...] += jnp.dot(onehot, ye, preferred_element_type=F32)

    @pl.when(last_g[g] == 1)
    def _():
        y = _layer_norm_rows(DEEPNORM_ALPHA * x_ref[...] + acc_ref[...], lnv_ref[0:1, :], lnv_ref[1:2, :])
        o_ref[...] = y
        ob_ref[...] = y.astype(BF16)


def _moe_combine(ye, pos_ne, x, lnvec, items):
    n = x.shape[0]
    t = MOE_TILE
    k_ = COMBINE_GROUP
    ng = items[3].shape[0]
    tile_blk = lambda w: pl.BlockSpec((t, w), lambda g, e, s, va, tl, fi, la: (tl[g], 0))
    slabs = [pl.BlockSpec((1, MOE_SUB, D_MODEL), lambda g, e, s, va, tl, fi, la, k=k: (e[g * k_ + k], s[g * k_ + k], 0))
             for k in range(k_)]
    grid_spec = pltpu.PrefetchScalarGridSpec(
        num_scalar_prefetch=6,
        grid=(ng,),
        in_specs=slabs + [tile_blk(N_EXPERTS), tile_blk(D_MODEL),
                          pl.BlockSpec(lnvec.shape, lambda g, e, s, va, tl, fi, la: (0, 0))],
        out_specs=[tile_blk(D_MODEL), tile_blk(D_MODEL)],
        scratch_shapes=[pltpu.VMEM((t, D_MODEL), F32)],
    )
    return pl.pallas_call(
        _combine_kernel,
        grid_spec=grid_spec,
        out_shape=[jax.ShapeDtypeStruct((n, D_MODEL), F32), jax.ShapeDtypeStruct((n, D_MODEL), BF16)],
        compiler_params=_cparams(("arbitrary",)),
        name="moe_combine",
    )(*items, *([ye] * k_), pos_ne, x, lnvec)


def _moe_items(group_off, n, cap):
    e_ = N_EXPERTS
    big = jnp.int32(2 ** 30)

    def build(rows, by_tile, k_, t):
        ns = cap // rows
        nt = n // t
        starts = group_off[:, ::t // LANES, 0]
        ends = jnp.concatenate([starts[:, 1:], jnp.full((e_, 1), cap, I32)], axis=1)
        e_p = jnp.broadcast_to(jnp.arange(e_, dtype=I32)[:, None], (e_, nt))
        tile_p = jnp.broadcast_to(jnp.arange(nt, dtype=I32)[None, :], (e_, nt))
        slab_p = jnp.minimum(starts // rows, ns - 1)
        valid_p = (ends > starts).astype(I32)
        bound = jnp.arange(ns, dtype=I32) * rows
        tile_s = jnp.sum((starts[:, :, None] <= bound[None, None, :]).astype(I32), axis=1) - 1
        tile_s = jnp.clip(tile_s, 0, nt - 1)
        at_tile = tile_s[:, :, None] == jnp.arange(nt, dtype=I32)[None, None, :]
        st_s = jnp.sum(jnp.where(at_tile, starts[:, None, :], 0), axis=2)
        en_s = jnp.sum(jnp.where(at_tile, ends[:, None, :], 0), axis=2)
        valid_s = ((st_s < bound[None, :]) & (bound[None, :] < en_s)).astype(I32)
        e_s = jnp.broadcast_to(jnp.arange(e_, dtype=I32)[:, None], (e_, ns))
        slab_s = jnp.broadcast_to(jnp.arange(ns, dtype=I32)[None, :], (e_, ns))
        cat = lambda a, b: jnp.concatenate([a.reshape(-1), b.reshape(-1)])
        e_a, slab_a, tile_a = cat(e_p, e_s), cat(slab_p, slab_s), cat(tile_p, tile_s)
        valid_a = cat(valid_p, valid_s)
        secondary = cat(jnp.zeros_like(e_p), jnp.ones_like(e_s))
        ni = e_a.shape[0]
        ar = jnp.arange(ni, dtype=I32)
        if by_tile:
            keep = jnp.maximum(valid_a, 1 - secondary)
            key = ((tile_a * e_ + e_a) * 2 + secondary) * ns + slab_a
        else:
            keep = valid_a
            key = (e_a * ns + slab_a) * nt + tile_a
        key = jnp.where(keep == 1, key, big + ar)
        rank = jnp.sum((key[None, :] < key[:, None]).astype(I32), axis=1)
        code = ((e_a * ns + slab_a) * nt + tile_a) * 2 + valid_a
        decode = lambda cd: (cd // (2 * nt * ns), (cd // (2 * nt)) % ns, (cd // 2) % nt, cd % 2)
        ocode = jnp.sum(jnp.where(rank[None, :] == ar[:, None], code[None, :], 0), axis=1)
        nkeep = jnp.sum(keep)
        kept = ar < nkeep
        oe, os_, otl, _ = decode(ocode)
        blk = otl if by_tile else oe * ns + os_

        nblk = nt if by_tile else e_ * ns
        ng = -(-ni // k_) + nblk
        first = jnp.concatenate([jnp.ones((1,), bool), blk[1:] != blk[:-1]])
        run_start = lax.cummax(jnp.where(first, ar, 0))
        sub = (ar - run_start) % k_
        leads = (sub == 0) & kept
        gid = jnp.cumsum(leads.astype(I32)) - 1
        slot = jnp.where(kept, gid * k_ + sub, -1)
        sar = jnp.arange(ng * k_, dtype=I32)
        hit = slot[None, :] == sar[:, None]
        has = jnp.sum(hit.astype(I32), axis=1)
        scode = jnp.sum(jnp.where(hit, ocode[None, :], 0), axis=1)
        n_real = jnp.sum(leads.astype(I32))
        gar = jnp.arange(ng, dtype=I32)
        lead = scode.reshape(ng, k_)[:, 0]
        lead = jnp.where(gar < n_real, lead, jnp.sum(jnp.where(gar == n_real - 1, lead, 0)))
        scode = jnp.where(has == 1, scode, jnp.repeat(lead, k_))
        se, ss, stl, sv = decode(scode)
        ge, gsl, gtl, _ = decode(lead)
        gblk = gtl if by_tile else ge * ns + gsl
        change = (gblk[1:] != gblk[:-1]).astype(I32)
        one = jnp.ones((1,), I32)
        slots = dict(e=se, slab=ss, tile=stl, valid=sv * has)
        groups = dict(e=ge, slab=gsl, tile=gtl, first=jnp.concatenate([one, change]),
                      last=jnp.concatenate([change, one]))
        return slots, groups

    gs, gg = build(MOE_TILE, False, GATHER_GROUP, GATHER_TOK)
    cs, cg = build(MOE_SUB, True, COMBINE_GROUP, MOE_TILE)
    return ((gs["tile"], gs["valid"], gg["e"], gg["slab"], gg["first"]),
            (cs["e"], cs["slab"], cs["valid"], cg["tile"], cg["first"], cg["last"]))


def _expert_choice_moe_ln(x, x_bf, w_router_t, wg, wu, wd, layer, lnvec):
    n = x.shape[0]
    cap = EC_CAPACITY_FACTOR * n // N_EXPERTS
    aff_en = _router(x, w_router_t)
    pos3, off3 = _select(aff_en.reshape(N_EXPERTS, n // LANES, LANES), cap)
    pos_en = pos3.reshape(N_EXPERTS, n)
    g_items, c_items = _moe_items(off3, n, cap)
    xe, gate = _moe_gather(x_bf, pos_en, aff_en, g_items, cap)
    ye = _expert_ffn(xe, gate, wg, wu, wd, layer)
    return _moe_combine(ye, pos_en.T, x, lnvec, c_items)


def _pack_cols(w):
    a_end = A_COLS
    b_end = a_end + B_COLS
    c_end = b_end + C_COLS
    pad = jnp.zeros(w.shape[:-1] + (LORA_PAD - LORA_COLS,), w.dtype)
    return (jnp.concatenate([w[..., :3 * A_WIDTH], w[..., a_end:b_end], w[..., 3 * A_WIDTH:a_end], pad], axis=-1),
            jnp.concatenate([w[..., c_end:], w[..., b_end:c_end]], axis=-1))


def _lora_rows(w, start):
    k = w.shape[-2]
    return jnp.pad(w, [(0, 0)] * (w.ndim - 2) + [(start, LORA_PAD - start - k), (0, 0)])


def _prepare(w_in, mu_prev, mu_next, decay_w0, decay_w2, iclr_a0, iclr_a2, gate_g2, k_k, k_a, r_k, gn_g, gn_b,
             sg_ln_g, sg_ln_b, sg_w, sg_b, rpb, p_a, p_b, p_c, w_out, ln_mix_g, ln_mix_b, w_router, e_gate, e_up,
             e_down, ln_ffn_g, ln_ffn_b, rows):
    l_ = w_in.shape[0]
    pad_a = lambda m: jnp.pad(m, ((0, 0), (0, LORA_PAD - LORA_COLS)))
    mup, mun = mu_prev, mu_next
    vec_rows = [mup[:, :A_WIDTH], mup[:, A_WIDTH:2 * A_WIDTH], mup[:, 2 * A_WIDTH:3 * A_WIDTH],
                mun[:, :A_WIDTH], mun[:, A_WIDTH:2 * A_WIDTH], mun[:, 2 * A_WIDTH:3 * A_WIDTH],
                k_k, k_a, r_k.reshape(l_, A_WIDTH), decay_w0[:, 0], decay_w0[:, 1], iclr_a0[:, 0], iclr_a0[:, 1]]
    vec = jnp.stack(vec_rows + [jnp.zeros_like(k_k)] * (16 - len(vec_rows)), axis=1)
    lvec = jnp.stack([pad_a(mup[:, 3 * A_WIDTH:]), pad_a(mun[:, 3 * A_WIDTH:])]
                     + [jnp.zeros((l_, LORA_PAD), F32)] * 6, axis=1)
    w2f = jnp.stack([_lora_rows(decay_w2[:, 0], 0), _lora_rows(decay_w2[:, 1], A_DECAY_LORA)], axis=1)
    a2f = jnp.stack([_lora_rows(iclr_a2[:, 0], 2 * A_DECAY_LORA),
                     _lora_rows(iclr_a2[:, 1], 2 * A_DECAY_LORA + A_ICLR_LORA)], axis=1)
    g2f = _lora_rows(gate_g2, 2 * A_DECAY_LORA + 2 * A_ICLR_LORA)
    head = np.arange(A_WIDTH) // A_HEAD_DIM
    hsum = jnp.asarray(head[:, None] == np.arange(LANES)[None, :], BF16)
    sgw2 = sg_w.reshape(l_, B_GROUPS // 2, 2, B_CHUNK, B_CHUNK).transpose(0, 1, 3, 2, 4)
    sgw2 = sgw2.reshape(l_, B_GROUPS // 2, B_CHUNK, 2 * B_CHUNK)
    sg_bias = jnp.repeat(jnp.swapaxes(sg_b, 1, 2), B_GROUP_DIM, axis=2)
    return dict(
        w_in=_pack_cols(w_in.astype(BF16)), vec=vec, lvec=lvec, w2f=w2f.astype(BF16), a2f=a2f.astype(BF16),
        g2f=g2f.astype(BF16), hsum=hsum, hexp=hsum.T, gnvec=jnp.stack([gn_g, gn_b], axis=1),
        sg_ln=jnp.stack([sg_ln_g, sg_ln_b], axis=1), sgw2=sgw2.astype(BF16), sg_bias=sg_bias,
        na_bias=jnp.stack([_na_bias_table(rpb[l], rows) for l in range(l_)]),
        p_a=p_a.astype(BF16), p_b=p_b.astype(BF16), p_c=p_c.astype(BF16), w_out=w_out.astype(BF16),
        ln_mix=jnp.stack([ln_mix_g, ln_mix_b], axis=1), w_router_t=jnp.swapaxes(w_router, 1, 2),
        e_gate=e_gate.astype(BF16), e_up=e_up.astype(BF16), e_down=e_down.astype(BF16),
        ln_ffn=jnp.stack([ln_ffn_g, ln_ffn_b], axis=1))


def _mixer(x, x_bf, p, l, b, t):
    n = b * t
    w_f, w_h = p["w_in"]
    proj = _matmul(x_bf, w_f, l, 1024, IN_TN)
    proj_h = _matmul(x_bf, w_h, l, 1024, IN_TN, BF16)
    proj3 = proj.reshape(b, t, F_COLS)
    proj3_h = proj_h.reshape(b, t, H_COLS)
    r, v, kk, lwf, kf, bf, lwb, kb, bb, bonus, g = _rwkv_prep(
        proj3, p["vec"][l], p["lvec"][l], p["w2f"][l], p["a2f"][l], p["g2f"][l], p["hsum"], p["hexp"])
    yf, yb = _rwkv_scan(r, v, kk, lwf, kf, bf, lwb, kb, bb)
    flat = lambda a: a.reshape(n, a.shape[-1])
    ya = _rwkv_post(flat(yf), flat(yb), flat(bonus), flat(g), p["gnvec"][l], p["hsum"], p["hexp"])
    ybr = _spatial_gating(proj3, p["sg_ln"][l], p["sgw2"][l], p["sg_bias"][l])
    ycr = _neighbourhood_attention(proj3_h, p["na_bias"][l])
    m = _merge(ya, flat(ybr), flat(ycr), proj_h, p["p_a"][l], p["p_b"][l], p["p_c"][l])
    return _out_proj_ln(m, p["w_out"][l], x, p["ln_mix"][l])


def _trunk(x3, p):
    b, t, _ = x3.shape
    x = x3.reshape(b * t, D_MODEL)
    x_bf = x.astype(BF16)
    for l in range(DEPTH):
        x, x_bf = _mixer(x, x_bf, p, l, b, t)
        x, x_bf = _expert_choice_moe_ln(x, x_bf, p["w_router_t"][l], p["e_gate"], p["e_up"], p["e_down"], l,
                                        p["ln_ffn"][l])
    return x.reshape(b, t, D_MODEL)


def kernel(x_prompt, x_sample, w_in, mu_prev, mu_next, decay_w0, decay_w2, iclr_a0, iclr_a2, gate_g2, k_k, k_a, r_k, gn_g, gn_b, sg_ln_g, sg_ln_b, sg_w, sg_b, rpb, p_a, p_b, p_c, w_out, ln_mix_g, ln_mix_b, w_router, e_gate, e_up, e_down, ln_ffn_g, ln_ffn_b):
    assert x_prompt.shape[1] == x_sample.shape[1]
    rows = x_prompt.shape[1] // GRID_W
    p = _prepare(w_in, mu_prev, mu_next, decay_w0, decay_w2, iclr_a0, iclr_a2, gate_g2, k_k, k_a, r_k, gn_g, gn_b,
                 sg_ln_g, sg_ln_b, sg_w, sg_b, rpb, p_a, p_b, p_c, w_out, ln_mix_g, ln_mix_b, w_router, e_gate,
                 e_up, e_down, ln_ffn_g, ln_ffn_b, rows)
    return (_trunk(x_prompt, p), _trunk(x_sample, p))
```

```python
import functools
from typing import NamedTuple

import numpy as np
import jax
import jax.numpy as jnp
from jax import lax
from jax.experimental import pallas as pl
from jax.experimental.pallas import tpu as pltpu

F32 = jnp.float32
BF16 = jnp.bfloat16
I32 = jnp.int32

D_MODEL = 2048
DEPTH = 4
GRID_W = 64
A_HEADS = 16
A_HEAD_DIM = 64
A_WIDTH = A_HEADS * A_HEAD_DIM
A_DECAY_LORA = 64
A_ICLR_LORA = 64
A_GATE_LORA = 160
B_GROUPS = 8
B_GROUP_DIM = 64
B_WIDTH = B_GROUPS * B_GROUP_DIM
B_CHUNK = 128
C_HEADS = 8
C_HEAD_DIM = 64
C_WIDTH = C_HEADS * C_HEAD_DIM
C_WIN_ROWS = 8
C_WIN_COLS = 16
N_EXPERTS = 16
EXPERT_HIDDEN = 1024
EC_CAPACITY_FACTOR = 2
A_COLS = 3 * A_WIDTH + 2 * A_DECAY_LORA + 2 * A_ICLR_LORA + A_GATE_LORA
B_COLS = 2 * B_WIDTH
C_COLS = 3 * C_WIDTH
G_COLS = 3 * D_MODEL
DEEPNORM_ALPHA = (2 * DEPTH) ** 0.25
LN_EPS = 1e-5
GN_EPS = 64e-5

LORA_COLS = 2 * A_DECAY_LORA + 2 * A_ICLR_LORA + A_GATE_LORA
LORA_PAD = 512
COL_R, COL_K, COL_V = 0, A_WIDTH, 2 * A_WIDTH
COL_B = 3 * A_WIDTH
COL_L = COL_B + B_COLS
F_COLS = COL_L + LORA_PAD
COL_G = 0
COL_C = G_COLS
H_COLS = COL_C + C_COLS
IN_TN = 1536

LANES = 128
MXU_DIM = 256
SCAN_CHUNK = 64
MOE_TILE = 256
MOE_SUB = 64
GATHER_TOK = 256
GATHER_GROUP = 4
COMBINE_GROUP = 16
VMEM_LIMIT = 48 * 1024 * 1024


def _cparams(sem):
    return pltpu.CompilerParams(dimension_semantics=sem, vmem_limit_bytes=VMEM_LIMIT)


def _dot(a, b):
    return jnp.dot(a.astype(BF16), b.astype(BF16), preferred_element_type=F32)


def _dot_nt(a, b):
    return lax.dot_general(a.astype(BF16), b.astype(BF16), (((1,), (1,)), ((), ())),
                           preferred_element_type=F32)


def _split2(x):
    hi = x.astype(BF16)
    lo = (x - hi.astype(F32)).astype(BF16)
    return hi, lo


def _split3(x):
    hi = x.astype(BF16)
    r1 = x - hi.astype(F32)
    mid = r1.astype(BF16)
    lo = (r1 - mid.astype(F32)).astype(BF16)
    return hi, mid, lo


def _dot_x01(x, m01):
    hi, lo = _split2(x)
    m = m01.astype(BF16)
    return jnp.dot(hi, m, preferred_element_type=F32) + jnp.dot(lo, m, preferred_element_type=F32)


def _head_sums(x, hsum, hexp):
    return _dot_x01(_dot_x01(x, hsum), hexp)


def _sigmoid(x):
    return 1.0 / (1.0 + jnp.exp(-x))


def _layer_norm_rows(z, g, b):
    mu = jnp.mean(z, axis=-1, keepdims=True)
    d = z - mu
    var = jnp.mean(d * d, axis=-1, keepdims=True)
    return d * lax.rsqrt(var + LN_EPS) * g + b


def _mm_kernel(a_ref, w_ref, o_ref):
    o_ref[...] = jnp.dot(a_ref[...], w_ref[...], preferred_element_type=F32).astype(o_ref.dtype)


def _matmul(a, w, layer, tm, tn, out_dtype=F32):
    m, k = a.shape
    n = w.shape[2]
    tm = min(tm, m)
    return pl.pallas_call(
        _mm_kernel,
        grid=(n // tn, m // tm),
        in_specs=[pl.BlockSpec((tm, k), lambda j, i: (i, 0)),
                  pl.BlockSpec((None, k, tn), lambda j, i: (layer, 0, j))],
        out_specs=pl.BlockSpec((tm, tn), lambda j, i: (i, j)),
        out_shape=jax.ShapeDtypeStruct((m, n), out_dtype),
        compiler_params=_cparams(("parallel", "parallel")),
        name="in_proj_" + jnp.dtype(out_dtype).name,
    )(a, w)


PREP_TT = 128
V_MUP_R, V_MUP_K, V_MUP_V, V_MUN_R, V_MUN_K, V_MUN_V, V_KK, V_KA, V_RK, V_W0F, V_W0B, V_A0F, V_A0B = range(13)


def _shift(cur, prv8, nxt8, mup, mun, i, nt):
    tt = cur.shape[0]
    row = lax.broadcasted_iota(I32, cur.shape, 0)
    first = jnp.where(i > 0, prv8[7:8, :], 0.0)
    last = jnp.where(i < nt - 1, nxt8[0:1, :], 0.0)
    prev = jnp.where(row == 0, first, pltpu.roll(cur, 1, 0))
    nxt = jnp.where(row == tt - 1, last, pltpu.roll(cur, tt - 1, 0))
    return cur * (1.0 - mup - mun) + mup * prev + mun * nxt


def _prep_kernel(rc, kc, vc, lc, rp, kp, vp, lp, rn, kn, vn, ln_, vec_ref, lvec_ref, w2_ref, a2_ref, g2_ref,
                 hs_ref, he_ref, r_o, v_o, kk_o, lwf_o, kf_o, bf_o, lwb_o, kb_o, bb_o, bonus_o, g_o):
    i = pl.program_id(1)
    nt = pl.num_programs(1)
    vec = vec_ref[...]
    row = lambda j: vec[j:j + 1, :]
    r = _shift(rc[0], rp[0], rn[0], row(V_MUP_R), row(V_MUN_R), i, nt)
    k = _shift(kc[0], kp[0], kn[0], row(V_MUP_K), row(V_MUN_K), i, nt)
    v = _shift(vc[0], vp[0], vn[0], row(V_MUP_V), row(V_MUN_V), i, nt)
    lo = _shift(lc[0], lp[0], ln_[0], lvec_ref[0:1, :], lvec_ref[1:2, :], i, nt)

    hsum, hexp = hs_ref[...], he_ref[...]
    kk = k * row(V_KK)
    kk = kk / jnp.maximum(jnp.sqrt(_head_sums(kk * kk, hsum, hexp)), 1e-12)

    tanh_lo = jnp.tanh(lo)
    ksum = jnp.zeros_like(k)
    outs = ((lwf_o, kf_o, bf_o), (lwb_o, kb_o, bb_o))
    for d in range(2):
        z = row(V_W0F + d) + _dot(tanh_lo, w2_ref[d])
        nz = -z
        softplus = jnp.maximum(nz, 0.0) + jnp.log(1.0 + jnp.exp(-jnp.abs(nz)))
        w_log = -softplus - 0.5
        lw = -jnp.exp(w_log)
        a = _sigmoid(row(V_A0F + d) + _dot(lo, a2_ref[d]))
        k_d = k * (1.0 + (a - 1.0) * row(V_KA))
        ksum = ksum + k_d
        lw_o, kd_o, bd_o = outs[d]
        lw_o[0] = lw
        kd_o[0] = k_d
        bd_o[0] = kk * a
    r_o[0] = r
    v_o[0] = v
    kk_o[0] = kk
    bonus_o[0] = _head_sums(r * ksum * row(V_RK), hsum, hexp) * v
    g_o[0] = _dot(_sigmoid(lo), g2_ref[...])


def _rwkv_prep(proj3, vec, lvec, w2f, a2f, g2f, hsum, hexp):
    b, t, _ = proj3.shape
    tt = PREP_TT
    nt = t // tt
    h8 = tt // 8
    cur = lambda cb, w: pl.BlockSpec((1, tt, w), lambda bi, i: (bi, i, cb))
    prv = lambda cb, w: pl.BlockSpec((1, 8, w), lambda bi, i: (bi, jnp.maximum(i * h8 - 1, 0), cb))
    nxt = lambda cb, w: pl.BlockSpec((1, 8, w), lambda bi, i: (bi, jnp.minimum((i + 1) * h8, t // 8 - 1), cb))
    cols = [(COL_R // A_WIDTH, A_WIDTH), (COL_K // A_WIDTH, A_WIDTH), (COL_V // A_WIDTH, A_WIDTH),
            (COL_L // LORA_PAD, LORA_PAD)]
    full = lambda shape: pl.BlockSpec(shape, lambda bi, i: (0,) * len(shape))
    in_specs = ([cur(*c) for c in cols] + [prv(*c) for c in cols] + [nxt(*c) for c in cols]
                + [full(vec.shape), full(lvec.shape), full(w2f.shape), full(a2f.shape), full(g2f.shape),
                   full(hsum.shape), full(hexp.shape)])
    out_spec = pl.BlockSpec((1, tt, A_WIDTH), lambda bi, i: (bi, i, 0))
    out_sds = jax.ShapeDtypeStruct((b, t, A_WIDTH), F32)
    return pl.pallas_call(
        _prep_kernel,
        grid=(b, nt),
        in_specs=in_specs,
        out_specs=[out_spec] * 11,
        out_shape=[out_sds] * 11,
        compiler_params=_cparams(("parallel", "parallel")),
        name="rwkv_prep",
    )(*([proj3] * 12), vec, lvec, w2f, a2f, g2f, hsum, hexp)


def _stack2(x, m0):
    return jnp.concatenate([jnp.where(m0, x, 0.0), jnp.where(m0, 0.0, x)], axis=0)


class _Chain(NamedTuple):
    r: jax.Array
    v: jax.Array
    kk: jax.Array
    lw: jax.Array
    kd: jax.Array
    bd: jax.Array
    s_ref: object
    rev: bool


def _scan_masks(c, reverse):
    ti = lax.broadcasted_iota(I32, (c, c), 0)
    si = lax.broadcasted_iota(I32, (c, c), 1)
    tri = jnp.where((si >= ti) if reverse else (si <= ti), 1.0, 0.0).astype(BF16)
    t2 = lax.broadcasted_iota(I32, (c, 2 * c), 0)
    s2 = lax.broadcasted_iota(I32, (c, 2 * c), 1)
    s2 = jnp.where(s2 >= c, s2 - c, s2)
    strict = (s2 > t2) if reverse else (s2 < t2)
    incl = (s2 >= t2) if reverse else (s2 <= t2)
    eye2 = jnp.where(s2 == t2, 1.0, 0.0)
    return tri, strict, incl, eye2


def _scan_chunk(chains):
    c = SCAN_CHUNK
    masks = {rev: _scan_masks(c, rev) for rev in (False, True)}
    m0 = lax.broadcasted_iota(I32, (c, LANES), 1) < A_HEAD_DIM
    half = lax.broadcasted_iota(I32, (c, 2 * c), 1) < c
    vi = lax.broadcasted_iota(I32, (LANES, LANES), 0)
    ki = lax.broadcasted_iota(I32, (LANES, LANES), 1)
    same_head = (vi < A_HEAD_DIM) == (ki < A_HEAD_DIM)
    bf = lambda x: x.astype(BF16)
    mm = lambda a, b: jnp.dot(a, b, preferred_element_type=F32)
    mm_nt = lambda a, b: lax.dot_general(a, b, (((1,), (1,)), ((), ())), preferred_element_type=F32)
    cat = jnp.concatenate

    cum = []
    for ch in chains:
        tri = masks[ch.rev][0]
        parts = mm(tri, cat(_split3(ch.lw), axis=1))
        cum.append(parts[:, :LANES] + parts[:, LANES:2 * LANES] + parts[:, 2 * LANES:])

    ar, bk2, bkw, v2, etot = [], [], [], [], []
    for ch, cm in zip(chains, cum):
        tot = cm[0:1, :] if ch.rev else cm[c - 1:c, :]
        einv = jnp.exp(-cm)
        etail = jnp.exp(tot - cm)
        at = -ch.kk * jnp.exp(cm - ch.lw)
        rt = ch.r * jnp.exp(cm)
        ar.append(bf(cat([at, rt], axis=0)))
        bk2.append(bf(cat([_stack2(ch.bd * einv, m0), _stack2(ch.kd * einv, m0)], axis=0)))
        bkw.append(bf(cat([ch.bd * etail, ch.kd * etail], axis=0)))
        v2.append(bf(_stack2(ch.v, m0)))
        etot.append(jnp.exp(tot))

    s0 = [ch.s_ref[...] for ch in chains]
    both = [mm_nt(a, cat([b, bf(s)], axis=0)) for a, b, s in zip(ar, bk2, s0)]
    mt = [r[:, :4 * c] for r in both]
    ars = [r[:, 4 * c:] for r in both]
    a_ab, a_xk, a_rb = [], [], []
    for ch, m in zip(chains, mt):
        _, strict, incl, _ = masks[ch.rev]
        a_ab.append(jnp.where(strict, m[:c, :2 * c], 0.0))
        a_xk.append(bf(cat([jnp.where(strict, m[:c, 2 * c:], 0.0), jnp.where(incl, m[c:, 2 * c:], 0.0)], axis=0)))
        a_rb.append(bf(jnp.where(incl, m[c:, :2 * c], 0.0)))
    xkv = [mm(a, v) for a, v in zip(a_xk, v2)]

    tm = [masks[ch.rev][3] + a for ch, a in zip(chains, a_ab)]
    pw = [mm(bf(a), _stack2(bf(a), half)) for a in a_ab]
    steps = int(np.log2(c)) - 1
    for k in range(steps):
        blk = [_stack2(bf(x), half) for x in pw]
        if k < steps - 1:
            both = [mm(bf(cat([t, x], axis=0)), d) for t, x, d in zip(tm, pw, blk)]
            tm = [t + r[:c] for t, r in zip(tm, both)]
            pw = [r[c:] for r in both]
        else:
            tm = [t + mm(bf(t), d) for t, d in zip(tm, blk)]

    u = [mm(bf(t), bf(_stack2(a[:c] + k[:c], m0))) for t, a, k in zip(tm, ars, xkv)]
    y = [a[c:] + k[c:] + mm(ab, bf(_stack2(uu, m0))) for a, k, ab, uu in zip(ars, xkv, a_rb, u)]
    upd = [mm(bf(cat([uu, ch.v], axis=0).T), w) for uu, ch, w in zip(u, chains, bkw)]
    for ch, s, e, up in zip(chains, s0, etot, upd):
        ch.s_ref[...] = s * e + jnp.where(same_head, up, 0.0)
    return y


SCAN_HP = 8


def _scan_kernel(rf, vf, kkf, lwf, kf, bf, rb, vb, kkb, lwb, kb, bb, yf_o, yb_o, sf_ref, sb_ref):
    @pl.when(pl.program_id(2) == 0)
    def _():
        sf_ref[...] = jnp.zeros_like(sf_ref)
        sb_ref[...] = jnp.zeros_like(sb_ref)

    chains = []
    for hp in range(SCAN_HP):
        ln = slice(hp * LANES, (hp + 1) * LANES)
        chains.append(_Chain(rf[0, :, ln], vf[0, :, ln], kkf[0, :, ln], lwf[0, :, ln], kf[0, :, ln], bf[0, :, ln],
                             sf_ref.at[hp], False))
        chains.append(_Chain(rb[0, :, ln], vb[0, :, ln], kkb[0, :, ln], lwb[0, :, ln], kb[0, :, ln], bb[0, :, ln],
                             sb_ref.at[hp], True))
    y = _scan_chunk(chains)
    for hp in range(SCAN_HP):
        ln = slice(hp * LANES, (hp + 1) * LANES)
        yf_o[0, :, ln] = y[2 * hp]
        yb_o[0, :, ln] = y[2 * hp + 1]


def _rwkv_scan(r, v, kk, lwf, kf, bf, lwb, kb, bb):
    b, t, _ = r.shape
    c = SCAN_CHUNK
    nc = t // c
    w = SCAN_HP * LANES
    fwd = pl.BlockSpec((1, c, w), lambda bi, hp, ci: (bi, ci, hp))
    bwd = pl.BlockSpec((1, c, w), lambda bi, hp, ci: (bi, nc - 1 - ci, hp))
    out_sds = jax.ShapeDtypeStruct((b, t, A_WIDTH), F32)
    return pl.pallas_call(
        _scan_kernel,
        grid=(b, A_WIDTH // w, nc),
        in_specs=[fwd] * 6 + [bwd] * 6,
        out_specs=[fwd, bwd],
        out_shape=[out_sds, out_sds],
        scratch_shapes=[pltpu.VMEM((SCAN_HP, LANES, LANES), F32), pltpu.VMEM((SCAN_HP, LANES, LANES), F32)],
        compiler_params=_cparams(("parallel", "parallel", "arbitrary")),
        name="rwkv_scan",
    )(r, v, kk, lwf, kf, bf, r, v, kk, lwb, kb, bb)


POST_TT = 256


def _post_kernel(yf, yb, bonus, g, vec_ref, hs_ref, he_ref, o_ref):
    y = yf[...] + yb[...]
    hsum, hexp = hs_ref[...], he_ref[...]
    inv = 1.0 / A_HEAD_DIM
    mu = _head_sums(y, hsum, hexp) * inv
    d = y - mu
    var = _head_sums(d * d, hsum, hexp) * inv
    yn = d * lax.rsqrt(var + GN_EPS) * vec_ref[0:1, :] + vec_ref[1:2, :]
    o_ref[...] = ((yn + bonus[...]) * g[...]).astype(o_ref.dtype)


def _rwkv_post(yf, yb, bonus, g, gnvec, hsum, hexp):
    n = yf.shape[0]
    tt = min(POST_TT, n)
    blk = pl.BlockSpec((tt, A_WIDTH), lambda i: (i, 0))
    full = lambda a: pl.BlockSpec(a.shape, lambda i: (0,) * a.ndim, pipeline_mode=pl.Buffered(1))
    return pl.pallas_call(
        _post_kernel,
        grid=(n // tt,),
        in_specs=[blk, blk, blk, blk, full(gnvec), full(hsum), full(hexp)],
        out_specs=blk,
        out_shape=jax.ShapeDtypeStruct((n, A_WIDTH), BF16),
        compiler_params=_cparams(("parallel",)),
        name="rwkv_post",
    )(yf, yb, bonus, g, gnvec, hsum, hexp)


def _gelu_tanh(x):
    return 0.5 * x * (1.0 + jnp.tanh(np.sqrt(2.0 / np.pi).astype(np.float32) * (x + 0.044715 * (x * x * x))))


def _sg_kernel(pb_ref, lnv_ref, w_ref, bias_ref, o_ref):
    z = _gelu_tanh(pb_ref[0])
    u = z[:, :B_WIDTH]
    v = _layer_norm_rows(z[:, B_WIDTH:], lnv_ref[0:1, :], lnv_ref[1:2, :])
    lane = lax.broadcasted_iota(I32, (B_CHUNK, LANES), 1)
    m0 = lane < B_GROUP_DIM
    parts = []
    for q in range(B_WIDTH // LANES):
        vq = v[:, q * LANES:(q + 1) * LANES]
        parts.append(_dot(w_ref[q], _stack2(vq, m0)))
    mixed = jnp.concatenate(parts, axis=1) + bias_ref[...]
    o_ref[0] = (u * mixed).astype(o_ref.dtype)


def _spatial_gating(proj3, lnvec, w2, bias_full):
    b, t, _ = proj3.shape
    full = lambda a: pl.BlockSpec(a.shape, lambda bi, i: (0,) * a.ndim)
    return pl.pallas_call(
        _sg_kernel,
        grid=(b, t // B_CHUNK),
        in_specs=[pl.BlockSpec((1, B_CHUNK, B_COLS), lambda bi, i: (bi, i, COL_B // B_COLS)),
                  full(lnvec), full(w2), full(bias_full)],
        out_specs=pl.BlockSpec((1, B_CHUNK, B_WIDTH), lambda bi, i: (bi, i, 0)),
        out_shape=jax.ShapeDtypeStruct((b, t, B_WIDTH), BF16),
        compiler_params=_cparams(("parallel", "parallel")),
        name="spatial_gating",
    )(proj3, lnvec, w2, bias_full)


NA_WIN = C_WIN_ROWS * GRID_W


def _na_row_start(i, rows):
    return jnp.clip(i - C_WIN_ROWS // 2, 0, rows - C_WIN_ROWS)


NA_QR = 4


def _na_kernel(q_ref, k_ref, v_ref, *rest, rows):
    bias_refs, o_ref = rest[:NA_QR], rest[NA_QR]
    kh_ref, vh_ref = rest[NA_QR + 1:]
    i = pl.program_id(1)
    scale = C_HEAD_DIM ** -0.5

    @pl.when(i == 0)
    def _():
        for h in range(C_HEADS):
            sl = slice(h * C_HEAD_DIM, (h + 1) * C_HEAD_DIM)
            kh_ref[h] = k_ref[0, :, sl]
            vh_ref[h] = v_ref[0, :, sl]

    qs, ks, vs, bs = [], [], [], []
    for j in range(NA_QR):
        start = pl.multiple_of(_na_row_start(i * NA_QR + j, rows) * GRID_W, GRID_W)
        q = q_ref[0, j * GRID_W:(j + 1) * GRID_W, :]
        for h in range(C_HEADS):
            sl = slice(h * C_HEAD_DIM, (h + 1) * C_HEAD_DIM)
            qs.append(q[:, sl])
            ks.append(kh_ref[h, pl.ds(start, NA_WIN), :])
            vs.append(vh_ref[h, pl.ds(start, NA_WIN), :])
            bs.append(bias_refs[j][0, h])
    nt = lambda a, b: lax.dot_general(a, b, (((1,), (1,)), ((), ())), preferred_element_type=F32)
    ss = [nt(q, k) * scale + bias for q, k, bias in zip(qs, ks, bs)]
    ps = []
    for s in ss:
        e = jnp.exp(s - jnp.max(s, axis=-1, keepdims=True))
        ps.append((e / jnp.sum(e, axis=-1, keepdims=True)).astype(BF16))
    os_ = [jnp.dot(p, v, preferred_element_type=F32) for p, v in zip(ps, vs)]
    for j in range(NA_QR):
        o_ref[0, j * GRID_W:(j + 1) * GRID_W, :] = jnp.concatenate(
            os_[j * C_HEADS:(j + 1) * C_HEADS], axis=1).astype(o_ref.dtype)


def _neighbourhood_attention(proj3, bias_tab):
    b, t, _ = proj3.shape
    rows = t // GRID_W
    cq = COL_C // C_WIDTH
    qr = NA_QR
    seq = lambda cb: pl.BlockSpec((1, t, C_WIDTH), lambda bi, i: (bi, 0, cb))
    bias = lambda j: pl.BlockSpec((1, C_HEADS, GRID_W, NA_WIN),
                                  lambda bi, i: (i * qr + j - _na_row_start(i * qr + j, rows), 0, 0, 0))
    return pl.pallas_call(
        functools.partial(_na_kernel, rows=rows),
        grid=(b, rows // qr),
        in_specs=[pl.BlockSpec((1, qr * GRID_W, C_WIDTH), lambda bi, i: (bi, i, cq)), seq(cq + 1), seq(cq + 2)]
                 + [bias(j) for j in range(qr)],
        out_specs=pl.BlockSpec((1, qr * GRID_W, C_WIDTH), lambda bi, i: (bi, i, 0)),
        out_shape=jax.ShapeDtypeStruct((b, t, C_WIDTH), BF16),
        scratch_shapes=[pltpu.VMEM((C_HEADS, t, C_HEAD_DIM), BF16), pltpu.VMEM((C_HEADS, t, C_HEAD_DIM), BF16)],
        compiler_params=_cparams(("parallel", "arbitrary")),
        name="nbr_attention",
    )(proj3, proj3, proj3, *([bias_tab] * qr))


def _na_bias_table(rpb, rows):
    kc = C_WIN_COLS
    cols = np.arange(GRID_W)
    col_start = np.clip(cols - kc // 2, 0, GRID_W - kc)
    key_col = np.arange(GRID_W)
    in_win = (key_col[None, :] >= col_start[:, None]) & (key_col[None, :] < col_start[:, None] + kc)
    delta = np.arange(C_WIN_ROWS)
    row_off = np.arange(C_WIN_ROWS)[None, :] - delta[:, None] + (C_WIN_ROWS - 1)
    by_row = rpb[:, row_off]
    span = 2 * GRID_W
    lead = GRID_W - C_WIN_COLS
    padded = jnp.pad(by_row, [(0, 0)] * 3 + [(lead, span - lead - (2 * C_WIN_COLS - 1))])
    skew = jnp.tile(padded, GRID_W)[..., :GRID_W * (span - 1)].reshape(by_row.shape[:3] + (GRID_W, span - 1))
    bias = skew[..., GRID_W - 1:]
    bias = jnp.where(in_win, bias, -1e30)
    bias = jnp.transpose(bias, (1, 0, 3, 2, 4)).reshape(C_WIN_ROWS, C_HEADS, GRID_W, NA_WIN)
    return bias.astype(F32)


MERGE_TM = 512


def _merge_kernel(ya, yb, yc, ga, gb, gc, pa, pb, pc, o_ref):
    m = _sigmoid(ga[...].astype(F32)) * jnp.dot(ya[...], pa[...], preferred_element_type=F32)
    m = m + _sigmoid(gb[...].astype(F32)) * jnp.dot(yb[...], pb[...], preferred_element_type=F32)
    m = m + _sigmoid(gc[...].astype(F32)) * jnp.dot(yc[...], pc[...], preferred_element_type=F32)
    o_ref[...] = m.astype(o_ref.dtype)


def _merge(ya, yb, yc, proj, p_a, p_b, p_c):
    n = ya.shape[0]
    tm = min(MERGE_TM, n)
    rowblk = lambda w: pl.BlockSpec((tm, w), lambda i: (i, 0))
    gate = lambda j: pl.BlockSpec((tm, D_MODEL), lambda i: (i, COL_G // D_MODEL + j))
    full = lambda a: pl.BlockSpec(a.shape, lambda i: (0,) * a.ndim, pipeline_mode=pl.Buffered(1))
    return pl.pallas_call(
        _merge_kernel,
        grid=(n // tm,),
        in_specs=[rowblk(A_WIDTH), rowblk(B_WIDTH), rowblk(C_WIDTH), gate(0), gate(1), gate(2),
                  full(p_a), full(p_b), full(p_c)],
        out_specs=rowblk(D_MODEL),
        out_shape=jax.ShapeDtypeStruct((n, D_MODEL), BF16),
        compiler_params=_cparams(("parallel",)),
        name="branch_merge",
    )(ya, yb, yc, proj, proj, proj, p_a, p_b, p_c)


OUT_TM = 512


def _outln_kernel(m_ref, w_ref, x_ref, lnv_ref, o_ref, ob_ref):
    h = jnp.dot(m_ref[...], w_ref[...], preferred_element_type=F32)
    y = _layer_norm_rows(DEEPNORM_ALPHA * x_ref[...] + h, lnv_ref[0:1, :], lnv_ref[1:2, :])
    o_ref[...] = y
    ob_ref[...] = y.astype(BF16)


def _out_proj_ln(m, w_out, x, lnvec):
    n = m.shape[0]
    tm = min(OUT_TM, n)
    blk = pl.BlockSpec((tm, D_MODEL), lambda i: (i, 0))
    full = lambda a: pl.BlockSpec(a.shape, lambda i: (0,) * a.ndim, pipeline_mode=pl.Buffered(1))
    return pl.pallas_call(
        _outln_kernel,
        grid=(n // tm,),
        in_specs=[blk, full(w_out), blk, full(lnvec)],
        out_specs=[blk, blk],
        out_shape=[jax.ShapeDtypeStruct((n, D_MODEL), F32), jax.ShapeDtypeStruct((n, D_MODEL), BF16)],
        compiler_params=_cparams(("parallel",)),
        name="out_proj_ln",
    )(m, w_out, x, lnvec)


ROUTER_TM = 512


def _router_kernel(x_ref, wt_ref, o_ref):
    xh, xl = _split2(x_ref[...])
    wh, wl = _split2(wt_ref[...])
    nt = lambda a, b: lax.dot_general(a, b, (((1,), (1,)), ((), ())), preferred_element_type=F32)
    logits = nt(wh, xh) + nt(wl, xh) + nt(wh, xl)
    logits = logits - jnp.max(logits, axis=0, keepdims=True)
    e = jnp.exp(logits)
    o_ref[...] = e / jnp.sum(e, axis=0, keepdims=True)


def _router(x, w_router_t):
    n = x.shape[0]
    tm = min(ROUTER_TM, n)
    return pl.pallas_call(
        _router_kernel,
        grid=(n // tm,),
        in_specs=[pl.BlockSpec((tm, D_MODEL), lambda i: (i, 0)),
                  pl.BlockSpec(w_router_t.shape, lambda i: (0, 0))],
        out_specs=pl.BlockSpec((N_EXPERTS, tm), lambda i: (0, i)),
        out_shape=jax.ShapeDtypeStruct((N_EXPERTS, n), F32),
        compiler_params=_cparams(("parallel",)),
        name="router",
    )(x, w_router_t)


def _select_kernel(aff_ref, pos_ref, off_ref, *, cap):
    aff = aff_ref[...]
    e_, g_, l_ = aff.shape
    n = g_ * l_
    bits = pltpu.bitcast(aff, I32)

    def count(mask):
        c = jnp.sum(jnp.where(mask, 1.0, 0.0), axis=2, keepdims=True)
        return jnp.sum(c, axis=1, keepdims=True)

    def thr_body(_, carry):
        lo, hi = carry
        mid = lo + (hi - lo + 1) // 2
        ok = count(bits >= mid) >= cap
        return jnp.where(ok, mid, lo), jnp.where(ok, hi, mid - 1)

    lo0 = jnp.zeros((e_, 1, 1), I32)
    hi0 = jnp.full((e_, 1, 1), 0x7F800000, I32)
    thr, _ = lax.fori_loop(0, 32, thr_body, (lo0, hi0))
    gt = bits > thr
    tie = bits == thr
    need = cap - count(gt)
    idx = lax.broadcasted_iota(I32, aff.shape, 1) * l_ + lax.broadcasted_iota(I32, aff.shape, 2)

    def idx_body(_, carry):
        lo, hi = carry
        mid = (lo + hi) // 2
        ok = count(tie & (idx <= mid)) >= need
        return jnp.where(ok, lo, mid + 1), jnp.where(ok, mid, hi)

    lo1 = jnp.zeros((e_, 1, 1), I32)
    hi1 = jnp.full((e_, 1, 1), n - 1, I32)
    cut, _ = lax.fori_loop(0, int(np.ceil(np.log2(n))) + 1, idx_body, (lo1, hi1))
    sel = jnp.where(gt | (tie & (idx <= cut)), 1.0, 0.0)

    sel2 = sel.reshape(e_ * g_, l_)
    ls = lax.broadcasted_iota(I32, (l_, l_), 0)
    lt = lax.broadcasted_iota(I32, (l_, l_), 1)
    incl = jnp.dot(sel2.astype(BF16), jnp.where(ls <= lt, 1.0, 0.0).astype(BF16), preferred_element_type=F32)
    tot = jnp.dot(sel2.astype(BF16), jnp.ones((l_, l_), BF16), preferred_element_type=F32)
    gs = lax.broadcasted_iota(I32, (g_, g_), 0)
    gt_ = lax.broadcasted_iota(I32, (g_, g_), 1)
    lower = jnp.where(gt_ < gs, 1.0, 0.0).astype(BF16)
    offs = [jnp.dot(lower, tot[e * g_:(e + 1) * g_].astype(BF16), preferred_element_type=F32)
            for e in range(e_)]
    off = jnp.concatenate(offs, axis=0)
    pos = incl + off - sel2
    pos_ref[...] = jnp.where(sel2 > 0.5, pos, -1.0).astype(I32).reshape(e_, g_, l_)
    off_ref[...] = off.astype(I32).reshape(e_, g_, l_)


def _select(aff3, cap):
    full = pl.BlockSpec(aff3.shape, lambda i: (0, 0, 0))
    sds = jax.ShapeDtypeStruct(aff3.shape, I32)
    return pl.pallas_call(
        functools.partial(_select_kernel, cap=cap),
        grid=(1,),
        in_specs=[full],
        out_specs=[full, full],
        out_shape=[sds, sds],
        compiler_params=_cparams(("arbitrary",)),
        name="expert_select",
    )(aff3)


def _gather_kernel(tile_s, valid_s, e_g, slab_g, first_g, *refs):
    k_ = GATHER_GROUP
    x_refs, pos_refs, aff_refs = refs[:k_], refs[k_:2 * k_], refs[2 * k_:3 * k_]
    o_ref, g_ref = refs[3 * k_:]
    g = pl.program_id(0)
    t = MOE_TILE

    @pl.when(first_g[g] == 1)
    def _():
        o_ref[...] = jnp.zeros_like(o_ref)
        g_ref[...] = jnp.zeros_like(g_ref)

    nvalid = valid_s[g * k_]
    for k in range(1, k_):
        nvalid = nvalid + valid_s[g * k_ + k]

    @pl.when(nvalid > 0)
    def _():
        rank = lax.broadcasted_iota(I32, (t, GATHER_TOK), 0) + slab_g[g] * t
        got = o_ref[0].astype(F32)
        gate = g_ref[0]
        per_dot = MXU_DIM // GATHER_TOK
        for k0 in range(0, k_, per_dot):
            onehots, xs = [], []
            for k in range(k0, k0 + per_dot):
                want = jnp.where(valid_s[g * k_ + k] == 1, rank, -2)
                hit = pos_refs[k][0] == want
                onehots.append(jnp.where(hit, 1.0, 0.0).astype(BF16))
                xs.append(x_refs[k][...])
                gate = gate + jnp.sum(jnp.where(hit, aff_refs[k][0], 0.0), axis=1, keepdims=True)
            got = got + jnp.dot(jnp.concatenate(onehots, axis=1), jnp.concatenate(xs, axis=0),
                                preferred_element_type=F32)
        o_ref[0] = got.astype(o_ref.dtype)
        g_ref[0] = gate


def _moe_gather(x_bf, pos_en, aff_en, items, cap):
    n = x_bf.shape[0]
    t = MOE_TILE
    k_ = GATHER_GROUP
    ng = items[2].shape[0]
    tk = GATHER_TOK
    xs = [pl.BlockSpec((tk, D_MODEL), lambda g, tl, va, e, s, fi, k=k: (tl[g * k_ + k], 0)) for k in range(k_)]
    rows = [pl.BlockSpec((1, 1, tk), lambda g, tl, va, e, s, fi, k=k: (e[g], 0, tl[g * k_ + k])) for k in range(k_)]
    grid_spec = pltpu.PrefetchScalarGridSpec(
        num_scalar_prefetch=5,
        grid=(ng,),
        in_specs=xs + rows + rows,
        out_specs=[pl.BlockSpec((1, t, D_MODEL), lambda g, tl, va, e, s, fi: (e[g], s[g], 0)),
                   pl.BlockSpec((1, t, 1), lambda g, tl, va, e, s, fi: (e[g], s[g], 0))],
    )
    pos3 = pos_en.reshape(N_EXPERTS, 1, n)
    aff3 = aff_en.reshape(N_EXPERTS, 1, n)
    return pl.pallas_call(
        _gather_kernel,
        grid_spec=grid_spec,
        out_shape=[jax.ShapeDtypeStruct((N_EXPERTS, cap, D_MODEL), BF16),
                   jax.ShapeDtypeStruct((N_EXPERTS, cap, 1), F32)],
        compiler_params=_cparams(("arbitrary",)),
        name="moe_gather",
    )(*items, *([x_bf] * k_), *([pos3] * k_), *([aff3] * k_))


FFN_TM = 512


def _ffn_kernel(x_ref, gate_ref, wg_ref, wu_ref, wd_ref, o_ref):
    x = x_ref[0]
    g = jnp.dot(x, wg_ref[0], preferred_element_type=F32)
    u = jnp.dot(x, wu_ref[0], preferred_element_type=F32)
    h = (g * _sigmoid(g)) * u
    y = jnp.dot(h.astype(BF16), wd_ref[0], preferred_element_type=F32) * gate_ref[0]
    o_ref[0] = y.astype(o_ref.dtype)


def _expert_ffn(xe, gate, wg, wu, wd, layer):
    e_, cap, _ = xe.shape
    tm = min(FFN_TM, cap)
    wspec = lambda a: pl.BlockSpec((None, 1) + a.shape[2:], lambda e, i: (layer, e, 0, 0))
    rows = lambda w: pl.BlockSpec((1, tm, w), lambda e, i: (e, i, 0))
    sds = jax.ShapeDtypeStruct((e_, cap, D_MODEL), BF16)
    return pl.pallas_call(
        _ffn_kernel,
        grid=(e_, cap // tm),
        in_specs=[rows(D_MODEL), rows(1), wspec(wg), wspec(wu), wspec(wd)],
        out_specs=rows(D_MODEL),
        out_shape=sds,
        compiler_params=_cparams(("parallel", "arbitrary")),
        name="expert_ffn",
    )(xe, gate, wg, wu, wd)


def _combine_kernel(e_s, slab_s, valid_s, tile_g, first_g, last_g, *refs):
    k_ = COMBINE_GROUP
    ye_refs = refs[:k_]
    pos_ref, x_ref, lnv_ref, o_ref, ob_ref, acc_ref = refs[k_:]
    g = pl.program_id(0)
    t = MOE_TILE

    @pl.when(first_g[g] == 1)
    def _():
        acc_ref[...] = jnp.zeros_like(acc_ref)

    nvalid = valid_s[g * k_]
    for k in range(1, k_):
        nvalid = nvalid + valid_s[g * k_ + k]

    @pl.when(nvalid > 0)
    def _():
        w = k_ * MOE_SUB
        p_hi, p_lo = _split2(pos_ref[...].astype(F32))
        expert = lax.broadcasted_iota(I32, (N_EXPERTS, w), 0)
        col_e = lax.broadcasted_iota(I32, (N_EXPERTS, w), 1)
        col = lax.broadcasted_iota(I32, (1, w), 1)
        pick = jnp.zeros((N_EXPERTS, w), F32)
        want = jnp.zeros((1, w), I32)
        for k in range(k_):
            s = g * k_ + k
            in_slot = (col >= k * MOE_SUB) & (col < (k + 1) * MOE_SUB)
            in_slot_e = (col_e >= k * MOE_SUB) & (col_e < (k + 1) * MOE_SUB)
            pick = jnp.where(in_slot_e & (expert == e_s[s]), 1.0, pick)
            base = jnp.where(valid_s[s] == 1, slab_s[s] * MOE_SUB, -2 * MOE_SUB)
            want = jnp.where(in_slot, col - k * MOE_SUB + base, want)
        pick = pick.astype(BF16)
        pos = jnp.dot(p_hi, pick, preferred_element_type=F32) + jnp.dot(p_lo, pick, preferred_element_type=F32)
        onehot = jnp.where(pos == want.astype(F32), 1.0, 0.0).astype(BF16)
        ye = jnp.concatenate([r[0] for r in ye_refs], axis=0)
        acc_ref[...] += jnp.dot(onehot, ye, preferred_element_type=F32)

    @pl.when(last_g[g] == 1)
    def _():
        y = _layer_norm_rows(DEEPNORM_ALPHA * x_ref[...] + acc_ref[...], lnv_ref[0:1, :], lnv_ref[1:2, :])
        o_ref[...] = y
        ob_ref[...] = y.astype(BF16)


def _moe_combine(ye, pos_ne, x, lnvec, items):
    n = x.shape[0]
    t = MOE_TILE
    k_ = COMBINE_GROUP
    ng = items[3].shape[0]
    tile_blk = lambda w: pl.BlockSpec((t, w), lambda g, e, s, va, tl, fi, la: (tl[g], 0))
    slabs = [pl.BlockSpec((1, MOE_SUB, D_MODEL), lambda g, e, s, va, tl, fi, la, k=k: (e[g * k_ + k], s[g * k_ + k], 0))
             for k in range(k_)]
    grid_spec = pltpu.PrefetchScalarGridSpec(
        num_scalar_prefetch=6,
        grid=(ng,),
        in_specs=slabs + [tile_blk(N_EXPERTS), tile_blk(D_MODEL),
                          pl.BlockSpec(lnvec.shape, lambda g, e, s, va, tl, fi, la: (0, 0))],
        out_specs=[tile_blk(D_MODEL), tile_blk(D_MODEL)],
        scratch_shapes=[pltpu.VMEM((t, D_MODEL), F32)],
    )
    return pl.pallas_call(
        _combine_kernel,
        grid_spec=grid_spec,
        out_shape=[jax.ShapeDtypeStruct((n, D_MODEL), F32), jax.ShapeDtypeStruct((n, D_MODEL), BF16)],
        compiler_params=_cparams(("arbitrary",)),
        name="moe_combine",
    )(*items, *([ye] * k_), pos_ne, x, lnvec)


def _moe_items(group_off, n, cap):
    e_ = N_EXPERTS
    big = jnp.int32(2 ** 30)

    def build(rows, by_tile, k_, t):
        ns = cap // rows
        nt = n // t
        starts = group_off[:, ::t // LANES, 0]
        ends = jnp.concatenate([starts[:, 1:], jnp.full((e_, 1), cap, I32)], axis=1)
        e_p = jnp.broadcast_to(jnp.arange(e_, dtype=I32)[:, None], (e_, nt))
        tile_p = jnp.broadcast_to(jnp.arange(nt, dtype=I32)[None, :], (e_, nt))
        slab_p = jnp.minimum(starts // rows, ns - 1)
        valid_p = (ends > starts).astype(I32)
        bound = jnp.arange(ns, dtype=I32) * rows
        tile_s = jnp.sum((starts[:, :, None] <= bound[None, None, :]).astype(I32), axis=1) - 1
        tile_s = jnp.clip(tile_s, 0, nt - 1)
        at_tile = tile_s[:, :, None] == jnp.arange(nt, dtype=I32)[None, None, :]
        st_s = jnp.sum(jnp.where(at_tile, starts[:, None, :], 0), axis=2)
        en_s = jnp.sum(jnp.where(at_tile, ends[:, None, :], 0), axis=2)
        valid_s = ((st_s < bound[None, :]) & (bound[None, :] < en_s)).astype(I32)
        e_s = jnp.broadcast_to(jnp.arange(e_, dtype=I32)[:, None], (e_, ns))
        slab_s = jnp.broadcast_to(jnp.arange(ns, dtype=I32)[None, :], (e_, ns))
        cat = lambda a, b: jnp.concatenate([a.reshape(-1), b.reshape(-1)])
        e_a, slab_a, tile_a = cat(e_p, e_s), cat(slab_p, slab_s), cat(tile_p, tile_s)
        valid_a = cat(valid_p, valid_s)
        secondary = cat(jnp.zeros_like(e_p), jnp.ones_like(e_s))
        ni = e_a.shape[0]
        ar = jnp.arange(ni, dtype=I32)
        if by_tile:
            keep = jnp.maximum(valid_a, 1 - secondary)
            key = ((tile_a * e_ + e_a) * 2 + secondary) * ns + slab_a
        else:
            keep = valid_a
            key = (e_a * ns + slab_a) * nt + tile_a
        key = jnp.where(keep == 1, key, big + ar)
        rank = jnp.sum((key[None, :] < key[:, None]).astype(I32), axis=1)
        code = ((e_a * ns + slab_a) * nt + tile_a) * 2 + valid_a
        decode = lambda cd: (cd // (2 * nt * ns), (cd // (2 * nt)) % ns, (cd // 2) % nt, cd % 2)
        ocode = jnp.sum(jnp.where(rank[None, :] == ar[:, None], code[None, :], 0), axis=1)
        nkeep = jnp.sum(keep)
        kept = ar < nkeep
        oe, os_, otl, _ = decode(ocode)
        blk = otl if by_tile else oe * ns + os_

        nblk = nt if by_tile else e_ * ns
        ng = -(-ni // k_) + nblk
        first = jnp.concatenate([jnp.ones((1,), bool), blk[1:] != blk[:-1]])
        run_start = lax.cummax(jnp.where(first, ar, 0))
        sub = (ar - run_start) % k_
        leads = (sub == 0) & kept
        gid = jnp.cumsum(leads.astype(I32)) - 1
        slot = jnp.where(kept, gid * k_ + sub, -1)
        sar = jnp.arange(ng * k_, dtype=I32)
        hit = slot[None, :] == sar[:, None]
        has = jnp.sum(hit.astype(I32), axis=1)
        scode = jnp.sum(jnp.where(hit, ocode[None, :], 0), axis=1)
        n_real = jnp.sum(leads.astype(I32))
        gar = jnp.arange(ng, dtype=I32)
        lead = scode.reshape(ng, k_)[:, 0]
        lead = jnp.where(gar < n_real, lead, jnp.sum(jnp.where(gar == n_real - 1, lead, 0)))
        scode = jnp.where(has == 1, scode, jnp.repeat(lead, k_))
        se, ss, stl, sv = decode(scode)
        ge, gsl, gtl, _ = decode(lead)
        gblk = gtl if by_tile else ge * ns + gsl
        change = (gblk[1:] != gblk[:-1]).astype(I32)
        one = jnp.ones((1,), I32)
        slots = dict(e=se, slab=ss, tile=stl, valid=sv * has)
        groups = dict(e=ge, slab=gsl, tile=gtl, first=jnp.concatenate([one, change]),
                      last=jnp.concatenate([change, one]))
        return slots, groups

    gs, gg = build(MOE_TILE, False, GATHER_GROUP, GATHER_TOK)
    cs, cg = build(MOE_SUB, True, COMBINE_GROUP, MOE_TILE)
    return ((gs["tile"], gs["valid"], gg["e"], gg["slab"], gg["first"]),
            (cs["e"], cs["slab"], cs["valid"], cg["tile"], cg["first"], cg["last"]))


def _expert_choice_moe_ln(x, x_bf, w_router_t, wg, wu, wd, layer, lnvec):
    n = x.shape[0]
    cap = EC_CAPACITY_FACTOR * n // N_EXPERTS
    aff_en = _router(x, w_router_t)
    pos3, off3 = _select(aff_en.reshape(N_EXPERTS, n // LANES, LANES), cap)
    pos_en = pos3.reshape(N_EXPERTS, n)
    g_items, c_items = _moe_items(off3, n, cap)
    xe, gate = _moe_gather(x_bf, pos_en, aff_en, g_items, cap)
    ye = _expert_ffn(xe, gate, wg, wu, wd, layer)
    return _moe_combine(ye, pos_en.T, x, lnvec, c_items)


def _pack_cols(w):
    a_end = A_COLS
    b_end = a_end + B_COLS
    c_end = b_end + C_COLS
    pad = jnp.zeros(w.shape[:-1] + (LORA_PAD - LORA_COLS,), w.dtype)
    return (jnp.concatenate([w[..., :3 * A_WIDTH], w[..., a_end:b_end], w[..., 3 * A_WIDTH:a_end], pad], axis=-1),
            jnp.concatenate([w[..., c_end:], w[..., b_end:c_end]], axis=-1))


def _lora_rows(w, start):
    k = w.shape[-2]
    return jnp.pad(w, [(0, 0)] * (w.ndim - 2) + [(start, LORA_PAD - start - k), (0, 0)])


def _prepare(w_in, mu_prev, mu_next, decay_w0, decay_w2, iclr_a0, iclr_a2, gate_g2, k_k, k_a, r_k, gn_g, gn_b,
             sg_ln_g, sg_ln_b, sg_w, sg_b, rpb, p_a, p_b, p_c, w_out, ln_mix_g, ln_mix_b, w_router, e_gate, e_up,
             e_down, ln_ffn_g, ln_ffn_b, rows):
    l_ = w_in.shape[0]
    pad_a = lambda m: jnp.pad(m, ((0, 0), (0, LORA_PAD - LORA_COLS)))
    mup, mun = mu_prev, mu_next
    vec_rows = [mup[:, :A_WIDTH], mup[:, A_WIDTH:2 * A_WIDTH], mup[:, 2 * A_WIDTH:3 * A_WIDTH],
                mun[:, :A_WIDTH], mun[:, A_WIDTH:2 * A_WIDTH], mun[:, 2 * A_WIDTH:3 * A_WIDTH],
                k_k, k_a, r_k.reshape(l_, A_WIDTH), decay_w0[:, 0], decay_w0[:, 1], iclr_a0[:, 0], iclr_a0[:, 1]]
    vec = jnp.stack(vec_rows + [jnp.zeros_like(k_k)] * (16 - len(vec_rows)), axis=1)
    lvec = jnp.stack([pad_a(mup[:, 3 * A_WIDTH:]), pad_a(mun[:, 3 * A_WIDTH:])]
                     + [jnp.zeros((l_, LORA_PAD), F32)] * 6, axis=1)
    w2f = jnp.stack([_lora_rows(decay_w2[:, 0], 0), _lora_rows(decay_w2[:, 1], A_DECAY_LORA)], axis=1)
    a2f = jnp.stack([_lora_rows(iclr_a2[:, 0], 2 * A_DECAY_LORA),
                     _lora_rows(iclr_a2[:, 1], 2 * A_DECAY_LORA + A_ICLR_LORA)], axis=1)
    g2f = _lora_rows(gate_g2, 2 * A_DECAY_LORA + 2 * A_ICLR_LORA)
    head = np.arange(A_WIDTH) // A_HEAD_DIM
    hsum = jnp.asarray(head[:, None] == np.arange(LANES)[None, :], BF16)
    sgw2 = sg_w.reshape(l_, B_GROUPS // 2, 2, B_CHUNK, B_CHUNK).transpose(0, 1, 3, 2, 4)
    sgw2 = sgw2.reshape(l_, B_GROUPS // 2, B_CHUNK, 2 * B_CHUNK)
    sg_bias = jnp.repeat(jnp.swapaxes(sg_b, 1, 2), B_GROUP_DIM, axis=2)
    return dict(
        w_in=_pack_cols(w_in.astype(BF16)), vec=vec, lvec=lvec, w2f=w2f.astype(BF16), a2f=a2f.astype(BF16),
        g2f=g2f.astype(BF16), hsum=hsum, hexp=hsum.T, gnvec=jnp.stack([gn_g, gn_b], axis=1),
        sg_ln=jnp.stack([sg_ln_g, sg_ln_b], axis=1), sgw2=sgw2.astype(BF16), sg_bias=sg_bias,
        na_bias=jnp.stack([_na_bias_table(rpb[l], rows) for l in range(l_)]),
        p_a=p_a.astype(BF16), p_b=p_b.astype(BF16), p_c=p_c.astype(BF16), w_out=w_out.astype(BF16),
        ln_mix=jnp.stack([ln_mix_g, ln_mix_b], axis=1), w_router_t=jnp.swapaxes(w_router, 1, 2),
        e_gate=e_gate.astype(BF16), e_up=e_up.astype(BF16), e_down=e_down.astype(BF16),
        ln_ffn=jnp.stack([ln_ffn_g, ln_ffn_b], axis=1))


def _mixer(x, x_bf, p, l, b, t):
    n = b * t
    w_f, w_h = p["w_in"]
    proj = _matmul(x_bf, w_f, l, 1024, IN_TN)
    proj_h = _matmul(x_bf, w_h, l, 1024, IN_TN, BF16)
    proj3 = proj.reshape(b, t, F_COLS)
    proj3_h = proj_h.reshape(b, t, H_COLS)
    r, v, kk, lwf, kf, bf, lwb, kb, bb, bonus, g = _rwkv_prep(
        proj3, p["vec"][l], p["lvec"][l], p["w2f"][l], p["a2f"][l], p["g2f"][l], p["hsum"], p["hexp"])
    yf, yb = _rwkv_scan(r, v, kk, lwf, kf, bf, lwb, kb, bb)
    flat = lambda a: a.reshape(n, a.shape[-1])
    ya = _rwkv_post(flat(yf), flat(yb), flat(bonus), flat(g), p["gnvec"][l], p["hsum"], p["hexp"])
    ybr = _spatial_gating(proj3, p["sg_ln"][l], p["sgw2"][l], p["sg_bias"][l])
    ycr = _neighbourhood_attention(proj3_h, p["na_bias"][l])
    m = _merge(ya, flat(ybr), flat(ycr), proj_h, p["p_a"][l], p["p_b"][l], p["p_c"][l])
    return _out_proj_ln(m, p["w_out"][l], x, p["ln_mix"][l])


def _trunk(x3, p):
    b, t, _ = x3.shape
    x = x3.reshape(b * t, D_MODEL)
    x_bf = x.astype(BF16)
    for l in range(DEPTH):
        x, x_bf = _mixer(x, x_bf, p, l, b, t)
        x, x_bf = _expert_choice_moe_ln(x, x_bf, p["w_router_t"][l], p["e_gate"], p["e_up"], p["e_down"], l,
                                        p["ln_ffn"][l])
    return x.reshape(b, t, D_MODEL)


def kernel(x_prompt, x_sample, w_in, mu_prev, mu_next, decay_w0, decay_w2, iclr_a0, iclr_a2, gate_g2, k_k, k_a, r_k, gn_g, gn_b, sg_ln_g, sg_ln_b, sg_w, sg_b, rpb, p_a, p_b, p_c, w_out, ln_mix_g, ln_mix_b, w_router, e_gate, e_up, e_down, ln_ffn_g, ln_ffn_b):
    assert x_prompt.shape[1] == x_sample.shape[1]
    rows = x_prompt.shape[1] // GRID_W
    p = _prepare(w_in, mu_prev, mu_next, decay_w0, decay_w2, iclr_a0, iclr_a2, gate_g2, k_k, k_a, r_k, gn_g, gn_b,
                 sg_ln_g, sg_ln_b, sg_w, sg_b, rpb, p_a, p_b, p_c, w_out, ln_mix_g, ln_mix_b, w_router, e_gate,
                 e_up, e_down, ln_ffn_g, ln_ffn_b, rows)
    return (_trunk(x_prompt, p), _trunk(x_sample, p))
```

```python
import functools
from typing import NamedTuple

import numpy as np
import jax
import jax.numpy as jnp
from jax import lax
from jax.experimental import pallas as pl
from jax.experimental.pallas import tpu as pltpu

F32 = jnp.float32
BF16 = jnp.bfloat16
I32 = jnp.int32

D_MODEL = 2048
DEPTH = 4
GRID_W = 64
A_HEADS = 16
A_HEAD_DIM = 64
A_WIDTH = A_HEADS * A_HEAD_DIM
A_DECAY_LORA = 64
A_ICLR_LORA = 64
A_GATE_LORA = 160
B_GROUPS = 8
B_GROUP_DIM = 64
B_WIDTH = B_GROUPS * B_GROUP_DIM
B_CHUNK = 128
C_HEADS = 8
C_HEAD_DIM = 64
C_WIDTH = C_HEADS * C_HEAD_DIM
C_WIN_ROWS = 8
C_WIN_COLS = 16
N_EXPERTS = 16
EXPERT_HIDDEN = 1024
EC_CAPACITY_FACTOR = 2
A_COLS = 3 * A_WIDTH + 2 * A_DECAY_LORA + 2 * A_ICLR_LORA + A_GATE_LORA
B_COLS = 2 * B_WIDTH
C_COLS = 3 * C_WIDTH
G_COLS = 3 * D_MODEL
DEEPNORM_ALPHA = (2 * DEPTH) ** 0.25
LN_EPS = 1e-5
GN_EPS = 64e-5

LORA_COLS = 2 * A_DECAY_LORA + 2 * A_ICLR_LORA + A_GATE_LORA
LORA_PAD = 512
COL_R, COL_K, COL_V = 0, A_WIDTH, 2 * A_WIDTH
COL_B = 3 * A_WIDTH
COL_L = COL_B + B_COLS
F_COLS = COL_L + LORA_PAD
COL_G = 0
COL_C = G_COLS
H_COLS = COL_C + C_COLS
IN_TN = 1536

LANES = 128
MXU_DIM = 256
SCAN_CHUNK = 64
MOE_TILE = 256
MOE_SUB = 64
GATHER_TOK = 256
GATHER_GROUP = 4
COMBINE_GROUP = 8
VMEM_LIMIT = 48 * 1024 * 1024


def _cparams(sem):
    return pltpu.CompilerParams(dimension_semantics=sem, vmem_limit_bytes=VMEM_LIMIT)


def _dot(a, b):
    return jnp.dot(a.astype(BF16), b.astype(BF16), preferred_element_type=F32)


def _split2(x):
    hi = x.astype(BF16)
    lo = (x - hi.astype(F32)).astype(BF16)
    return hi, lo


def _split3(x):
    hi = x.astype(BF16)
    r1 = x - hi.astype(F32)
    mid = r1.astype(BF16)
    lo = (r1 - mid.astype(F32)).astype(BF16)
    return hi, mid, lo


def _dot_x01(x, m01):
    hi, lo = _split2(x)
    m = m01.astype(BF16)
    return jnp.dot(hi, m, preferred_element_type=F32) + jnp.dot(lo, m, preferred_element_type=F32)


def _head_sums(x, hsum, hexp):
    return _dot_x01(_dot_x01(x, hsum), hexp)


def _sigmoid(x):
    return 1.0 / (1.0 + jnp.exp(-x))


def _layer_norm_rows(z, g, b):
    mu = jnp.mean(z, axis=-1, keepdims=True)
    d = z - mu
    var = jnp.mean(d * d, axis=-1, keepdims=True)
    return d * lax.rsqrt(var + LN_EPS) * g + b


def _mm_kernel(a_ref, w_ref, o_ref):
    o_ref[...] = jnp.dot(a_ref[...], w_ref[...], preferred_element_type=F32).astype(o_ref.dtype)


def _matmul(a, w, layer, tm, tn, out_dtype=F32):
    m, k = a.shape
    n = w.shape[2]
    tm = min(tm, m)
    return pl.pallas_call(
        _mm_kernel,
        grid=(n // tn, m // tm),
        in_specs=[pl.BlockSpec((tm, k), lambda j, i: (i, 0)),
                  pl.BlockSpec((None, k, tn), lambda j, i: (layer, 0, j))],
        out_specs=pl.BlockSpec((tm, tn), lambda j, i: (i, j)),
        out_shape=jax.ShapeDtypeStruct((m, n), out_dtype),
        compiler_params=_cparams(("parallel", "parallel")),
        name="in_proj_" + jnp.dtype(out_dtype).name,
    )(a, w)


PREP_TT = 128
V_MUP_R, V_MUP_K, V_MUP_V, V_MUN_R, V_MUN_K, V_MUN_V, V_KK, V_KA, V_RK, V_W0F, V_W0B, V_A0F, V_A0B = range(13)


def _shift(cur, prv8, nxt8, mup, mun, i, nt):
    tt = cur.shape[0]
    row = lax.broadcasted_iota(I32, cur.shape, 0)
    first = jnp.where(i > 0, prv8[7:8, :], 0.0)
    last = jnp.where(i < nt - 1, nxt8[0:1, :], 0.0)
    prev = jnp.where(row == 0, first, pltpu.roll(cur, 1, 0))
    nxt = jnp.where(row == tt - 1, last, pltpu.roll(cur, tt - 1, 0))
    return cur * (1.0 - mup - mun) + mup * prev + mun * nxt


def _prep_kernel(rc, kc, vc, lc, rp, kp, vp, lp, rn, kn, vn, ln_, vec_ref, lvec_ref, w2_ref, a2_ref, g2_ref,
                 hs_ref, he_ref, r_o, v_o, kk_o, lwf_o, kf_o, bf_o, lwb_o, kb_o, bb_o, bonus_o, g_o):
    i = pl.program_id(1)
    nt = pl.num_programs(1)
    vec = vec_ref[...]
    row = lambda j: vec[j:j + 1, :]
    r = _shift(rc[0], rp[0], rn[0], row(V_MUP_R), row(V_MUN_R), i, nt)
    k = _shift(kc[0], kp[0], kn[0], row(V_MUP_K), row(V_MUN_K), i, nt)
    v = _shift(vc[0], vp[0], vn[0], row(V_MUP_V), row(V_MUN_V), i, nt)
    lo = _shift(lc[0], lp[0], ln_[0], lvec_ref[0:1, :], lvec_ref[1:2, :], i, nt)

    hsum, hexp = hs_ref[...], he_ref[...]
    kk = k * row(V_KK)
    kk = kk / jnp.maximum(jnp.sqrt(_head_sums(kk * kk, hsum, hexp)), 1e-12)

    tanh_lo = jnp.tanh(lo)
    ksum = jnp.zeros_like(k)
    outs = ((lwf_o, kf_o, bf_o), (lwb_o, kb_o, bb_o))
    for d in range(2):
        z = row(V_W0F + d) + _dot(tanh_lo, w2_ref[d])
        nz = -z
        softplus = jnp.maximum(nz, 0.0) + jnp.log(1.0 + jnp.exp(-jnp.abs(nz)))
        w_log = -softplus - 0.5
        lw = -jnp.exp(w_log)
        a = _sigmoid(row(V_A0F + d) + _dot(lo, a2_ref[d]))
        k_d = k * (1.0 + (a - 1.0) * row(V_KA))
        ksum = ksum + k_d
        lw_o, kd_o, bd_o = outs[d]
        lw_o[0] = lw
        kd_o[0] = k_d
        bd_o[0] = kk * a
    r_o[0] = r
    v_o[0] = v
    kk_o[0] = kk
    bonus_o[0] = _head_sums(r * ksum * row(V_RK), hsum, hexp) * v
    g_o[0] = _dot(_sigmoid(lo), g2_ref[...])


def _rwkv_prep(proj3, vec, lvec, w2f, a2f, g2f, hsum, hexp):
    b, t, _ = proj3.shape
    tt = PREP_TT
    nt = t // tt
    h8 = tt // 8
    cur = lambda cb, w: pl.BlockSpec((1, tt, w), lambda bi, i: (bi, i, cb))
    prv = lambda cb, w: pl.BlockSpec((1, 8, w), lambda bi, i: (bi, jnp.maximum(i * h8 - 1, 0), cb))
    nxt = lambda cb, w: pl.BlockSpec((1, 8, w), lambda bi, i: (bi, jnp.minimum((i + 1) * h8, t // 8 - 1), cb))
    cols = [(COL_R // A_WIDTH, A_WIDTH), (COL_K // A_WIDTH, A_WIDTH), (COL_V // A_WIDTH, A_WIDTH),
            (COL_L // LORA_PAD, LORA_PAD)]
    full = lambda shape: pl.BlockSpec(shape, lambda bi, i: (0,) * len(shape))
    in_specs = ([cur(*c) for c in cols] + [prv(*c) for c in cols] + [nxt(*c) for c in cols]
                + [full(vec.shape), full(lvec.shape), full(w2f.shape), full(a2f.shape), full(g2f.shape),
                   full(hsum.shape), full(hexp.shape)])
    out_spec = pl.BlockSpec((1, tt, A_WIDTH), lambda bi, i: (bi, i, 0))
    out_sds = jax.ShapeDtypeStruct((b, t, A_WIDTH), F32)
    return pl.pallas_call(
        _prep_kernel,
        grid=(b, nt),
        in_specs=in_specs,
        out_specs=[out_spec] * 11,
        out_shape=[out_sds] * 11,
        compiler_params=_cparams(("parallel", "parallel")),
        name="rwkv_prep",
    )(*([proj3] * 12), vec, lvec, w2f, a2f, g2f, hsum, hexp)


def _stack2(x, m0):
    return jnp.concatenate([jnp.where(m0, x, 0.0), jnp.where(m0, 0.0, x)], axis=0)


class _Chain(NamedTuple):
    r: jax.Array
    v: jax.Array
    kk: jax.Array
    lw: jax.Array
    kd: jax.Array
    bd: jax.Array
    s_ref: object
    rev: bool


def _scan_masks(c, reverse):
    ti = lax.broadcasted_iota(I32, (c, c), 0)
    si = lax.broadcasted_iota(I32, (c, c), 1)
    tri = jnp.where((si >= ti) if reverse else (si <= ti), 1.0, 0.0).astype(BF16)
    t2 = lax.broadcasted_iota(I32, (c, 2 * c), 0)
    s2 = lax.broadcasted_iota(I32, (c, 2 * c), 1)
    s2 = jnp.where(s2 >= c, s2 - c, s2)
    strict = (s2 > t2) if reverse else (s2 < t2)
    incl = (s2 >= t2) if reverse else (s2 <= t2)
    eye2 = jnp.where(s2 == t2, 1.0, 0.0)
    return tri, strict, incl, eye2


def _scan_chunk(chains):
    c = SCAN_CHUNK
    masks = {rev: _scan_masks(c, rev) for rev in (False, True)}
    m0 = lax.broadcasted_iota(I32, (c, LANES), 1) < A_HEAD_DIM
    half = lax.broadcasted_iota(I32, (c, 2 * c), 1) < c
    vi = lax.broadcasted_iota(I32, (LANES, LANES), 0)
    ki = lax.broadcasted_iota(I32, (LANES, LANES), 1)
    same_head = (vi < A_HEAD_DIM) == (ki < A_HEAD_DIM)
    bf = lambda x: x.astype(BF16)
    mm = lambda a, b: jnp.dot(a, b, preferred_element_type=F32)
    mm_nt = lambda a, b: lax.dot_general(a, b, (((1,), (1,)), ((), ())), preferred_element_type=F32)
    cat = jnp.concatenate

    cum = []
    for ch in chains:
        tri = masks[ch.rev][0]
        parts = mm(tri, cat(_split3(ch.lw), axis=1))
        cum.append(parts[:, :LANES] + parts[:, LANES:2 * LANES] + parts[:, 2 * LANES:])

    ar, bk2, bkw, v2, etot = [], [], [], [], []
    for ch, cm in zip(chains, cum):
        tot = cm[0:1, :] if ch.rev else cm[c - 1:c, :]
        einv = jnp.exp(-cm)
        etail = jnp.exp(tot - cm)
        at = -ch.kk * jnp.exp(cm - ch.lw)
        rt = ch.r * jnp.exp(cm)
        ar.append(bf(cat([at, rt], axis=0)))
        bk2.append(bf(cat([_stack2(ch.bd * einv, m0), _stack2(ch.kd * einv, m0)], axis=0)))
        bkw.append(bf(cat([ch.bd * etail, ch.kd * etail], axis=0)))
        v2.append(bf(_stack2(ch.v, m0)))
        etot.append(jnp.exp(tot))

    s0 = [ch.s_ref[...] for ch in chains]
    both = [mm_nt(a, cat([b, bf(s)], axis=0)) for a, b, s in zip(ar, bk2, s0)]
    mt = [r[:, :4 * c] for r in both]
    ars = [r[:, 4 * c:] for r in both]
    a_ab, a_xk, a_rb = [], [], []
    for ch, m in zip(chains, mt):
        _, strict, incl, _ = masks[ch.rev]
        a_ab.append(jnp.where(strict, m[:c, :2 * c], 0.0))
        a_xk.append(bf(cat([jnp.where(strict, m[:c, 2 * c:], 0.0), jnp.where(incl, m[c:, 2 * c:], 0.0)], axis=0)))
        a_rb.append(bf(jnp.where(incl, m[c:, :2 * c], 0.0)))
    xkv = [mm(a, v) for a, v in zip(a_xk, v2)]

    tm = [masks[ch.rev][3] + a for ch, a in zip(chains, a_ab)]
    pw = [mm(bf(a), _stack2(bf(a), half)) for a in a_ab]
    steps = int(np.log2(c)) - 1
    for k in range(steps):
        blk = [_stack2(bf(x), half) for x in pw]
        if k < steps - 1:
            both = [mm(bf(cat([t, x], axis=0)), d) for t, x, d in zip(tm, pw, blk)]
            tm = [t + r[:c] for t, r in zip(tm, both)]
            pw = [r[c:] for r in both]
        else:
            tm = [t + mm(bf(t), d) for t, d in zip(tm, blk)]

    u = [mm(bf(t), bf(_stack2(a[:c] + k[:c], m0))) for t, a, k in zip(tm, ars, xkv)]
    y = [a[c:] + k[c:] + mm(ab, bf(_stack2(uu, m0))) for a, k, ab, uu in zip(ars, xkv, a_rb, u)]
    upd = [mm(bf(cat([uu, ch.v], axis=0).T), w) for uu, ch, w in zip(u, chains, bkw)]
    for ch, s, e, up in zip(chains, s0, etot, upd):
        ch.s_ref[...] = s * e + jnp.where(same_head, up, 0.0)
    return y


SCAN_HP = 8


def _scan_kernel(rf, vf, kkf, lwf, kf, bf, rb, vb, kkb, lwb, kb, bb, yf_o, yb_o, sf_ref, sb_ref):
    @pl.when(pl.program_id(2) == 0)
    def _():
        sf_ref[...] = jnp.zeros_like(sf_ref)
        sb_ref[...] = jnp.zeros_like(sb_ref)

    chains = []
    for hp in range(SCAN_HP):
        ln = slice(hp * LANES, (hp + 1) * LANES)
        chains.append(_Chain(rf[0, :, ln], vf[0, :, ln], kkf[0, :, ln], lwf[0, :, ln], kf[0, :, ln], bf[0, :, ln],
                             sf_ref.at[hp], False))
        chains.append(_Chain(rb[0, :, ln], vb[0, :, ln], kkb[0, :, ln], lwb[0, :, ln], kb[0, :, ln], bb[0, :, ln],
                             sb_ref.at[hp], True))
    y = _scan_chunk(chains)
    for hp in range(SCAN_HP):
        ln = slice(hp * LANES, (hp + 1) * LANES)
        yf_o[0, :, ln] = y[2 * hp]
        yb_o[0, :, ln] = y[2 * hp + 1]


def _rwkv_scan(r, v, kk, lwf, kf, bf, lwb, kb, bb):
    b, t, _ = r.shape
    c = SCAN_CHUNK
    nc = t // c
    w = SCAN_HP * LANES
    fwd = pl.BlockSpec((1, c, w), lambda bi, hp, ci: (bi, ci, hp))
    bwd = pl.BlockSpec((1, c, w), lambda bi, hp, ci: (bi, nc - 1 - ci, hp))
    out_sds = jax.ShapeDtypeStruct((b, t, A_WIDTH), F32)
    return pl.pallas_call(
        _scan_kernel,
        grid=(b, A_WIDTH // w, nc),
        in_specs=[fwd] * 6 + [bwd] * 6,
        out_specs=[fwd, bwd],
        out_shape=[out_sds, out_sds],
        scratch_shapes=[pltpu.VMEM((SCAN_HP, LANES, LANES), F32), pltpu.VMEM((SCAN_HP, LANES, LANES), F32)],
        compiler_params=_cparams(("parallel", "parallel", "arbitrary")),
        name="rwkv_scan",
    )(r, v, kk, lwf, kf, bf, r, v, kk, lwb, kb, bb)


POST_TT = 256


def _post_kernel(yf, yb, bonus, g, vec_ref, hs_ref, he_ref, o_ref):
    y = yf[...] + yb[...]
    hsum, hexp = hs_ref[...], he_ref[...]
    inv = 1.0 / A_HEAD_DIM
    mu = _head_sums(y, hsum, hexp) * inv
    d = y - mu
    var = _head_sums(d * d, hsum, hexp) * inv
    yn = d * lax.rsqrt(var + GN_EPS) * vec_ref[0:1, :] + vec_ref[1:2, :]
    o_ref[...] = ((yn + bonus[...]) * g[...]).astype(o_ref.dtype)


def _rwkv_post(yf, yb, bonus, g, gnvec, hsum, hexp):
    n = yf.shape[0]
    tt = min(POST_TT, n)
    blk = pl.BlockSpec((tt, A_WIDTH), lambda i: (i, 0))
    full = lambda a: pl.BlockSpec(a.shape, lambda i: (0,) * a.ndim, pipeline_mode=pl.Buffered(1))
    return pl.pallas_call(
        _post_kernel,
        grid=(n // tt,),
        in_specs=[blk, blk, blk, blk, full(gnvec), full(hsum), full(hexp)],
        out_specs=blk,
        out_shape=jax.ShapeDtypeStruct((n, A_WIDTH), BF16),
        compiler_params=_cparams(("parallel",)),
        name="rwkv_post",
    )(yf, yb, bonus, g, gnvec, hsum, hexp)


def _gelu_tanh(x):
    return 0.5 * x * (1.0 + jnp.tanh(np.sqrt(2.0 / np.pi).astype(np.float32) * (x + 0.044715 * (x * x * x))))


def _sg_kernel(pb_ref, lnv_ref, w_ref, bias_ref, o_ref):
    z = _gelu_tanh(pb_ref[0])
    u = z[:, :B_WIDTH]
    v = _layer_norm_rows(z[:, B_WIDTH:], lnv_ref[0:1, :], lnv_ref[1:2, :])
    lane = lax.broadcasted_iota(I32, (B_CHUNK, LANES), 1)
    m0 = lane < B_GROUP_DIM
    parts = []
    for q in range(B_WIDTH // LANES):
        vq = v[:, q * LANES:(q + 1) * LANES]
        parts.append(_dot(w_ref[q], _stack2(vq, m0)))
    mixed = jnp.concatenate(parts, axis=1) + bias_ref[...]
    o_ref[0] = (u * mixed).astype(o_ref.dtype)


def _spatial_gating(proj3, lnvec, w2, bias_full):
    b, t, _ = proj3.shape
    full = lambda a: pl.BlockSpec(a.shape, lambda bi, i: (0,) * a.ndim)
    return pl.pallas_call(
        _sg_kernel,
        grid=(b, t // B_CHUNK),
        in_specs=[pl.BlockSpec((1, B_CHUNK, B_COLS), lambda bi, i: (bi, i, COL_B // B_COLS)),
                  full(lnvec), full(w2), full(bias_full)],
        out_specs=pl.BlockSpec((1, B_CHUNK, B_WIDTH), lambda bi, i: (bi, i, 0)),
        out_shape=jax.ShapeDtypeStruct((b, t, B_WIDTH), BF16),
        compiler_params=_cparams(("parallel", "parallel")),
        name="spatial_gating",
    )(proj3, lnvec, w2, bias_full)


NA_WIN = C_WIN_ROWS * GRID_W


def _na_row_start(i, rows):
    return jnp.clip(i - C_WIN_ROWS // 2, 0, rows - C_WIN_ROWS)


NA_QR = 4


def _na_kernel(q_ref, k_ref, v_ref, *rest, rows):
    bias_refs, o_ref = rest[:NA_QR], rest[NA_QR]
    kh_ref, vh_ref = rest[NA_QR + 1:]
    i = pl.program_id(1)
    scale = C_HEAD_DIM ** -0.5

    @pl.when(i == 0)
    def _():
        for h in range(C_HEADS):
            sl = slice(h * C_HEAD_DIM, (h + 1) * C_HEAD_DIM)
            kh_ref[h] = k_ref[0, :, sl]
            vh_ref[h] = v_ref[0, :, sl]

    qs, ks, vs, bs = [], [], [], []
    for j in range(NA_QR):
        start = pl.multiple_of(_na_row_start(i * NA_QR + j, rows) * GRID_W, GRID_W)
        q = q_ref[0, j * GRID_W:(j + 1) * GRID_W, :]
        for h in range(C_HEADS):
            sl = slice(h * C_HEAD_DIM, (h + 1) * C_HEAD_DIM)
            qs.append(q[:, sl])
            ks.append(kh_ref[h, pl.ds(start, NA_WIN), :])
            vs.append(vh_ref[h, pl.ds(start, NA_WIN), :])
            bs.append(bias_refs[j][0, h])
    nt = lambda a, b: lax.dot_general(a, b, (((1,), (1,)), ((), ())), preferred_element_type=F32)
    ss = [nt(q, k) * scale + bias for q, k, bias in zip(qs, ks, bs)]
    ps = []
    for s in ss:
        e = jnp.exp(s - jnp.max(s, axis=-1, keepdims=True))
        ps.append((e / jnp.sum(e, axis=-1, keepdims=True)).astype(BF16))
    os_ = [jnp.dot(p, v, preferred_element_type=F32) for p, v in zip(ps, vs)]
    for j in range(NA_QR):
        o_ref[0, j * GRID_W:(j + 1) * GRID_W, :] = jnp.concatenate(
            os_[j * C_HEADS:(j + 1) * C_HEADS], axis=1).astype(o_ref.dtype)


def _neighbourhood_attention(proj3, bias_tab):
    b, t, _ = proj3.shape
    rows = t // GRID_W
    cq = COL_C // C_WIDTH
    qr = NA_QR
    seq = lambda cb: pl.BlockSpec((1, t, C_WIDTH), lambda bi, i: (bi, 0, cb))
    bias = lambda j: pl.BlockSpec((1, C_HEADS, GRID_W, NA_WIN),
                                  lambda bi, i: (i * qr + j - _na_row_start(i * qr + j, rows), 0, 0, 0))
    return pl.pallas_call(
        functools.partial(_na_kernel, rows=rows),
        grid=(b, rows // qr),
        in_specs=[pl.BlockSpec((1, qr * GRID_W, C_WIDTH), lambda bi, i: (bi, i, cq)), seq(cq + 1), seq(cq + 2)]
                 + [bias(j) for j in range(qr)],
        out_specs=pl.BlockSpec((1, qr * GRID_W, C_WIDTH), lambda bi, i: (bi, i, 0)),
        out_shape=jax.ShapeDtypeStruct((b, t, C_WIDTH), BF16),
        scratch_shapes=[pltpu.VMEM((C_HEADS, t, C_HEAD_DIM), BF16), pltpu.VMEM((C_HEADS, t, C_HEAD_DIM), BF16)],
        compiler_params=_cparams(("parallel", "arbitrary")),
        name="nbr_attention",
    )(proj3, proj3, proj3, *([bias_tab] * qr))


def _na_bias_table(rpb, rows):
    kc = C_WIN_COLS
    cols = np.arange(GRID_W)
    col_start = np.clip(cols - kc // 2, 0, GRID_W - kc)
    key_col = np.arange(GRID_W)
    in_win = (key_col[None, :] >= col_start[:, None]) & (key_col[None, :] < col_start[:, None] + kc)
    delta = np.arange(C_WIN_ROWS)
    row_off = np.arange(C_WIN_ROWS)[None, :] - delta[:, None] + (C_WIN_ROWS - 1)
    by_row = rpb[:, row_off]
    span = 2 * GRID_W
    lead = GRID_W - C_WIN_COLS
    padded = jnp.pad(by_row, [(0, 0)] * 3 + [(lead, span - lead - (2 * C_WIN_COLS - 1))])
    skew = jnp.tile(padded, GRID_W)[..., :GRID_W * (span - 1)].reshape(by_row.shape[:3] + (GRID_W, span - 1))
    bias = skew[..., GRID_W - 1:]
    bias = jnp.where(in_win, bias, -1e30)
    bias = jnp.transpose(bias, (1, 0, 3, 2, 4)).reshape(C_WIN_ROWS, C_HEADS, GRID_W, NA_WIN)
    return bias.astype(F32)


MERGE_TM = 512


def _merge_kernel(ya, yb, yc, ga, gb, gc, pa, pb, pc, o_ref):
    m = _sigmoid(ga[...].astype(F32)) * jnp.dot(ya[...], pa[...], preferred_element_type=F32)
    m = m + _sigmoid(gb[...].astype(F32)) * jnp.dot(yb[...], pb[...], preferred_element_type=F32)
    m = m + _sigmoid(gc[...].astype(F32)) * jnp.dot(yc[...], pc[...], preferred_element_type=F32)
    o_ref[...] = m.astype(o_ref.dtype)


def _merge(ya, yb, yc, proj, p_a, p_b, p_c):
    n = ya.shape[0]
    tm = min(MERGE_TM, n)
    rowblk = lambda w: pl.BlockSpec((tm, w), lambda i: (i, 0))
    gate = lambda j: pl.BlockSpec((tm, D_MODEL), lambda i: (i, COL_G // D_MODEL + j))
    full = lambda a: pl.BlockSpec(a.shape, lambda i: (0,) * a.ndim, pipeline_mode=pl.Buffered(1))
    return pl.pallas_call(
        _merge_kernel,
        grid=(n // tm,),
        in_specs=[rowblk(A_WIDTH), rowblk(B_WIDTH), rowblk(C_WIDTH), gate(0), gate(1), gate(2),
                  full(p_a), full(p_b), full(p_c)],
        out_specs=rowblk(D_MODEL),
        out_shape=jax.ShapeDtypeStruct((n, D_MODEL), BF16),
        compiler_params=_cparams(("parallel",)),
        name="branch_merge",
    )(ya, yb, yc, proj, proj, proj, p_a, p_b, p_c)


OUT_TM = 512


def _outln_kernel(m_ref, w_ref, x_ref, lnv_ref, o_ref, ob_ref):
    h = jnp.dot(m_ref[...], w_ref[...], preferred_element_type=F32)
    y = _layer_norm_rows(DEEPNORM_ALPHA * x_ref[...] + h, lnv_ref[0:1, :], lnv_ref[1:2, :])
    o_ref[...] = y
    ob_ref[...] = y.astype(BF16)


def _out_proj_ln(m, w_out, x, lnvec):
    n = m.shape[0]
    tm = min(OUT_TM, n)
    blk = pl.BlockSpec((tm, D_MODEL), lambda i: (i, 0))
    full = lambda a: pl.BlockSpec(a.shape, lambda i: (0,) * a.ndim, pipeline_mode=pl.Buffered(1))
    return pl.pallas_call(
        _outln_kernel,
        grid=(n // tm,),
        in_specs=[blk, full(w_out), blk, full(lnvec)],
        out_specs=[blk, blk],
        out_shape=[jax.ShapeDtypeStruct((n, D_MODEL), F32), jax.ShapeDtypeStruct((n, D_MODEL), BF16)],
        compiler_params=_cparams(("parallel",)),
        name="out_proj_ln",
    )(m, w_out, x, lnvec)


ROUTER_TM = 512


def _router_kernel(x_ref, wt_ref, o_ref):
    xh, xl = _split2(x_ref[...])
    wh, wl = _split2(wt_ref[...])
    nt = lambda a, b: lax.dot_general(a, b, (((1,), (1,)), ((), ())), preferred_element_type=F32)
    logits = nt(wh, xh) + nt(wl, xh) + nt(wh, xl)
    logits = logits - jnp.max(logits, axis=0, keepdims=True)
    e = jnp.exp(logits)
    o_ref[...] = e / jnp.sum(e, axis=0, keepdims=True)


def _router(x, w_router_t):
    n = x.shape[0]
    tm = min(ROUTER_TM, n)
    return pl.pallas_call(
        _router_kernel,
        grid=(n // tm,),
        in_specs=[pl.BlockSpec((tm, D_MODEL), lambda i: (i, 0)),
                  pl.BlockSpec(w_router_t.shape, lambda i: (0, 0))],
        out_specs=pl.BlockSpec((N_EXPERTS, tm), lambda i: (0, i)),
        out_shape=jax.ShapeDtypeStruct((N_EXPERTS, n), F32),
        compiler_params=_cparams(("parallel",)),
        name="router",
    )(x, w_router_t)


def _select_kernel(aff_ref, pos_ref, off_ref, *, cap):
    aff = aff_ref[...]
    e_, g_, l_ = aff.shape
    n = g_ * l_
    bits = pltpu.bitcast(aff, I32)

    def count(mask):
        c = jnp.sum(jnp.where(mask, 1.0, 0.0), axis=2, keepdims=True)
        return jnp.sum(c, axis=1, keepdims=True)

    def thr_body(_, carry):
        lo, hi = carry
        mid = lo + (hi - lo + 1) // 2
        ok = count(bits >= mid) >= cap
        return jnp.where(ok, mid, lo), jnp.where(ok, hi, mid - 1)

    lo0 = jnp.zeros((e_, 1, 1), I32)
    hi0 = jnp.full((e_, 1, 1), 0x7F800000, I32)
    thr, _ = lax.fori_loop(0, 32, thr_body, (lo0, hi0))
    gt = bits > thr
    tie = bits == thr
    need = cap - count(gt)
    idx = lax.broadcasted_iota(I32, aff.shape, 1) * l_ + lax.broadcasted_iota(I32, aff.shape, 2)

    def idx_body(_, carry):
        lo, hi = carry
        mid = (lo + hi) // 2
        ok = count(tie & (idx <= mid)) >= need
        return jnp.where(ok, lo, mid + 1), jnp.where(ok, mid, hi)

    lo1 = jnp.zeros((e_, 1, 1), I32)
    hi1 = jnp.full((e_, 1, 1), n - 1, I32)
    cut, _ = lax.fori_loop(0, int(np.ceil(np.log2(n))) + 1, idx_body, (lo1, hi1))
    sel = jnp.where(gt | (tie & (idx <= cut)), 1.0, 0.0)

    sel2 = sel.reshape(e_ * g_, l_)
    ls = lax.broadcasted_iota(I32, (l_, l_), 0)
    lt = lax.broadcasted_iota(I32, (l_, l_), 1)
    incl = jnp.dot(sel2.astype(BF16), jnp.where(ls <= lt, 1.0, 0.0).astype(BF16), preferred_element_type=F32)
    tot = jnp.dot(sel2.astype(BF16), jnp.ones((l_, l_), BF16), preferred_element_type=F32)
    gs = lax.broadcasted_iota(I32, (g_, g_), 0)
    gt_ = lax.broadcasted_iota(I32, (g_, g_), 1)
    lower = jnp.where(gt_ < gs, 1.0, 0.0).astype(BF16)
    offs = [jnp.dot(lower, tot[e * g_:(e + 1) * g_].astype(BF16), preferred_element_type=F32)
            for e in range(e_)]
    off = jnp.concatenate(offs, axis=0)
    pos = incl + off - sel2
    pos_ref[...] = jnp.where(sel2 > 0.5, pos, -1.0).astype(I32).reshape(e_, g_, l_)
    off_ref[...] = off.astype(I32).reshape(e_, g_, l_)


def _select(aff3, cap):
    full = pl.BlockSpec(aff3.shape, lambda i: (0, 0, 0))
    sds = jax.ShapeDtypeStruct(aff3.shape, I32)
    return pl.pallas_call(
        functools.partial(_select_kernel, cap=cap),
        grid=(1,),
        in_specs=[full],
        out_specs=[full, full],
        out_shape=[sds, sds],
        compiler_params=_cparams(("arbitrary",)),
        name="expert_select",
    )(aff3)


def _gather_kernel(tile_s, valid_s, e_g, slab_g, first_g, *refs):
    k_ = GATHER_GROUP
    x_refs, pos_refs, aff_refs = refs[:k_], refs[k_:2 * k_], refs[2 * k_:3 * k_]
    o_ref, g_ref = refs[3 * k_:]
    g = pl.program_id(0)
    t = MOE_TILE

    @pl.when(first_g[g] == 1)
    def _():
        o_ref[...] = jnp.zeros_like(o_ref)
        g_ref[...] = jnp.zeros_like(g_ref)

    nvalid = valid_s[g * k_]
    for k in range(1, k_):
        nvalid = nvalid + valid_s[g * k_ + k]

    @pl.when(nvalid > 0)
    def _():
        rank = lax.broadcasted_iota(I32, (t, GATHER_TOK), 0) + slab_g[g] * t
        got = o_ref[0].astype(F32)
        gate = g_ref[0]
        per_dot = MXU_DIM // GATHER_TOK
        for k0 in range(0, k_, per_dot):
            onehots, xs = [], []
            for k in range(k0, k0 + per_dot):
                want = jnp.where(valid_s[g * k_ + k] == 1, rank, -2)
                hit = pos_refs[k][0] == want
                onehots.append(jnp.where(hit, 1.0, 0.0).astype(BF16))
                xs.append(x_refs[k][...])
                gate = gate + jnp.sum(jnp.where(hit, aff_refs[k][0], 0.0), axis=1, keepdims=True)
            got = got + jnp.dot(jnp.concatenate(onehots, axis=1), jnp.concatenate(xs, axis=0),
                                preferred_element_type=F32)
        o_ref[0] = got.astype(o_ref.dtype)
        g_ref[0] = gate


def _moe_gather(x_bf, pos_en, aff_en, items, cap):
    n = x_bf.shape[0]
    t = MOE_TILE
    k_ = GATHER_GROUP
    ng = items[2].shape[0]
    tk = GATHER_TOK
    xs = [pl.BlockSpec((tk, D_MODEL), lambda g, tl, va, e, s, fi, k=k: (tl[g * k_ + k], 0)) for k in range(k_)]
    rows = [pl.BlockSpec((1, 1, tk), lambda g, tl, va, e, s, fi, k=k: (e[g], 0, tl[g * k_ + k])) for k in range(k_)]
    grid_spec = pltpu.PrefetchScalarGridSpec(
        num_scalar_prefetch=5,
        grid=(ng,),
        in_specs=xs + rows + rows,
        out_specs=[pl.BlockSpec((1, t, D_MODEL), lambda g, tl, va, e, s, fi: (e[g], s[g], 0)),
                   pl.BlockSpec((1, t, 1), lambda g, tl, va, e, s, fi: (e[g], s[g], 0))],
    )
    pos3 = pos_en.reshape(N_EXPERTS, 1, n)
    aff3 = aff_en.reshape(N_EXPERTS, 1, n)
    return pl.pallas_call(
        _gather_kernel,
        grid_spec=grid_spec,
        out_shape=[jax.ShapeDtypeStruct((N_EXPERTS, cap, D_MODEL), BF16),
                   jax.ShapeDtypeStruct((N_EXPERTS, cap, 1), F32)],
        compiler_params=_cparams(("arbitrary",)),
        name="moe_gather",
    )(*items, *([x_bf] * k_), *([pos3] * k_), *([aff3] * k_))


FFN_TM = 512


def _ffn_kernel(x_ref, gate_ref, wg_ref, wu_ref, wd_ref, o_ref):
    x = x_ref[0]
    g = jnp.dot(x, wg_ref[0], preferred_element_type=F32)
    u = jnp.dot(x, wu_ref[0], preferred_element_type=F32)
    h = (g * _sigmoid(g)) * u
    y = jnp.dot(h.astype(BF16), wd_ref[0], preferred_element_type=F32) * gate_ref[0]
    o_ref[0] = y.astype(o_ref.dtype)


def _expert_ffn(xe, gate, wg, wu, wd, layer):
    e_, cap, _ = xe.shape
    tm = min(FFN_TM, cap)
    wspec = lambda a: pl.BlockSpec((None, 1) + a.shape[2:], lambda e, i: (layer, e, 0, 0))
    rows = lambda w: pl.BlockSpec((1, tm, w), lambda e, i: (e, i, 0))
    sds = jax.ShapeDtypeStruct((e_, cap, D_MODEL), BF16)
    return pl.pallas_call(
        _ffn_kernel,
        grid=(e_, cap // tm),
        in_specs=[rows(D_MODEL), rows(1), wspec(wg), wspec(wu), wspec(wd)],
        out_specs=rows(D_MODEL),
        out_shape=sds,
        compiler_params=_cparams(("parallel", "arbitrary")),
        name="expert_ffn",
    )(xe, gate, wg, wu, wd)


def _combine_kernel(e_s, slab_s, valid_s, tile_g, first_g, last_g, *refs):
    k_ = COMBINE_GROUP
    ye_refs = refs[:k_]
    pos_ref, x_ref, lnv_ref, o_ref, ob_ref, acc_ref = refs[k_:]
    g = pl.program_id(0)
    t = MOE_TILE

    @pl.when(first_g[g] == 1)
    def _():
        acc_ref[...] = jnp.zeros_like(acc_ref)

    nvalid = valid_s[g * k_]
    for k in range(1, k_):
        nvalid = nvalid + valid_s[g * k_ + k]

    @pl.when(nvalid > 0)
    def _():
        w = k_ * MOE_SUB
        p_hi, p_lo = _split2(pos_ref[...].astype(F32))
        expert = lax.broadcasted_iota(I32, (N_EXPERTS, w), 0)
        col_e = lax.broadcasted_iota(I32, (N_EXPERTS, w), 1)
        col = lax.broadcasted_iota(I32, (1, w), 1)
        pick = jnp.zeros((N_EXPERTS, w), F32)
        want = jnp.zeros((1, w), I32)
        for k in range(k_):
            s = g * k_ + k
            in_slot = (col >= k * MOE_SUB) & (col < (k + 1) * MOE_SUB)
            in_slot_e = (col_e >= k * MOE_SUB) & (col_e < (k + 1) * MOE_SUB)
            pick = jnp.where(in_slot_e & (expert == e_s[s]), 1.0, pick)
            base = jnp.where(valid_s[s] == 1, slab_s[s] * MOE_SUB, -2 * MOE_SUB)
            want = jnp.where(in_slot, col - k * MOE_SUB + base, want)
        pick = pick.astype(BF16)
        pos = jnp.dot(p_hi, pick, preferred_element_type=F32) + jnp.dot(p_lo, pick, preferred_element_type=F32)
        onehot = jnp.where(pos == want.astype(F32), 1.0, 0.0).astype(BF16)
        ye = jnp.concatenate([r[0] for r in ye_refs], axis=0)
        acc_ref[...] += jnp.dot(onehot, ye, preferred_element_type=F32)

    @pl.when(last_g[g] == 1)
    def _():
        y = _layer_norm_rows(DEEPNORM_ALPHA * x_ref[...] + acc_ref[...], lnv_ref[0:1, :], lnv_ref[1:2, :])
        o_ref[...] = y
        ob_ref[...] = y.astype(BF16)


def _moe_combine(ye, pos_ne, x, lnvec, items):
    n = x.shape[0]
    t = MOE_TILE
    k_ = COMBINE_GROUP
    ng = items[3].shape[0]
    tile_blk = lambda w: pl.BlockSpec((t, w), lambda g, e, s, va, tl, fi, la: (tl[g], 0))
    slabs = [pl.BlockSpec((1, MOE_SUB, D_MODEL), lambda g, e, s, va, tl, fi, la, k=k: (e[g * k_ + k], s[g * k_ + k], 0))
             for k in range(k_)]
    grid_spec = pltpu.PrefetchScalarGridSpec(
        num_scalar_prefetch=6,
        grid=(ng,),
        in_specs=slabs + [tile_blk(N_EXPERTS), tile_blk(D_MODEL),
                          pl.BlockSpec(lnvec.shape, lambda g, e, s, va, tl, fi, la: (0, 0))],
        out_specs=[tile_blk(D_MODEL), tile_blk(D_MODEL)],
        scratch_shapes=[pltpu.VMEM((t, D_MODEL), F32)],
    )
    return pl.pallas_call(
        _combine_kernel,
        grid_spec=grid_spec,
        out_shape=[jax.ShapeDtypeStruct((n, D_MODEL), F32), jax.ShapeDtypeStruct((n, D_MODEL), BF16)],
        compiler_params=_cparams(("arbitrary",)),
        name="moe_combine",
    )(*items, *([ye] * k_), pos_ne, x, lnvec)


def _moe_items(group_off, n, cap):
    e_ = N_EXPERTS
    big = jnp.int32(2 ** 30)

    def build(rows, by_tile, k_, t):
        ns = cap // rows
        nt = n // t
        starts = group_off[:, ::t // LANES, 0]
        ends = jnp.concatenate([starts[:, 1:], jnp.full((e_, 1), cap, I32)], axis=1)
        e_p = jnp.broadcast_to(jnp.arange(e_, dtype=I32)[:, None], (e_, nt))
        tile_p = jnp.broadcast_to(jnp.arange(nt, dtype=I32)[None, :], (e_, nt))
        slab_p = jnp.minimum(starts // rows, ns - 1)
        valid_p = (ends > starts).astype(I32)
        bound = jnp.arange(ns, dtype=I32) * rows
        tile_s = jnp.sum((starts[:, :, None] <= bound[None, None, :]).astype(I32), axis=1) - 1
        tile_s = jnp.clip(tile_s, 0, nt - 1)
        at_tile = tile_s[:, :, None] == jnp.arange(nt, dtype=I32)[None, None, :]
        st_s = jnp.sum(jnp.where(at_tile, starts[:, None, :], 0), axis=2)
        en_s = jnp.sum(jnp.where(at_tile, ends[:, None, :], 0), axis=2)
        valid_s = ((st_s < bound[None, :]) & (bound[None, :] < en_s)).astype(I32)
        e_s = jnp.broadcast_to(jnp.arange(e_, dtype=I32)[:, None], (e_, ns))
        slab_s = jnp.broadcast_to(jnp.arange(ns, dtype=I32)[None, :], (e_, ns))
        cat = lambda a, b: jnp.concatenate([a.reshape(-1), b.reshape(-1)])
        e_a, slab_a, tile_a = cat(e_p, e_s), cat(slab_p, slab_s), cat(tile_p, tile_s)
        valid_a = cat(valid_p, valid_s)
        secondary = cat(jnp.zeros_like(e_p), jnp.ones_like(e_s))
        ni = e_a.shape[0]
        ar = jnp.arange(ni, dtype=I32)
        if by_tile:
            keep = jnp.maximum(valid_a, 1 - secondary)
            key = ((tile_a * e_ + e_a) * 2 + secondary) * ns + slab_a
        else:
            keep = valid_a
            key = (e_a * ns + slab_a) * nt + tile_a
        key = jnp.where(keep == 1, key, big + ar)
        rank = jnp.sum((key[None, :] < key[:, None]).astype(I32), axis=1)
        code = ((e_a * ns + slab_a) * nt + tile_a) * 2 + valid_a
        decode = lambda cd: (cd // (2 * nt * ns), (cd // (2 * nt)) % ns, (cd // 2) % nt, cd % 2)
        ocode = jnp.sum(jnp.where(rank[None, :] == ar[:, None], code[None, :], 0), axis=1)
        nkeep = jnp.sum(keep)
        kept = ar < nkeep
        oe, os_, otl, _ = decode(ocode)
        blk = otl if by_tile else oe * ns + os_

        nblk = nt if by_tile else e_ * ns
        ng = -(-ni // k_) + nblk
        first = jnp.concatenate([jnp.ones((1,), bool), blk[1:] != blk[:-1]])
        run_start = lax.cummax(jnp.where(first, ar, 0))
        sub = (ar - run_start) % k_
        leads = (sub == 0) & kept
        gid = jnp.cumsum(leads.astype(I32)) - 1
        slot = jnp.where(kept, gid * k_ + sub, -1)
        sar = jnp.arange(ng * k_, dtype=I32)
        hit = slot[None, :] == sar[:, None]
        has = jnp.sum(hit.astype(I32), axis=1)
        scode = jnp.sum(jnp.where(hit, ocode[None, :], 0), axis=1)
        n_real = jnp.sum(leads.astype(I32))
        gar = jnp.arange(ng, dtype=I32)
        lead = scode.reshape(ng, k_)[:, 0]
        lead = jnp.where(gar < n_real, lead, jnp.sum(jnp.where(gar == n_real - 1, lead, 0)))
        scode = jnp.where(has == 1, scode, jnp.repeat(lead, k_))
        se, ss, stl, sv = decode(scode)
        ge, gsl, gtl, _ = decode(lead)
        gblk = gtl if by_tile else ge * ns + gsl
        change = (gblk[1:] != gblk[:-1]).astype(I32)
        one = jnp.ones((1,), I32)
        slots = dict(e=se, slab=ss, tile=stl, valid=sv * has)
        groups = dict(e=ge, slab=gsl, tile=gtl, first=jnp.concatenate([one, change]),
                      last=jnp.concatenate([change, one]))
        return slots, groups

    gs, gg = build(MOE_TILE, False, GATHER_GROUP, GATHER_TOK)
    cs, cg = build(MOE_SUB, True, COMBINE_GROUP, MOE_TILE)
    return ((gs["tile"], gs["valid"], gg["e"], gg["slab"], gg["first"]),
            (cs["e"], cs["slab"], cs["valid"], cg["tile"], cg["first"], cg["last"]))


def _expert_choice_moe_ln(x, x_bf, w_router_t, wg, wu, wd, layer, lnvec):
    n = x.shape[0]
    cap = EC_CAPACITY_FACTOR * n // N_EXPERTS
    aff_en = _router(x, w_router_t)
    pos3, off3 = _select(aff_en.reshape(N_EXPERTS, n // LANES, LANES), cap)
    pos_en = pos3.reshape(N_EXPERTS, n)
    g_items, c_items = _moe_items(off3, n, cap)
    xe, gate = _moe_gather(x_bf, pos_en, aff_en, g_items, cap)
    ye = _expert_ffn(xe, gate, wg, wu, wd, layer)
    return _moe_combine(ye, pos_en.T, x, lnvec, c_items)


def _pack_cols(w):
    a_end = A_COLS
    b_end = a_end + B_COLS
    c_end = b_end + C_COLS
    pad = jnp.zeros(w.shape[:-1] + (LORA_PAD - LORA_COLS,), w.dtype)
    return (jnp.concatenate([w[..., :3 * A_WIDTH], w[..., a_end:b_end], w[..., 3 * A_WIDTH:a_end], pad], axis=-1),
            jnp.concatenate([w[..., c_end:], w[..., b_end:c_end]], axis=-1))


def _lora_rows(w, start):
    k = w.shape[-2]
    return jnp.pad(w, [(0, 0)] * (w.ndim - 2) + [(start, LORA_PAD - start - k), (0, 0)])


def _prepare(w_in, mu_prev, mu_next, decay_w0, decay_w2, iclr_a0, iclr_a2, gate_g2, k_k, k_a, r_k, gn_g, gn_b,
             sg_ln_g, sg_ln_b, sg_w, sg_b, rpb, p_a, p_b, p_c, w_out, ln_mix_g, ln_mix_b, w_router, e_gate, e_up,
             e_down, ln_ffn_g, ln_ffn_b, rows):
    l_ = w_in.shape[0]
    pad_a = lambda m: jnp.pad(m, ((0, 0), (0, LORA_PAD - LORA_COLS)))
    mup, mun = mu_prev, mu_next
    vec_rows = [mup[:, :A_WIDTH], mup[:, A_WIDTH:2 * A_WIDTH], mup[:, 2 * A_WIDTH:3 * A_WIDTH],
                mun[:, :A_WIDTH], mun[:, A_WIDTH:2 * A_WIDTH], mun[:, 2 * A_WIDTH:3 * A_WIDTH],
                k_k, k_a, r_k.reshape(l_, A_WIDTH), decay_w0[:, 0], decay_w0[:, 1], iclr_a0[:, 0], iclr_a0[:, 1]]
    vec = jnp.stack(vec_rows + [jnp.zeros_like(k_k)] * (16 - len(vec_rows)), axis=1)
    lvec = jnp.stack([pad_a(mup[:, 3 * A_WIDTH:]), pad_a(mun[:, 3 * A_WIDTH:])]
                     + [jnp.zeros((l_, LORA_PAD), F32)] * 6, axis=1)
    w2f = jnp.stack([_lora_rows(decay_w2[:, 0], 0), _lora_rows(decay_w2[:, 1], A_DECAY_LORA)], axis=1)
    a2f = jnp.stack([_lora_rows(iclr_a2[:, 0], 2 * A_DECAY_LORA),
                     _lora_rows(iclr_a2[:, 1], 2 * A_DECAY_LORA + A_ICLR_LORA)], axis=1)
    g2f = _lora_rows(gate_g2, 2 * A_DECAY_LORA + 2 * A_ICLR_LORA)
    head = np.arange(A_WIDTH) // A_HEAD_DIM
    hsum = jnp.asarray(head[:, None] == np.arange(LANES)[None, :], BF16)
    sgw2 = sg_w.reshape(l_, B_GROUPS // 2, 2, B_CHUNK, B_CHUNK).transpose(0, 1, 3, 2, 4)
    sgw2 = sgw2.reshape(l_, B_GROUPS // 2, B_CHUNK, 2 * B_CHUNK)
    sg_bias = jnp.repeat(jnp.swapaxes(sg_b, 1, 2), B_GROUP_DIM, axis=2)
    return dict(
        w_in=_pack_cols(w_in.astype(BF16)), vec=vec, lvec=lvec, w2f=w2f.astype(BF16), a2f=a2f.astype(BF16),
        g2f=g2f.astype(BF16), hsum=hsum, hexp=hsum.T, gnvec=jnp.stack([gn_g, gn_b], axis=1),
        sg_ln=jnp.stack([sg_ln_g, sg_ln_b], axis=1), sgw2=sgw2.astype(BF16), sg_bias=sg_bias,
        na_bias=jnp.stack([_na_bias_table(rpb[l], rows) for l in range(l_)]),
        p_a=p_a.astype(BF16), p_b=p_b.astype(BF16), p_c=p_c.astype(BF16), w_out=w_out.astype(BF16),
        ln_mix=jnp.stack([ln_mix_g, ln_mix_b], axis=1), w_router_t=jnp.swapaxes(w_router, 1, 2),
        e_gate=e_gate.astype(BF16), e_up=e_up.astype(BF16), e_down=e_down.astype(BF16),
        ln_ffn=jnp.stack([ln_ffn_g, ln_ffn_b], axis=1))


def _mixer(x, x_bf, p, l, b, t):
    n = b * t
    w_f, w_h = p["w_in"]
    proj = _matmul(x_bf, w_f, l, 1024, IN_TN)
    proj_h = _matmul(x_bf, w_h, l, 1024, IN_TN, BF16)
    proj3 = proj.reshape(b, t, F_COLS)
    proj3_h = proj_h.reshape(b, t, H_COLS)
    r, v, kk, lwf, kf, bf, lwb, kb, bb, bonus, g = _rwkv_prep(
        proj3, p["vec"][l], p["lvec"][l], p["w2f"][l], p["a2f"][l], p["g2f"][l], p["hsum"], p["hexp"])
    yf, yb = _rwkv_scan(r, v, kk, lwf, kf, bf, lwb, kb, bb)
    flat = lambda a: a.reshape(n, a.shape[-1])
    ya = _rwkv_post(flat(yf), flat(yb), flat(bonus), flat(g), p["gnvec"][l], p["hsum"], p["hexp"])
    ybr = _spatial_gating(proj3, p["sg_ln"][l], p["sgw2"][l], p["sg_bias"][l])
    ycr = _neighbourhood_attention(proj3_h, p["na_bias"][l])
    m = _merge(ya, flat(ybr), flat(ycr), proj_h, p["p_a"][l], p["p_b"][l], p["p_c"][l])
    return _out_proj_ln(m, p["w_out"][l], x, p["ln_mix"][l])


def _trunk(x3, p):
    b, t, _ = x3.shape
    x = x3.reshape(b * t, D_MODEL)
    x_bf = x.astype(BF16)
    for l in range(DEPTH):
        x, x_bf = _mixer(x, x_bf, p, l, b, t)
        x, x_bf = _expert_choice_moe_ln(x, x_bf, p["w_router_t"][l], p["e_gate"], p["e_up"], p["e_down"], l,
                                        p["ln_ffn"][l])
    return x.reshape(b, t, D_MODEL)


def kernel(x_prompt, x_sample, w_in, mu_prev, mu_next, decay_w0, decay_w2, iclr_a0, iclr_a2, gate_g2, k_k, k_a, r_k, gn_g, gn_b, sg_ln_g, sg_ln_b, sg_w, sg_b, rpb, p_a, p_b, p_c, w_out, ln_mix_g, ln_mix_b, w_router, e_gate, e_up, e_down, ln_ffn_g, ln_ffn_b):
    assert x_prompt.shape[1] == x_sample.shape[1]
    rows = x_prompt.shape[1] // GRID_W
    p = _prepare(w_in, mu_prev, mu_next, decay_w0, decay_w2, iclr_a0, iclr_a2, gate_g2, k_k, k_a, r_k, gn_g, gn_b,
                 sg_ln_g, sg_ln_b, sg_w, sg_b, rpb, p_a, p_b, p_c, w_out, ln_mix_g, ln_mix_b, w_router, e_gate,
                 e_up, e_down, ln_ffn_g, ln_ffn_b, rows)
    return (_trunk(x_prompt, p), _trunk(x_sample, p))
```

```python
import functools
from typing import NamedTuple

import numpy as np
import jax
import jax.numpy as jnp
from jax import lax
from jax.experimental import pallas as pl
from jax.experimental.pallas import tpu as pltpu

F32 = jnp.float32
BF16 = jnp.bfloat16
I32 = jnp.int32

D_MODEL = 2048
DEPTH = 4
GRID_W = 64
A_HEADS = 16
A_HEAD_DIM = 64
A_WIDTH = A_HEADS * A_HEAD_DIM
A_DECAY_LORA = 64
A_ICLR_LORA = 64
A_GATE_LORA = 160
B_GROUPS = 8
B_GROUP_DIM = 64
B_WIDTH = B_GROUPS * B_GROUP_DIM
B_CHUNK = 128
C_HEADS = 8
C_HEAD_DIM = 64
C_WIDTH = C_HEADS * C_HEAD_DIM
C_WIN_ROWS = 8
C_WIN_COLS = 16
N_EXPERTS = 16
EXPERT_HIDDEN = 1024
EC_CAPACITY_FACTOR = 2
A_COLS = 3 * A_WIDTH + 2 * A_DECAY_LORA + 2 * A_ICLR_LORA + A_GATE_LORA
B_COLS = 2 * B_WIDTH
C_COLS = 3 * C_WIDTH
G_COLS = 3 * D_MODEL
DEEPNORM_ALPHA = (2 * DEPTH) ** 0.25
LN_EPS = 1e-5
GN_EPS = 64e-5

LORA_COLS = 2 * A_DECAY_LORA + 2 * A_ICLR_LORA + A_GATE_LORA
LORA_PAD = 512
COL_R, COL_K, COL_V = 0, A_WIDTH, 2 * A_WIDTH
COL_B = 3 * A_WIDTH
COL_L = COL_B + B_COLS
F_COLS = COL_L + LORA_PAD
COL_G = 0
COL_C = G_COLS
H_COLS = COL_C + C_COLS
IN_TN = 1536

LANES = 128
MXU_DIM = 256
SCAN_CHUNK = 64
MOE_TILE = 256
MOE_SUB = 64
GATHER_TOK = 256
GATHER_GROUP = 4
COMBINE_GROUP = 8
VMEM_LIMIT = 48 * 1024 * 1024


def _cparams(sem):
    return pltpu.CompilerParams(dimension_semantics=sem, vmem_limit_bytes=VMEM_LIMIT)


def _dot(a, b):
    return jnp.dot(a.astype(BF16), b.astype(BF16), preferred_element_type=F32)


def _split2(x):
    hi = x.astype(BF16)
    lo = (x - hi.astype(F32)).astype(BF16)
    return hi, lo


def _split3(x):
    hi = x.astype(BF16)
    r1 = x - hi.astype(F32)
    mid = r1.astype(BF16)
    lo = (r1 - mid.astype(F32)).astype(BF16)
    return hi, mid, lo


def _dot_x01(x, m01):
    hi, lo = _split2(x)
    m = m01.astype(BF16)
    return jnp.dot(hi, m, preferred_element_type=F32) + jnp.dot(lo, m, preferred_element_type=F32)


def _head_sums(x, hsum, hexp):
    return _dot_x01(_dot_x01(x, hsum), hexp)


def _sigmoid(x):
    return 1.0 / (1.0 + jnp.exp(-x))


def _layer_norm_rows(z, g, b):
    mu = jnp.mean(z, axis=-1, keepdims=True)
    d = z - mu
    var = jnp.mean(d * d, axis=-1, keepdims=True)
    return d * lax.rsqrt(var + LN_EPS) * g + b


def _mm_kernel(a_ref, w_ref, o_ref):
    o_ref[...] = jnp.dot(a_ref[...], w_ref[...], preferred_element_type=F32).astype(o_ref.dtype)


def _matmul(a, w, layer, tm, tn, out_dtype=F32):
    m, k = a.shape
    n = w.shape[2]
    tm = min(tm, m)
    return pl.pallas_call(
        _mm_kernel,
        grid=(n // tn, m // tm),
        in_specs=[pl.BlockSpec((tm, k), lambda j, i: (i, 0)),
                  pl.BlockSpec((None, k, tn), lambda j, i: (layer, 0, j))],
        out_specs=pl.BlockSpec((tm, tn), lambda j, i: (i, j)),
        out_shape=jax.ShapeDtypeStruct((m, n), out_dtype),
        compiler_params=_cparams(("parallel", "parallel")),
        name="in_proj_" + jnp.dtype(out_dtype).name,
    )(a, w)


PREP_TT = 256
V_MUP_R, V_MUP_K, V_MUP_V, V_MUN_R, V_MUN_K, V_MUN_V, V_KK, V_KA, V_RK, V_W0F, V_W0B, V_A0F, V_A0B = range(13)


def _shift(cur, prv8, nxt8, mup, mun, i, nt):
    tt = cur.shape[0]
    row = lax.broadcasted_iota(I32, cur.shape, 0)
    first = jnp.where(i > 0, prv8[7:8, :], 0.0)
    last = jnp.where(i < nt - 1, nxt8[0:1, :], 0.0)
    prev = jnp.where(row == 0, first, pltpu.roll(cur, 1, 0))
    nxt = jnp.where(row == tt - 1, last, pltpu.roll(cur, tt - 1, 0))
    return cur * (1.0 - mup - mun) + mup * prev + mun * nxt


def _prep_kernel(rc, kc, vc, lc, rp, kp, vp, lp, rn, kn, vn, ln_, vec_ref, lvec_ref, w2_ref, a2_ref, g2_ref,
                 hs_ref, he_ref, r_o, v_o, kk_o, lwf_o, kf_o, bf_o, lwb_o, kb_o, bb_o, bonus_o, g_o):
    i = pl.program_id(1)
    nt = pl.num_programs(1)
    vec = vec_ref[...]
    row = lambda j: vec[j:j + 1, :]
    r = _shift(rc[0], rp[0], rn[0], row(V_MUP_R), row(V_MUN_R), i, nt)
    k = _shift(kc[0], kp[0], kn[0], row(V_MUP_K), row(V_MUN_K), i, nt)
    v = _shift(vc[0], vp[0], vn[0], row(V_MUP_V), row(V_MUN_V), i, nt)
    lo = _shift(lc[0], lp[0], ln_[0], lvec_ref[0:1, :], lvec_ref[1:2, :], i, nt)

    hsum, hexp = hs_ref[...], he_ref[...]
    kk = k * row(V_KK)
    kk = kk / jnp.maximum(jnp.sqrt(_head_sums(kk * kk, hsum, hexp)), 1e-12)

    tanh_lo = jnp.tanh(lo)
    ksum = jnp.zeros_like(k)
    outs = ((lwf_o, kf_o, bf_o), (lwb_o, kb_o, bb_o))
    for d in range(2):
        z = row(V_W0F + d) + _dot(tanh_lo, w2_ref[d])
        nz = -z
        softplus = jnp.maximum(nz, 0.0) + jnp.log(1.0 + jnp.exp(-jnp.abs(nz)))
        w_log = -softplus - 0.5
        lw = -jnp.exp(w_log)
        a = _sigmoid(row(V_A0F + d) + _dot(lo, a2_ref[d]))
        k_d = k * (1.0 + (a - 1.0) * row(V_KA))
        ksum = ksum + k_d
        lw_o, kd_o, bd_o = outs[d]
        lw_o[0] = lw
        kd_o[0] = k_d
        bd_o[0] = kk * a
    r_o[0] = r
    v_o[0] = v
    kk_o[0] = kk
    bonus_o[0] = _head_sums(r * ksum * row(V_RK), hsum, hexp) * v
    g_o[0] = _dot(_sigmoid(lo), g2_ref[...])


def _rwkv_prep(proj3, vec, lvec, w2f, a2f, g2f, hsum, hexp):
    b, t, _ = proj3.shape
    tt = PREP_TT
    nt = t // tt
    h8 = tt // 8
    cur = lambda cb, w: pl.BlockSpec((1, tt, w), lambda bi, i: (bi, i, cb))
    prv = lambda cb, w: pl.BlockSpec((1, 8, w), lambda bi, i: (bi, jnp.maximum(i * h8 - 1, 0), cb))
    nxt = lambda cb, w: pl.BlockSpec((1, 8, w), lambda bi, i: (bi, jnp.minimum((i + 1) * h8, t // 8 - 1), cb))
    cols = [(COL_R // A_WIDTH, A_WIDTH), (COL_K // A_WIDTH, A_WIDTH), (COL_V // A_WIDTH, A_WIDTH),
            (COL_L // LORA_PAD, LORA_PAD)]
    full = lambda shape: pl.BlockSpec(shape, lambda bi, i: (0,) * len(shape))
    in_specs = ([cur(*c) for c in cols] + [prv(*c) for c in cols] + [nxt(*c) for c in cols]
                + [full(vec.shape), full(lvec.shape), full(w2f.shape), full(a2f.shape), full(g2f.shape),
                   full(hsum.shape), full(hexp.shape)])
    out_spec = pl.BlockSpec((1, tt, A_WIDTH), lambda bi, i: (bi, i, 0))
    out_sds = jax.ShapeDtypeStruct((b, t, A_WIDTH), F32)
    return pl.pallas_call(
        _prep_kernel,
        grid=(b, nt),
        in_specs=in_specs,
        out_specs=[out_spec] * 11,
        out_shape=[out_sds] * 11,
        compiler_params=_cparams(("parallel", "parallel")),
        name="rwkv_prep",
    )(*([proj3] * 12), vec, lvec, w2f, a2f, g2f, hsum, hexp)


def _stack2(x, m0):
    return jnp.concatenate([jnp.where(m0, x, 0.0), jnp.where(m0, 0.0, x)], axis=0)


class _Chain(NamedTuple):
    r: jax.Array
    v: jax.Array
    kk: jax.Array
    lw: jax.Array
    kd: jax.Array
    bd: jax.Array
    s_ref: object
    rev: bool


def _scan_masks(c, reverse):
    ti = lax.broadcasted_iota(I32, (c, c), 0)
    si = lax.broadcasted_iota(I32, (c, c), 1)
    tri = jnp.where((si >= ti) if reverse else (si <= ti), 1.0, 0.0).astype(BF16)
    t2 = lax.broadcasted_iota(I32, (c, 2 * c), 0)
    s2 = lax.broadcasted_iota(I32, (c, 2 * c), 1)
    s2 = jnp.where(s2 >= c, s2 - c, s2)
    strict = (s2 > t2) if reverse else (s2 < t2)
    incl = (s2 >= t2) if reverse else (s2 <= t2)
    eye2 = jnp.where(s2 == t2, 1.0, 0.0)
    return tri, strict, incl, eye2


def _scan_chunk(chains):
    c = SCAN_CHUNK
    masks = {rev: _scan_masks(c, rev) for rev in (False, True)}
    m0 = lax.broadcasted_iota(I32, (c, LANES), 1) < A_HEAD_DIM
    half = lax.broadcasted_iota(I32, (c, 2 * c), 1) < c
    vi = lax.broadcasted_iota(I32, (LANES, LANES), 0)
    ki = lax.broadcasted_iota(I32, (LANES, LANES), 1)
    same_head = (vi < A_HEAD_DIM) == (ki < A_HEAD_DIM)
    bf = lambda x: x.astype(BF16)
    mm = lambda a, b: jnp.dot(a, b, preferred_element_type=F32)
    mm_nt = lambda a, b: lax.dot_general(a, b, (((1,), (1,)), ((), ())), preferred_element_type=F32)
    cat = jnp.concatenate

    cum = []
    for ch in chains:
        tri = masks[ch.rev][0]
        parts = mm(tri, cat(_split3(ch.lw), axis=1))
        cum.append(parts[:, :LANES] + parts[:, LANES:2 * LANES] + parts[:, 2 * LANES:])

    ar, bk2, bkw, v2, etot = [], [], [], [], []
    for ch, cm in zip(chains, cum):
        tot = cm[0:1, :] if ch.rev else cm[c - 1:c, :]
        einv = jnp.exp(-cm)
        etail = jnp.exp(tot - cm)
        at = -ch.kk * jnp.exp(cm - ch.lw)
        rt = ch.r * jnp.exp(cm)
        ar.append(bf(cat([at, rt], axis=0)))
        bk2.append(bf(cat([_stack2(ch.bd * einv, m0), _stack2(ch.kd * einv, m0)], axis=0)))
        bkw.append(bf(cat([ch.bd * etail, ch.kd * etail], axis=0)))
        v2.append(bf(_stack2(ch.v, m0)))
        etot.append(jnp.exp(tot))

    s0 = [ch.s_ref[...] for ch in chains]
    both = [mm_nt(a, cat([b, bf(s)], axis=0)) for a, b, s in zip(ar, bk2, s0)]
    mt = [r[:, :4 * c] for r in both]
    ars = [r[:, 4 * c:] for r in both]
    a_ab, a_xk, a_rb = [], [], []
    for ch, m in zip(chains, mt):
        _, strict, incl, _ = masks[ch.rev]
        a_ab.append(jnp.where(strict, m[:c, :2 * c], 0.0))
        a_xk.append(bf(cat([jnp.where(strict, m[:c, 2 * c:], 0.0), jnp.where(incl, m[c:, 2 * c:], 0.0)], axis=0)))
        a_rb.append(bf(jnp.where(incl, m[c:, :2 * c], 0.0)))
    xkv = [mm(a, v) for a, v in zip(a_xk, v2)]

    tm = [masks[ch.rev][3] + a for ch, a in zip(chains, a_ab)]
    pw = [mm(bf(a), _stack2(bf(a), half)) for a in a_ab]
    steps = int(np.log2(c)) - 1
    for k in range(steps):
        blk = [_stack2(bf(x), half) for x in pw]
        if k < steps - 1:
            both = [mm(bf(cat([t, x], axis=0)), d) for t, x, d in zip(tm, pw, blk)]
            tm = [t + r[:c] for t, r in zip(tm, both)]
            pw = [r[c:] for r in both]
        else:
            tm = [t + mm(bf(t), d) for t, d in zip(tm, blk)]

    u = [mm(bf(t), bf(_stack2(a[:c] + k[:c], m0))) for t, a, k in zip(tm, ars, xkv)]
    y = [a[c:] + k[c:] + mm(ab, bf(_stack2(uu, m0))) for a, k, ab, uu in zip(ars, xkv, a_rb, u)]
    upd = [mm(bf(cat([uu, ch.v], axis=0).T), w) for uu, ch, w in zip(u, chains, bkw)]
    for ch, s, e, up in zip(chains, s0, etot, upd):
        ch.s_ref[...] = s * e + jnp.where(same_head, up, 0.0)
    return y


SCAN_HP = 8


def _scan_kernel(rf, vf, kkf, lwf, kf, bf, rb, vb, kkb, lwb, kb, bb, yf_o, yb_o, sf_ref, sb_ref):
    @pl.when(pl.program_id(2) == 0)
    def _():
        sf_ref[...] = jnp.zeros_like(sf_ref)
        sb_ref[...] = jnp.zeros_like(sb_ref)

    chains = []
    for hp in range(SCAN_HP):
        ln = slice(hp * LANES, (hp + 1) * LANES)
        chains.append(_Chain(rf[0, :, ln], vf[0, :, ln], kkf[0, :, ln], lwf[0, :, ln], kf[0, :, ln], bf[0, :, ln],
                             sf_ref.at[hp], False))
        chains.append(_Chain(rb[0, :, ln], vb[0, :, ln], kkb[0, :, ln], lwb[0, :, ln], kb[0, :, ln], bb[0, :, ln],
                             sb_ref.at[hp], True))
    y = _scan_chunk(chains)
    for hp in range(SCAN_HP):
        ln = slice(hp * LANES, (hp + 1) * LANES)
        yf_o[0, :, ln] = y[2 * hp]
        yb_o[0, :, ln] = y[2 * hp + 1]


def _rwkv_scan(r, v, kk, lwf, kf, bf, lwb, kb, bb):
    b, t, _ = r.shape
    c = SCAN_CHUNK
    nc = t // c
    w = SCAN_HP * LANES
    fwd = pl.BlockSpec((1, c, w), lambda bi, hp, ci: (bi, ci, hp))
    bwd = pl.BlockSpec((1, c, w), lambda bi, hp, ci: (bi, nc - 1 - ci, hp))
    out_sds = jax.ShapeDtypeStruct((b, t, A_WIDTH), F32)
    return pl.pallas_call(
        _scan_kernel,
        grid=(b, A_WIDTH // w, nc),
        in_specs=[fwd] * 6 + [bwd] * 6,
        out_specs=[fwd, bwd],
        out_shape=[out_sds, out_sds],
        scratch_shapes=[pltpu.VMEM((SCAN_HP, LANES, LANES), F32), pltpu.VMEM((SCAN_HP, LANES, LANES), F32)],
        compiler_params=_cparams(("parallel", "parallel", "arbitrary")),
        name="rwkv_scan",
    )(r, v, kk, lwf, kf, bf, r, v, kk, lwb, kb, bb)


POST_TT = 256


def _post_kernel(yf, yb, bonus, g, vec_ref, hs_ref, he_ref, o_ref):
    y = yf[...] + yb[...]
    hsum, hexp = hs_ref[...], he_ref[...]
    inv = 1.0 / A_HEAD_DIM
    mu = _head_sums(y, hsum, hexp) * inv
    d = y - mu
    var = _head_sums(d * d, hsum, hexp) * inv
    yn = d * lax.rsqrt(var + GN_EPS) * vec_ref[0:1, :] + vec_ref[1:2, :]
    o_ref[...] = ((yn + bonus[...]) * g[...]).astype(o_ref.dtype)


def _rwkv_post(yf, yb, bonus, g, gnvec, hsum, hexp):
    n = yf.shape[0]
    tt = min(POST_TT, n)
    blk = pl.BlockSpec((tt, A_WIDTH), lambda i: (i, 0))
    full = lambda a: pl.BlockSpec(a.shape, lambda i: (0,) * a.ndim, pipeline_mode=pl.Buffered(1))
    return pl.pallas_call(
        _post_kernel,
        grid=(n // tt,),
        in_specs=[blk, blk, blk, blk, full(gnvec), full(hsum), full(hexp)],
        out_specs=blk,
        out_shape=jax.ShapeDtypeStruct((n, A_WIDTH), BF16),
        compiler_params=_cparams(("parallel",)),
        name="rwkv_post",
    )(yf, yb, bonus, g, gnvec, hsum, hexp)


def _gelu_tanh(x):
    return 0.5 * x * (1.0 + jnp.tanh(np.sqrt(2.0 / np.pi).astype(np.float32) * (x + 0.044715 * (x * x * x))))


def _sg_kernel(pb_ref, lnv_ref, w_ref, bias_ref, o_ref):
    z = _gelu_tanh(pb_ref[0])
    u = z[:, :B_WIDTH]
    v = _layer_norm_rows(z[:, B_WIDTH:], lnv_ref[0:1, :], lnv_ref[1:2, :])
    lane = lax.broadcasted_iota(I32, (B_CHUNK, LANES), 1)
    m0 = lane < B_GROUP_DIM
    parts = []
    for q in range(B_WIDTH // LANES):
        vq = v[:, q * LANES:(q + 1) * LANES]
        parts.append(_dot(w_ref[q], _stack2(vq, m0)))
    mixed = jnp.concatenate(parts, axis=1) + bias_ref[...]
    o_ref[0] = (u * mixed).astype(o_ref.dtype)


def _spatial_gating(proj3, lnvec, w2, bias_full):
    b, t, _ = proj3.shape
    full = lambda a: pl.BlockSpec(a.shape, lambda bi, i: (0,) * a.ndim)
    return pl.pallas_call(
        _sg_kernel,
        grid=(b, t // B_CHUNK),
        in_specs=[pl.BlockSpec((1, B_CHUNK, B_COLS), lambda bi, i: (bi, i, COL_B // B_COLS)),
                  full(lnvec), full(w2), full(bias_full)],
        out_specs=pl.BlockSpec((1, B_CHUNK, B_WIDTH), lambda bi, i: (bi, i, 0)),
        out_shape=jax.ShapeDtypeStruct((b, t, B_WIDTH), BF16),
        compiler_params=_cparams(("parallel", "parallel")),
        name="spatial_gating",
    )(proj3, lnvec, w2, bias_full)


NA_WIN = C_WIN_ROWS * GRID_W


def _na_row_start(i, rows):
    return jnp.clip(i - C_WIN_ROWS // 2, 0, rows - C_WIN_ROWS)


NA_QR = 4


def _na_kernel(q_ref, k_ref, v_ref, *rest, rows):
    bias_refs, o_ref = rest[:NA_QR], rest[NA_QR]
    kh_ref, vh_ref = rest[NA_QR + 1:]
    i = pl.program_id(1)
    scale = C_HEAD_DIM ** -0.5

    @pl.when(i == 0)
    def _():
        for h in range(C_HEADS):
            sl = slice(h * C_HEAD_DIM, (h + 1) * C_HEAD_DIM)
            kh_ref[h] = k_ref[0, :, sl]
            vh_ref[h] = v_ref[0, :, sl]

    qs, ks, vs, bs = [], [], [], []
    for j in range(NA_QR):
        start = pl.multiple_of(_na_row_start(i * NA_QR + j, rows) * GRID_W, GRID_W)
        q = q_ref[0, j * GRID_W:(j + 1) * GRID_W, :]
        for h in range(C_HEADS):
            sl = slice(h * C_HEAD_DIM, (h + 1) * C_HEAD_DIM)
            qs.append(q[:, sl])
            ks.append(kh_ref[h, pl.ds(start, NA_WIN), :])
            vs.append(vh_ref[h, pl.ds(start, NA_WIN), :])
            bs.append(bias_refs[j][0, h])
    nt = lambda a, b: lax.dot_general(a, b, (((1,), (1,)), ((), ())), preferred_element_type=F32)
    ss = [nt(q, k) * scale + bias for q, k, bias in zip(qs, ks, bs)]
    ps = []
    for s in ss:
        e = jnp.exp(s - jnp.max(s, axis=-1, keepdims=True))
        ps.append((e / jnp.sum(e, axis=-1, keepdims=True)).astype(BF16))
    os_ = [jnp.dot(p, v, preferred_element_type=F32) for p, v in zip(ps, vs)]
    for j in range(NA_QR):
        o_ref[0, j * GRID_W:(j + 1) * GRID_W, :] = jnp.concatenate(
            os_[j * C_HEADS:(j + 1) * C_HEADS], axis=1).astype(o_ref.dtype)


def _neighbourhood_attention(proj3, bias_tab):
    b, t, _ = proj3.shape
    rows = t // GRID_W
    cq = COL_C // C_WIDTH
    qr = NA_QR
    seq = lambda cb: pl.BlockSpec((1, t, C_WIDTH), lambda bi, i: (bi, 0, cb))
    bias = lambda j: pl.BlockSpec((1, C_HEADS, GRID_W, NA_WIN),
                                  lambda bi, i: (i * qr + j - _na_row_start(i * qr + j, rows), 0, 0, 0))
    return pl.pallas_call(
        functools.partial(_na_kernel, rows=rows),
        grid=(b, rows // qr),
        in_specs=[pl.BlockSpec((1, qr * GRID_W, C_WIDTH), lambda bi, i: (bi, i, cq)), seq(cq + 1), seq(cq + 2)]
                 + [bias(j) for j in range(qr)],
        out_specs=pl.BlockSpec((1, qr * GRID_W, C_WIDTH), lambda bi, i: (bi, i, 0)),
        out_shape=jax.ShapeDtypeStruct((b, t, C_WIDTH), BF16),
        scratch_shapes=[pltpu.VMEM((C_HEADS, t, C_HEAD_DIM), BF16), pltpu.VMEM((C_HEADS, t, C_HEAD_DIM), BF16)],
        compiler_params=_cparams(("parallel", "arbitrary")),
        name="nbr_attention",
    )(proj3, proj3, proj3, *([bias_tab] * qr))


def _na_bias_table(rpb, rows):
    kc = C_WIN_COLS
    cols = np.arange(GRID_W)
    col_start = np.clip(cols - kc // 2, 0, GRID_W - kc)
    key_col = np.arange(GRID_W)
    in_win = (key_col[None, :] >= col_start[:, None]) & (key_col[None, :] < col_start[:, None] + kc)
    delta = np.arange(C_WIN_ROWS)
    row_off = np.arange(C_WIN_ROWS)[None, :] - delta[:, None] + (C_WIN_ROWS - 1)
    by_row = rpb[:, row_off]
    span = 2 * GRID_W
    lead = GRID_W - C_WIN_COLS
    padded = jnp.pad(by_row, [(0, 0)] * 3 + [(lead, span - lead - (2 * C_WIN_COLS - 1))])
    skew = jnp.tile(padded, GRID_W)[..., :GRID_W * (span - 1)].reshape(by_row.shape[:3] + (GRID_W, span - 1))
    bias = skew[..., GRID_W - 1:]
    bias = jnp.where(in_win, bias, -1e30)
    bias = jnp.transpose(bias, (1, 0, 3, 2, 4)).reshape(C_WIN_ROWS, C_HEADS, GRID_W, NA_WIN)
    return bias.astype(F32)


MERGE_TM = 512


def _merge_kernel(ya, yb, yc, ga, gb, gc, pa, pb, pc, o_ref):
    m = _sigmoid(ga[...].astype(F32)) * jnp.dot(ya[...], pa[...], preferred_element_type=F32)
    m = m + _sigmoid(gb[...].astype(F32)) * jnp.dot(yb[...], pb[...], preferred_element_type=F32)
    m = m + _sigmoid(gc[...].astype(F32)) * jnp.dot(yc[...], pc[...], preferred_element_type=F32)
    o_ref[...] = m.astype(o_ref.dtype)


def _merge(ya, yb, yc, proj, p_a, p_b, p_c):
    n = ya.shape[0]
    tm = min(MERGE_TM, n)
    rowblk = lambda w: pl.BlockSpec((tm, w), lambda i: (i, 0))
    gate = lambda j: pl.BlockSpec((tm, D_MODEL), lambda i: (i, COL_G // D_MODEL + j))
    full = lambda a: pl.BlockSpec(a.shape, lambda i: (0,) * a.ndim, pipeline_mode=pl.Buffered(1))
    return pl.pallas_call(
        _merge_kernel,
        grid=(n // tm,),
        in_specs=[rowblk(A_WIDTH), rowblk(B_WIDTH), rowblk(C_WIDTH), gate(0), gate(1), gate(2),
                  full(p_a), full(p_b), full(p_c)],
        out_specs=rowblk(D_MODEL),
        out_shape=jax.ShapeDtypeStruct((n, D_MODEL), BF16),
        compiler_params=_cparams(("parallel",)),
        name="branch_merge",
    )(ya, yb, yc, proj, proj, proj, p_a, p_b, p_c)


OUT_TM = 512


def _outln_kernel(m_ref, w_ref, x_ref, lnv_ref, o_ref, ob_ref):
    h = jnp.dot(m_ref[...], w_ref[...], preferred_element_type=F32)
    y = _layer_norm_rows(DEEPNORM_ALPHA * x_ref[...] + h, lnv_ref[0:1, :], lnv_ref[1:2, :])
    o_ref[...] = y
    ob_ref[...] = y.astype(BF16)


def _out_proj_ln(m, w_out, x, lnvec):
    n = m.shape[0]
    tm = min(OUT_TM, n)
    blk = pl.BlockSpec((tm, D_MODEL), lambda i: (i, 0))
    full = lambda a: pl.BlockSpec(a.shape, lambda i: (0,) * a.ndim, pipeline_mode=pl.Buffered(1))
    return pl.pallas_call(
        _outln_kernel,
        grid=(n // tm,),
        in_specs=[blk, full(w_out), blk, full(lnvec)],
        out_specs=[blk, blk],
        out_shape=[jax.ShapeDtypeStruct((n, D_MODEL), F32), jax.ShapeDtypeStruct((n, D_MODEL), BF16)],
        compiler_params=_cparams(("parallel",)),
        name="out_proj_ln",
    )(m, w_out, x, lnvec)


ROUTER_TM = 512


def _router_kernel(x_ref, wt_ref, o_ref):
    xh, xl = _split2(x_ref[...])
    wh, wl = _split2(wt_ref[...])
    nt = lambda a, b: lax.dot_general(a, b, (((1,), (1,)), ((), ())), preferred_element_type=F32)
    logits = nt(wh, xh) + nt(wl, xh) + nt(wh, xl)
    logits = logits - jnp.max(logits, axis=0, keepdims=True)
    e = jnp.exp(logits)
    o_ref[...] = e / jnp.sum(e, axis=0, keepdims=True)


def _router(x, w_router_t):
    n = x.shape[0]
    tm = min(ROUTER_TM, n)
    return pl.pallas_call(
        _router_kernel,
        grid=(n // tm,),
        in_specs=[pl.BlockSpec((tm, D_MODEL), lambda i: (i, 0)),
                  pl.BlockSpec(w_router_t.shape, lambda i: (0, 0))],
        out_specs=pl.BlockSpec((N_EXPERTS, tm), lambda i: (0, i)),
        out_shape=jax.ShapeDtypeStruct((N_EXPERTS, n), F32),
        compiler_params=_cparams(("parallel",)),
        name="router",
    )(x, w_router_t)


def _select_kernel(aff_ref, pos_ref, off_ref, *, cap):
    aff = aff_ref[...]
    e_, g_, l_ = aff.shape
    n = g_ * l_
    bits = pltpu.bitcast(aff, I32)

    def count(mask):
        c = jnp.sum(jnp.where(mask, 1.0, 0.0), axis=2, keepdims=True)
        return jnp.sum(c, axis=1, keepdims=True)

    def thr_body(_, carry):
        lo, hi = carry
        mid = lo + (hi - lo + 1) // 2
        ok = count(bits >= mid) >= cap
        return jnp.where(ok, mid, lo), jnp.where(ok, hi, mid - 1)

    lo0 = jnp.zeros((e_, 1, 1), I32)
    hi0 = jnp.full((e_, 1, 1), 0x7F800000, I32)
    thr, _ = lax.fori_loop(0, 32, thr_body, (lo0, hi0))
    gt = bits > thr
    tie = bits == thr
    need = cap - count(gt)
    idx = lax.broadcasted_iota(I32, aff.shape, 1) * l_ + lax.broadcasted_iota(I32, aff.shape, 2)

    def idx_body(_, carry):
        lo, hi = carry
        mid = (lo + hi) // 2
        ok = count(tie & (idx <= mid)) >= need
        return jnp.where(ok, lo, mid + 1), jnp.where(ok, mid, hi)

    lo1 = jnp.zeros((e_, 1, 1), I32)
    hi1 = jnp.full((e_, 1, 1), n - 1, I32)
    cut, _ = lax.fori_loop(0, int(np.ceil(np.log2(n))) + 1, idx_body, (lo1, hi1))
    sel = jnp.where(gt | (tie & (idx <= cut)), 1.0, 0.0)

    sel2 = sel.reshape(e_ * g_, l_)
    ls = lax.broadcasted_iota(I32, (l_, l_), 0)
    lt = lax.broadcasted_iota(I32, (l_, l_), 1)
    incl = jnp.dot(sel2.astype(BF16), jnp.where(ls <= lt, 1.0, 0.0).astype(BF16), preferred_element_type=F32)
    tot = jnp.dot(sel2.astype(BF16), jnp.ones((l_, l_), BF16), preferred_element_type=F32)
    gs = lax.broadcasted_iota(I32, (g_, g_), 0)
    gt_ = lax.broadcasted_iota(I32, (g_, g_), 1)
    lower = jnp.where(gt_ < gs, 1.0, 0.0).astype(BF16)
    offs = [jnp.dot(lower, tot[e * g_:(e + 1) * g_].astype(BF16), preferred_element_type=F32)
            for e in range(e_)]
    off = jnp.concatenate(offs, axis=0)
    pos = incl + off - sel2
    pos_ref[...] = jnp.where(sel2 > 0.5, pos, -1.0).astype(I32).reshape(e_, g_, l_)
    off_ref[...] = off.astype(I32).reshape(e_, g_, l_)


def _select(aff3, cap):
    full = pl.BlockSpec(aff3.shape, lambda i: (0, 0, 0))
    sds = jax.ShapeDtypeStruct(aff3.shape, I32)
    return pl.pallas_call(
        functools.partial(_select_kernel, cap=cap),
        grid=(1,),
        in_specs=[full],
        out_specs=[full, full],
        out_shape=[sds, sds],
        compiler_params=_cparams(("arbitrary",)),
        name="expert_select",
    )(aff3)


def _gather_kernel(tile_s, valid_s, e_g, slab_g, first_g, *refs):
    k_ = GATHER_GROUP
    x_refs, pos_refs, aff_refs = refs[:k_], refs[k_:2 * k_], refs[2 * k_:3 * k_]
    o_ref, g_ref = refs[3 * k_:]
    g = pl.program_id(0)
    t = MOE_TILE

    @pl.when(first_g[g] == 1)
    def _():
        o_ref[...] = jnp.zeros_like(o_ref)
        g_ref[...] = jnp.zeros_like(g_ref)

    nvalid = valid_s[g * k_]
    for k in range(1, k_):
        nvalid = nvalid + valid_s[g * k_ + k]

    @pl.when(nvalid > 0)
    def _():
        rank = lax.broadcasted_iota(I32, (t, GATHER_TOK), 0) + slab_g[g] * t
        got = o_ref[0].astype(F32)
        gate = g_ref[0]
        per_dot = MXU_DIM // GATHER_TOK
        for k0 in range(0, k_, per_dot):
            onehots, xs = [], []
            for k in range(k0, k0 + per_dot):
                want = jnp.where(valid_s[g * k_ + k] == 1, rank, -2)
                hit = pos_refs[k][0] == want
                onehots.append(jnp.where(hit, 1.0, 0.0).astype(BF16))
                xs.append(x_refs[k][...])
                gate = gate + jnp.sum(jnp.where(hit, aff_refs[k][0], 0.0), axis=1, keepdims=True)
            got = got + jnp.dot(jnp.concatenate(onehots, axis=1), jnp.concatenate(xs, axis=0),
                                preferred_element_type=F32)
        o_ref[0] = got.astype(o_ref.dtype)
        g_ref[0] = gate


def _moe_gather(x_bf, pos_en, aff_en, items, cap):
    n = x_bf.shape[0]
    t = MOE_TILE
    k_ = GATHER_GROUP
    ng = items[2].shape[0]
    tk = GATHER_TOK
    xs = [pl.BlockSpec((tk, D_MODEL), lambda g, tl, va, e, s, fi, k=k: (tl[g * k_ + k], 0)) for k in range(k_)]
    rows = [pl.BlockSpec((1, 1, tk), lambda g, tl, va, e, s, fi, k=k: (e[g], 0, tl[g * k_ + k])) for k in range(k_)]
    grid_spec = pltpu.PrefetchScalarGridSpec(
        num_scalar_prefetch=5,
        grid=(ng,),
        in_specs=xs + rows + rows,
        out_specs=[pl.BlockSpec((1, t, D_MODEL), lambda g, tl, va, e, s, fi: (e[g], s[g], 0)),
                   pl.BlockSpec((1, t, 1), lambda g, tl, va, e, s, fi: (e[g], s[g], 0))],
    )
    pos3 = pos_en.reshape(N_EXPERTS, 1, n)
    aff3 = aff_en.reshape(N_EXPERTS, 1, n)
    return pl.pallas_call(
        _gather_kernel,
        grid_spec=grid_spec,
        out_shape=[jax.ShapeDtypeStruct((N_EXPERTS, cap, D_MODEL), BF16),
                   jax.ShapeDtypeStruct((N_EXPERTS, cap, 1), F32)],
        compiler_params=_cparams(("arbitrary",)),
        name="moe_gather",
    )(*items, *([x_bf] * k_), *([pos3] * k_), *([aff3] * k_))


FFN_TM = 512


def _ffn_kernel(x_ref, gate_ref, wg_ref, wu_ref, wd_ref, o_ref):
    x = x_ref[0]
    g = jnp.dot(x, wg_ref[0], preferred_element_type=F32)
    u = jnp.dot(x, wu_ref[0], preferred_element_type=F32)
    h = (g * _sigmoid(g)) * u
    y = jnp.dot(h.astype(BF16), wd_ref[0], preferred_element_type=F32) * gate_ref[0]
    o_ref[0] = y.astype(o_ref.dtype)


def _expert_ffn(xe, gate, wg, wu, wd, layer):
    e_, cap, _ = xe.shape
    tm = min(FFN_TM, cap)
    wspec = lambda a: pl.BlockSpec((None, 1) + a.shape[2:], lambda e, i: (layer, e, 0, 0))
    rows = lambda w: pl.BlockSpec((1, tm, w), lambda e, i: (e, i, 0))
    sds = jax.ShapeDtypeStruct((e_, cap, D_MODEL), BF16)
    return pl.pallas_call(
        _ffn_kernel,
        grid=(e_, cap // tm),
        in_specs=[rows(D_MODEL), rows(1), wspec(wg), wspec(wu), wspec(wd)],
        out_specs=rows(D_MODEL),
        out_shape=sds,
        compiler_params=_cparams(("parallel", "arbitrary")),
        name="expert_ffn",
    )(xe, gate, wg, wu, wd)


def _combine_kernel(e_s, slab_s, valid_s, tile_g, first_g, last_g, *refs):
    k_ = COMBINE_GROUP
    ye_refs = refs[:k_]
    pos_ref, x_ref, lnv_ref, o_ref, ob_ref, acc_ref = refs[k_:]
    g = pl.program_id(0)
    t = MOE_TILE

    @pl.when(first_g[g] == 1)
    def _():
        acc_ref[...] = jnp.zeros_like(acc_ref)

    nvalid = valid_s[g * k_]
    for k in range(1, k_):
        nvalid = nvalid + valid_s[g * k_ + k]

    @pl.when(nvalid > 0)
    def _():
        w = k_ * MOE_SUB
        p_hi, p_lo = _split2(pos_ref[...].astype(F32))
        expert = lax.broadcasted_iota(I32, (N_EXPERTS, w), 0)
        col_e = lax.broadcasted_iota(I32, (N_EXPERTS, w), 1)
        col = lax.broadcasted_iota(I32, (1, w), 1)
        pick = jnp.zeros((N_EXPERTS, w), F32)
        want = jnp.zeros((1, w), I32)
        for k in range(k_):
            s = g * k_ + k
            in_slot = (col >= k * MOE_SUB) & (col < (k + 1) * MOE_SUB)
            in_slot_e = (col_e >= k * MOE_SUB) & (col_e < (k + 1) * MOE_SUB)
            pick = jnp.where(in_slot_e & (expert == e_s[s]), 1.0, pick)
            base = jnp.where(valid_s[s] == 1, slab_s[s] * MOE_SUB, -2 * MOE_SUB)
            want = jnp.where(in_slot, col - k * MOE_SUB + base, want)
        pick = pick.astype(BF16)
        pos = jnp.dot(p_hi, pick, preferred_element_type=F32) + jnp.dot(p_lo, pick, preferred_element_type=F32)
        onehot = jnp.where(pos == want.astype(F32), 1.0, 0.0).astype(BF16)
        ye = jnp.concatenate([r[0] for r in ye_refs], axis=0)
        acc_ref[...] += jnp.dot(onehot, ye, preferred_element_type=F32)

    @pl.when(last_g[g] == 1)
    def _():
        y = _layer_norm_rows(DEEPNORM_ALPHA * x_ref[...] + acc_ref[...], lnv_ref[0:1, :], lnv_ref[1:2, :])
        o_ref[...] = y
        ob_ref[...] = y.astype(BF16)


def _moe_combine(ye, pos_ne, x, lnvec, items):
    n = x.shape[0]
    t = MOE_TILE
    k_ = COMBINE_GROUP
    ng = items[3].shape[0]
    tile_blk = lambda w: pl.BlockSpec((t, w), lambda g, e, s, va, tl, fi, la: (tl[g], 0))
    slabs = [pl.BlockSpec((1, MOE_SUB, D_MODEL), lambda g, e, s, va, tl, fi, la, k=k: (e[g * k_ + k], s[g * k_ + k], 0))
             for k in range(k_)]
    grid_spec = pltpu.PrefetchScalarGridSpec(
        num_scalar_prefetch=6,
        grid=(ng,),
        in_specs=slabs + [tile_blk(N_EXPERTS), tile_blk(D_MODEL),
                          pl.BlockSpec(lnvec.shape, lambda g, e, s, va, tl, fi, la: (0, 0))],
        out_specs=[tile_blk(D_MODEL), tile_blk(D_MODEL)],
        scratch_shapes=[pltpu.VMEM((t, D_MODEL), F32)],
    )
    return pl.pallas_call(
        _combine_kernel,
        grid_spec=grid_spec,
        out_shape=[jax.ShapeDtypeStruct((n, D_MODEL), F32), jax.ShapeDtypeStruct((n, D_MODEL), BF16)],
        compiler_params=_cparams(("arbitrary",)),
        name="moe_combine",
    )(*items, *([ye] * k_), pos_ne, x, lnvec)


def _moe_items(group_off, n, cap):
    e_ = N_EXPERTS
    big = jnp.int32(2 ** 30)

    def build(rows, by_tile, k_, t):
        ns = cap // rows
        nt = n // t
        starts = group_off[:, ::t // LANES, 0]
        ends = jnp.concatenate([starts[:, 1:], jnp.full((e_, 1), cap, I32)], axis=1)
        e_p = jnp.broadcast_to(jnp.arange(e_, dtype=I32)[:, None], (e_, nt))
        tile_p = jnp.broadcast_to(jnp.arange(nt, dtype=I32)[None, :], (e_, nt))
        slab_p = jnp.minimum(starts // rows, ns - 1)
        valid_p = (ends > starts).astype(I32)
        bound = jnp.arange(ns, dtype=I32) * rows
        tile_s = jnp.sum((starts[:, :, None] <= bound[None, None, :]).astype(I32), axis=1) - 1
        tile_s = jnp.clip(tile_s, 0, nt - 1)
        at_tile = tile_s[:, :, None] == jnp.arange(nt, dtype=I32)[None, None, :]
        st_s = jnp.sum(jnp.where(at_tile, starts[:, None, :], 0), axis=2)
        en_s = jnp.sum(jnp.where(at_tile, ends[:, None, :], 0), axis=2)
        valid_s = ((st_s < bound[None, :]) & (bound[None, :] < en_s)).astype(I32)
        e_s = jnp.broadcast_to(jnp.arange(e_, dtype=I32)[:, None], (e_, ns))
        slab_s = jnp.broadcast_to(jnp.arange(ns, dtype=I32)[None, :], (e_, ns))
        cat = lambda a, b: jnp.concatenate([a.reshape(-1), b.reshape(-1)])
        e_a, slab_a, tile_a = cat(e_p, e_s), cat(slab_p, slab_s), cat(tile_p, tile_s)
        valid_a = cat(valid_p, valid_s)
        secondary = cat(jnp.zeros_like(e_p), jnp.ones_like(e_s))
        ni = e_a.shape[0]
        ar = jnp.arange(ni, dtype=I32)
        if by_tile:
            keep = jnp.maximum(valid_a, 1 - secondary)
            key = ((tile_a * e_ + e_a) * 2 + secondary) * ns + slab_a
        else:
            keep = valid_a
            key = (e_a * ns + slab_a) * nt + tile_a
        key = jnp.where(keep == 1, key, big + ar)
        rank = jnp.sum((key[None, :] < key[:, None]).astype(I32), axis=1)
        code = ((e_a * ns + slab_a) * nt + tile_a) * 2 + valid_a
        decode = lambda cd: (cd // (2 * nt * ns), (cd // (2 * nt)) % ns, (cd // 2) % nt, cd % 2)
        ocode = jnp.sum(jnp.where(rank[None, :] == ar[:, None], code[None, :], 0), axis=1)
        nkeep = jnp.sum(keep)
        kept = ar < nkeep
        oe, os_, otl, _ = decode(ocode)
        blk = otl if by_tile else oe * ns + os_

        nblk = nt if by_tile else e_ * ns
        ng = -(-ni // k_) + nblk
        first = jnp.concatenate([jnp.ones((1,), bool), blk[1:] != blk[:-1]])
        run_start = lax.cummax(jnp.where(first, ar, 0))
        sub = (ar - run_start) % k_
        leads = (sub == 0) & kept
        gid = jnp.cumsum(leads.astype(I32)) - 1
        slot = jnp.where(kept, gid * k_ + sub, -1)
        sar = jnp.arange(ng * k_, dtype=I32)
        hit = slot[None, :] == sar[:, None]
        has = jnp.sum(hit.astype(I32), axis=1)
        scode = jnp.sum(jnp.where(hit, ocode[None, :], 0), axis=1)
        n_real = jnp.sum(leads.astype(I32))
        gar = jnp.arange(ng, dtype=I32)
        lead = scode.reshape(ng, k_)[:, 0]
        lead = jnp.where(gar < n_real, lead, jnp.sum(jnp.where(gar == n_real - 1, lead, 0)))
        scode = jnp.where(has == 1, scode, jnp.repeat(lead, k_))
        se, ss, stl, sv = decode(scode)
        ge, gsl, gtl, _ = decode(lead)
        gblk = gtl if by_tile else ge * ns + gsl
        change = (gblk[1:] != gblk[:-1]).astype(I32)
        one = jnp.ones((1,), I32)
        slots = dict(e=se, slab=ss, tile=stl, valid=sv * has)
        groups = dict(e=ge, slab=gsl, tile=gtl, first=jnp.concatenate([one, change]),
                      last=jnp.concatenate([change, one]))
        return slots, groups

    gs, gg = build(MOE_TILE, False, GATHER_GROUP, GATHER_TOK)
    cs, cg = build(MOE_SUB, True, COMBINE_GROUP, MOE_TILE)
    return ((gs["tile"], gs["valid"], gg["e"], gg["slab"], gg["first"]),
            (cs["e"], cs["slab"], cs["valid"], cg["tile"], cg["first"], cg["last"]))


def _expert_choice_moe_ln(x, x_bf, w_router_t, wg, wu, wd, layer, lnvec):
    n = x.shape[0]
    cap = EC_CAPACITY_FACTOR * n // N_EXPERTS
    aff_en = _router(x, w_router_t)
    pos3, off3 = _select(aff_en.reshape(N_EXPERTS, n // LANES, LANES), cap)
    pos_en = pos3.reshape(N_EXPERTS, n)
    g_items, c_items = _moe_items(off3, n, cap)
    xe, gate = _moe_gather(x_bf, pos_en, aff_en, g_items, cap)
    ye = _expert_ffn(xe, gate, wg, wu, wd, layer)
    return _moe_combine(ye, pos_en.T, x, lnvec, c_items)


def _pack_cols(w):
    a_end = A_COLS
    b_end = a_end + B_COLS
    c_end = b_end + C_COLS
    pad = jnp.zeros(w.shape[:-1] + (LORA_PAD - LORA_COLS,), w.dtype)
    return (jnp.concatenate([w[..., :3 * A_WIDTH], w[..., a_end:b_end], w[..., 3 * A_WIDTH:a_end], pad], axis=-1),
            jnp.concatenate([w[..., c_end:], w[..., b_end:c_end]], axis=-1))


def _lora_rows(w, start):
    k = w.shape[-2]
    return jnp.pad(w, [(0, 0)] * (w.ndim - 2) + [(start, LORA_PAD - start - k), (0, 0)])


def _prepare(w_in, mu_prev, mu_next, decay_w0, decay_w2, iclr_a0, iclr_a2, gate_g2, k_k, k_a, r_k, gn_g, gn_b,
             sg_ln_g, sg_ln_b, sg_w, sg_b, rpb, p_a, p_b, p_c, w_out, ln_mix_g, ln_mix_b, w_router, e_gate, e_up,
             e_down, ln_ffn_g, ln_ffn_b, rows):
    l_ = w_in.shape[0]
    pad_a = lambda m: jnp.pad(m, ((0, 0), (0, LORA_PAD - LORA_COLS)))
    mup, mun = mu_prev, mu_next
    vec_rows = [mup[:, :A_WIDTH], mup[:, A_WIDTH:2 * A_WIDTH], mup[:, 2 * A_WIDTH:3 * A_WIDTH],
                mun[:, :A_WIDTH], mun[:, A_WIDTH:2 * A_WIDTH], mun[:, 2 * A_WIDTH:3 * A_WIDTH],
                k_k, k_a, r_k.reshape(l_, A_WIDTH), decay_w0[:, 0], decay_w0[:, 1], iclr_a0[:, 0], iclr_a0[:, 1]]
    vec = jnp.stack(vec_rows + [jnp.zeros_like(k_k)] * (16 - len(vec_rows)), axis=1)
    lvec = jnp.stack([pad_a(mup[:, 3 * A_WIDTH:]), pad_a(mun[:, 3 * A_WIDTH:])]
                     + [jnp.zeros((l_, LORA_PAD), F32)] * 6, axis=1)
    w2f = jnp.stack([_lora_rows(decay_w2[:, 0], 0), _lora_rows(decay_w2[:, 1], A_DECAY_LORA)], axis=1)
    a2f = jnp.stack([_lora_rows(iclr_a2[:, 0], 2 * A_DECAY_LORA),
                     _lora_rows(iclr_a2[:, 1], 2 * A_DECAY_LORA + A_ICLR_LORA)], axis=1)
    g2f = _lora_rows(gate_g2, 2 * A_DECAY_LORA + 2 * A_ICLR_LORA)
    head = np.arange(A_WIDTH) // A_HEAD_DIM
    hsum = jnp.asarray(head[:, None] == np.arange(LANES)[None, :], BF16)
    sgw2 = sg_w.reshape(l_, B_GROUPS // 2, 2, B_CHUNK, B_CHUNK).transpose(0, 1, 3, 2, 4)
    sgw2 = sgw2.reshape(l_, B_GROUPS // 2, B_CHUNK, 2 * B_CHUNK)
    sg_bias = jnp.repeat(jnp.swapaxes(sg_b, 1, 2), B_GROUP_DIM, axis=2)
    return dict(
        w_in=_pack_cols(w_in.astype(BF16)), vec=vec, lvec=lvec, w2f=w2f.astype(BF16), a2f=a2f.astype(BF16),
        g2f=g2f.astype(BF16), hsum=hsum, hexp=hsum.T, gnvec=jnp.stack([gn_g, gn_b], axis=1),
        sg_ln=jnp.stack([sg_ln_g, sg_ln_b], axis=1), sgw2=sgw2.astype(BF16), sg_bias=sg_bias,
        na_bias=jnp.stack([_na_bias_table(rpb[l], rows) for l in range(l_)]),
        p_a=p_a.astype(BF16), p_b=p_b.astype(BF16), p_c=p_c.astype(BF16), w_out=w_out.astype(BF16),
        ln_mix=jnp.stack([ln_mix_g, ln_mix_b], axis=1), w_router_t=jnp.swapaxes(w_router, 1, 2),
        e_gate=e_gate.astype(BF16), e_up=e_up.astype(BF16), e_down=e_down.astype(BF16),
        ln_ffn=jnp.stack([ln_ffn_g, ln_ffn_b], axis=1))


def _mixer(x, x_bf, p, l, b, t):
    n = b * t
    w_f, w_h = p["w_in"]
    proj = _matmul(x_bf, w_f, l, 1024, IN_TN)
    proj_h = _matmul(x_bf, w_h, l, 1024, IN_TN, BF16)
    proj3 = proj.reshape(b, t, F_COLS)
    proj3_h = proj_h.reshape(b, t, H_COLS)
    r, v, kk, lwf, kf, bf, lwb, kb, bb, bonus, g = _rwkv_prep(
        proj3, p["vec"][l], p["lvec"][l], p["w2f"][l], p["a2f"][l], p["g2f"][l], p["hsum"], p["hexp"])
    yf, yb = _rwkv_scan(r, v, kk, lwf, kf, bf, lwb, kb, bb)
    flat = lambda a: a.reshape(n, a.shape[-1])
    ya = _rwkv_post(flat(yf), flat(yb), flat(bonus), flat(g), p["gnvec"][l], p["hsum"], p["hexp"])
    ybr = _spatial_gating(proj3, p["sg_ln"][l], p["sgw2"][l], p["sg_bias"][l])
    ycr = _neighbourhood_attention(proj3_h, p["na_bias"][l])
    m = _merge(ya, flat(ybr), flat(ycr), proj_h, p["p_a"][l], p["p_b"][l], p["p_c"][l])
    return _out_proj_ln(m, p["w_out"][l], x, p["ln_mix"][l])


def _trunk(x3, p):
    b, t, _ = x3.shape
    x = x3.reshape(b * t, D_MODEL)
    x_bf = x.astype(BF16)
    for l in range(DEPTH):
        x, x_bf = _mixer(x, x_bf, p, l, b, t)
        x, x_bf = _expert_choice_moe_ln(x, x_bf, p["w_router_t"][l], p["e_gate"], p["e_up"], p["e_down"], l,
                                        p["ln_ffn"][l])
    return x.reshape(b, t, D_MODEL)


def kernel(x_prompt, x_sample, w_in, mu_prev, mu_next, decay_w0, decay_w2, iclr_a0, iclr_a2, gate_g2, k_k, k_a, r_k, gn_g, gn_b, sg_ln_g, sg_ln_b, sg_w, sg_b, rpb, p_a, p_b, p_c, w_out, ln_mix_g, ln_mix_b, w_router, e_gate, e_up, e_down, ln_ffn_g, ln_ffn_b):
    assert x_prompt.shape[1] == x_sample.shape[1]
    rows = x_prompt.shape[1] // GRID_W
    p = _prepare(w_in, mu_prev, mu_next, decay_w0, decay_w2, iclr_a0, iclr_a2, gate_g2, k_k, k_a, r_k, gn_g, gn_b,
                 sg_ln_g, sg_ln_b, sg_w, sg_b, rpb, p_a, p_b, p_c, w_out, ln_mix_g, ln_mix_b, w_router, e_gate,
                 e_up, e_down, ln_ffn_g, ln_ffn_b, rows)
    return (_trunk(x_prompt, p), _trunk(x_sample, p))
```

```python
import functools
from typing import NamedTuple

import numpy as np
import jax
import jax.numpy as jnp
from jax import lax
from jax.experimental import pallas as pl
from jax.experimental.pallas import tpu as pltpu

F32 = jnp.float32
BF16 = jnp.bfloat16
I32 = jnp.int32

D_MODEL = 2048
DEPTH = 4
GRID_W = 64
A_HEADS = 16
A_HEAD_DIM = 64
A_WIDTH = A_HEADS * A_HEAD_DIM
A_DECAY_LORA = 64
A_ICLR_LORA = 64
A_GATE_LORA = 160
B_GROUPS = 8
B_GROUP_DIM = 64
B_WIDTH = B_GROUPS * B_GROUP_DIM
B_CHUNK = 128
C_HEADS = 8
C_HEAD_DIM = 64
C_WIDTH = C_HEADS * C_HEAD_DIM
C_WIN_ROWS = 8
C_WIN_COLS = 16
N_EXPERTS = 16
EXPERT_HIDDEN = 1024
EC_CAPACITY_FACTOR = 2
A_COLS = 3 * A_WIDTH + 2 * A_DECAY_LORA + 2 * A_ICLR_LORA + A_GATE_LORA
B_COLS = 2 * B_WIDTH
C_COLS = 3 * C_WIDTH
G_COLS = 3 * D_MODEL
DEEPNORM_ALPHA = (2 * DEPTH) ** 0.25
LN_EPS = 1e-5
GN_EPS = 64e-5

LORA_COLS = 2 * A_DECAY_LORA + 2 * A_ICLR_LORA + A_GATE_LORA
LORA_PAD = 512
COL_R, COL_K, COL_V = 0, A_WIDTH, 2 * A_WIDTH
COL_B = 3 * A_WIDTH
COL_L = COL_B + B_COLS
F_COLS = COL_L + LORA_PAD
COL_G = 0
COL_C = G_COLS
H_COLS = COL_C + C_COLS
IN_TN = 1536

LANES = 128
MXU_DIM = 256
SCAN_CHUNK = 64
MOE_TILE = 256
MOE_SUB = 64
GATHER_TOK = 256
GATHER_GROUP = 4
COMBINE_GROUP = 8
VMEM_LIMIT = 48 * 1024 * 1024


def _cparams(sem):
    return pltpu.CompilerParams(dimension_semantics=sem, vmem_limit_bytes=VMEM_LIMIT)


def _dot(a, b):
    return jnp.dot(a.astype(BF16), b.astype(BF16), preferred_element_type=F32)


def _split2(x):
    hi = x.astype(BF16)
    lo = (x - hi.astype(F32)).astype(BF16)
    return hi, lo


def _split3(x):
    hi = x.astype(BF16)
    r1 = x - hi.astype(F32)
    mid = r1.astype(BF16)
    lo = (r1 - mid.astype(F32)).astype(BF16)
    return hi, mid, lo


def _dot_x01(x, m01):
    hi, lo = _split2(x)
    m = m01.astype(BF16)
    return jnp.dot(hi, m, preferred_element_type=F32) + jnp.dot(lo, m, preferred_element_type=F32)


def _head_sums(x, hsum, hexp):
    return _dot_x01(_dot_x01(x, hsum), hexp)


def _sigmoid(x):
    return 1.0 / (1.0 + jnp.exp(-x))


def _layer_norm_rows(z, g, b):
    mu = jnp.mean(z, axis=-1, keepdims=True)
    d = z - mu
    var = jnp.mean(d * d, axis=-1, keepdims=True)
    return d * lax.rsqrt(var + LN_EPS) * g + b


def _mm_kernel(a_ref, w_ref, o_ref):
    o_ref[...] = jnp.dot(a_ref[...], w_ref[...], preferred_element_type=F32).astype(o_ref.dtype)


def _matmul(a, w, layer, tm, tn, out_dtype=F32):
    m, k = a.shape
    n = w.shape[2]
    tm = min(tm, m)
    return pl.pallas_call(
        _mm_kernel,
        grid=(n // tn, m // tm),
        in_specs=[pl.BlockSpec((tm, k), lambda j, i: (i, 0)),
                  pl.BlockSpec((None, k, tn), lambda j, i: (layer, 0, j))],
        out_specs=pl.BlockSpec((tm, tn), lambda j, i: (i, j)),
        out_shape=jax.ShapeDtypeStruct((m, n), out_dtype),
        compiler_params=_cparams(("parallel", "parallel")),
        name="in_proj_" + jnp.dtype(out_dtype).name,
    )(a, w)


PREP_TT = 256
V_MUP_R, V_MUP_K, V_MUP_V, V_MUN_R, V_MUN_K, V_MUN_V, V_KK, V_KA, V_RK, V_W0F, V_W0B, V_A0F, V_A0B = range(13)


def _shift(cur, prv8, nxt8, mup, mun, i, nt):
    tt = cur.shape[0]
    row = lax.broadcasted_iota(I32, cur.shape, 0)
    first = jnp.where(i > 0, prv8[7:8, :], 0.0)
    last = jnp.where(i < nt - 1, nxt8[0:1, :], 0.0)
    prev = jnp.where(row == 0, first, pltpu.roll(cur, 1, 0))
    nxt = jnp.where(row == tt - 1, last, pltpu.roll(cur, tt - 1, 0))
    return cur * (1.0 - mup - mun) + mup * prev + mun * nxt


def _prep_kernel(rc, kc, vc, lc, rp, kp, vp, lp, rn, kn, vn, ln_, vec_ref, lvec_ref, w2_ref, a2_ref, g2_ref,
                 hs_ref, he_ref, r_o, v_o, kk_o, lwf_o, kf_o, bf_o, lwb_o, kb_o, bb_o, bonus_o, g_o):
    i = pl.program_id(1)
    nt = pl.num_programs(1)
    vec = vec_ref[...]
    row = lambda j: vec[j:j + 1, :]
    r = _shift(rc[0], rp[0], rn[0], row(V_MUP_R), row(V_MUN_R), i, nt)
    k = _shift(kc[0], kp[0], kn[0], row(V_MUP_K), row(V_MUN_K), i, nt)
    v = _shift(vc[0], vp[0], vn[0], row(V_MUP_V), row(V_MUN_V), i, nt)
    lo = _shift(lc[0], lp[0], ln_[0], lvec_ref[0:1, :], lvec_ref[1:2, :], i, nt)

    hsum, hexp = hs_ref[...], he_ref[...]
    kk = k * row(V_KK)
    kk = kk / jnp.maximum(jnp.sqrt(_head_sums(kk * kk, hsum, hexp)), 1e-12)

    tanh_lo = jnp.tanh(lo)
    ksum = jnp.zeros_like(k)
    outs = ((lwf_o, kf_o, bf_o), (lwb_o, kb_o, bb_o))
    for d in range(2):
        z = row(V_W0F + d) + _dot(tanh_lo, w2_ref[d])
        nz = -z
        softplus = jnp.maximum(nz, 0.0) + jnp.log(1.0 + jnp.exp(-jnp.abs(nz)))
        w_log = -softplus - 0.5
        lw = -jnp.exp(w_log)
        a = _sigmoid(row(V_A0F + d) + _dot(lo, a2_ref[d]))
        k_d = k * (1.0 + (a - 1.0) * row(V_KA))
        ksum = ksum + k_d
        lw_o, kd_o, bd_o = outs[d]
        lw_o[0] = lw
        kd_o[0] = k_d
        bd_o[0] = kk * a
    r_o[0] = r
    v_o[0] = v
    kk_o[0] = kk
    bonus_o[0] = _head_sums(r * ksum * row(V_RK), hsum, hexp) * v
    g_o[0] = _dot(_sigmoid(lo), g2_ref[...])


def _rwkv_prep(proj3, vec, lvec, w2f, a2f, g2f, hsum, hexp):
    b, t, _ = proj3.shape
    tt = PREP_TT
    nt = t // tt
    h8 = tt // 8
    cur = lambda cb, w: pl.BlockSpec((1, tt, w), lambda bi, i: (bi, i, cb))
    prv = lambda cb, w: pl.BlockSpec((1, 8, w), lambda bi, i: (bi, jnp.maximum(i * h8 - 1, 0), cb))
    nxt = lambda cb, w: pl.BlockSpec((1, 8, w), lambda bi, i: (bi, jnp.minimum((i + 1) * h8, t // 8 - 1), cb))
    cols = [(COL_R // A_WIDTH, A_WIDTH), (COL_K // A_WIDTH, A_WIDTH), (COL_V // A_WIDTH, A_WIDTH),
            (COL_L // LORA_PAD, LORA_PAD)]
    full = lambda shape: pl.BlockSpec(shape, lambda bi, i: (0,) * len(shape))
    in_specs = ([cur(*c) for c in cols] + [prv(*c) for c in cols] + [nxt(*c) for c in cols]
                + [full(vec.shape), full(lvec.shape), full(w2f.shape), full(a2f.shape), full(g2f.shape),
                   full(hsum.shape), full(hexp.shape)])
    out_spec = pl.BlockSpec((1, tt, A_WIDTH), lambda bi, i: (bi, i, 0))
    out_sds = jax.ShapeDtypeStruct((b, t, A_WIDTH), F32)
    return pl.pallas_call(
        _prep_kernel,
        grid=(b, nt),
        in_specs=in_specs,
        out_specs=[out_spec] * 11,
        out_shape=[out_sds] * 11,
        compiler_params=_cparams(("parallel", "parallel")),
        name="rwkv_prep",
    )(*([proj3] * 12), vec, lvec, w2f, a2f, g2f, hsum, hexp)


def _stack2(x, m0):
    return jnp.concatenate([jnp.where(m0, x, 0.0), jnp.where(m0, 0.0, x)], axis=0)


class _Chain(NamedTuple):
    r: jax.Array
    v: jax.Array
    kk: jax.Array
    lw: jax.Array
    kd: jax.Array
    bd: jax.Array
    s_ref: object
    rev: bool


def _scan_masks(c, reverse):
    ti = lax.broadcasted_iota(I32, (c, c), 0)
    si = lax.broadcasted_iota(I32, (c, c), 1)
    tri = jnp.where((si >= ti) if reverse else (si <= ti), 1.0, 0.0).astype(BF16)
    t2 = lax.broadcasted_iota(I32, (c, 2 * c), 0)
    s2 = lax.broadcasted_iota(I32, (c, 2 * c), 1)
    s2 = jnp.where(s2 >= c, s2 - c, s2)
    strict = (s2 > t2) if reverse else (s2 < t2)
    incl = (s2 >= t2) if reverse else (s2 <= t2)
    eye2 = jnp.where(s2 == t2, 1.0, 0.0)
    return tri, strict, incl, eye2


def _scan_chunk(chains):
    c = SCAN_CHUNK
    masks = {rev: _scan_masks(c, rev) for rev in (False, True)}
    m0 = lax.broadcasted_iota(I32, (c, LANES), 1) < A_HEAD_DIM
    half = lax.broadcasted_iota(I32, (c, 2 * c), 1) < c
    vi = lax.broadcasted_iota(I32, (LANES, LANES), 0)
    ki = lax.broadcasted_iota(I32, (LANES, LANES), 1)
    same_head = (vi < A_HEAD_DIM) == (ki < A_HEAD_DIM)
    bf = lambda x: x.astype(BF16)
    mm = lambda a, b: jnp.dot(a, b, preferred_element_type=F32)
    mm_nt = lambda a, b: lax.dot_general(a, b, (((1,), (1,)), ((), ())), preferred_element_type=F32)
    cat = jnp.concatenate

    cum = []
    for ch in chains:
        tri = masks[ch.rev][0]
        parts = mm(tri, cat(_split3(ch.lw), axis=1))
        cum.append(parts[:, :LANES] + parts[:, LANES:2 * LANES] + parts[:, 2 * LANES:])

    ar, bk2, bkw, v2, etot = [], [], [], [], []
    for ch, cm in zip(chains, cum):
        tot = cm[0:1, :] if ch.rev else cm[c - 1:c, :]
        einv = jnp.exp(-cm)
        etail = jnp.exp(tot - cm)
        at = -ch.kk * jnp.exp(cm - ch.lw)
        rt = ch.r * jnp.exp(cm)
        ar.append(bf(cat([at, rt], axis=0)))
        bk2.append(bf(cat([_stack2(ch.bd * einv, m0), _stack2(ch.kd * einv, m0)], axis=0)))
        bkw.append(bf(cat([ch.bd * etail, ch.kd * etail], axis=0)))
        v2.append(bf(_stack2(ch.v, m0)))
        etot.append(jnp.exp(tot))

    s0 = [ch.s_ref[...] for ch in chains]
    both = [mm_nt(a, cat([b, bf(s)], axis=0)) for a, b, s in zip(ar, bk2, s0)]
    mt = [r[:, :4 * c] for r in both]
    ars = [r[:, 4 * c:] for r in both]
    a_ab, a_xk, a_rb = [], [], []
    for ch, m in zip(chains, mt):
        _, strict, incl, _ = masks[ch.rev]
        a_ab.append(jnp.where(strict, m[:c, :2 * c], 0.0))
        a_xk.append(bf(cat([jnp.where(strict, m[:c, 2 * c:], 0.0), jnp.where(incl, m[c:, 2 * c:], 0.0)], axis=0)))
        a_rb.append(bf(jnp.where(incl, m[c:, :2 * c], 0.0)))
    xkv = [mm(a, v) for a, v in zip(a_xk, v2)]

    tm = [masks[ch.rev][3] + a for ch, a in zip(chains, a_ab)]
    pw = [mm(bf(a), _stack2(bf(a), half)) for a in a_ab]
    steps = int(np.log2(c)) - 1
    for k in range(steps):
        blk = [_stack2(bf(x), half) for x in pw]
        if k < steps - 1:
            both = [mm(bf(cat([t, x], axis=0)), d) for t, x, d in zip(tm, pw, blk)]
            tm = [t + r[:c] for t, r in zip(tm, both)]
            pw = [r[c:] for r in both]
        else:
            tm = [t + mm(bf(t), d) for t, d in zip(tm, blk)]

    u = [mm(bf(t), bf(_stack2(a[:c] + k[:c], m0))) for t, a, k in zip(tm, ars, xkv)]
    y = [a[c:] + k[c:] + mm(ab, bf(_stack2(uu, m0))) for a, k, ab, uu in zip(ars, xkv, a_rb, u)]
    upd = [mm(bf(cat([uu, ch.v], axis=0).T), w) for uu, ch, w in zip(u, chains, bkw)]
    for ch, s, e, up in zip(chains, s0, etot, upd):
        ch.s_ref[...] = s * e + jnp.where(same_head, up, 0.0)
    return y


SCAN_SUB = 2
SCAN_HP = 8


def _scan_kernel(rf, vf, kkf, lwf, kf, bf, rb, vb, kkb, lwb, kb, bb, yf_o, yb_o, sf_ref, sb_ref):
    @pl.when(pl.program_id(2) == 0)
    def _():
        sf_ref[...] = jnp.zeros_like(sf_ref)
        sb_ref[...] = jnp.zeros_like(sb_ref)

    c = SCAN_CHUNK
    for sub in range(SCAN_SUB):
        fr = slice(sub * c, (sub + 1) * c)
        br = slice((SCAN_SUB - 1 - sub) * c, (SCAN_SUB - sub) * c)
        chains = []
        for hp in range(SCAN_HP):
            ln = slice(hp * LANES, (hp + 1) * LANES)
            chains.append(_Chain(rf[0, fr, ln], vf[0, fr, ln], kkf[0, fr, ln], lwf[0, fr, ln], kf[0, fr, ln],
                                 bf[0, fr, ln], sf_ref.at[hp], False))
            chains.append(_Chain(rb[0, br, ln], vb[0, br, ln], kkb[0, br, ln], lwb[0, br, ln], kb[0, br, ln],
                                 bb[0, br, ln], sb_ref.at[hp], True))
        y = _scan_chunk(chains)
        for hp in range(SCAN_HP):
            ln = slice(hp * LANES, (hp + 1) * LANES)
            yf_o[0, fr, ln] = y[2 * hp]
            yb_o[0, br, ln] = y[2 * hp + 1]


def _rwkv_scan(r, v, kk, lwf, kf, bf, lwb, kb, bb):
    b, t, _ = r.shape
    c = SCAN_CHUNK * SCAN_SUB
    nc = t // c
    w = SCAN_HP * LANES
    fwd = pl.BlockSpec((1, c, w), lambda bi, hp, ci: (bi, ci, hp))
    bwd = pl.BlockSpec((1, c, w), lambda bi, hp, ci: (bi, nc - 1 - ci, hp))
    out_sds = jax.ShapeDtypeStruct((b, t, A_WIDTH), F32)
    return pl.pallas_call(
        _scan_kernel,
        grid=(b, A_WIDTH // w, nc),
        in_specs=[fwd] * 6 + [bwd] * 6,
        out_specs=[fwd, bwd],
        out_shape=[out_sds, out_sds],
        scratch_shapes=[pltpu.VMEM((SCAN_HP, LANES, LANES), F32), pltpu.VMEM((SCAN_HP, LANES, LANES), F32)],
        compiler_params=_cparams(("parallel", "parallel", "arbitrary")),
        name="rwkv_scan",
    )(r, v, kk, lwf, kf, bf, r, v, kk, lwb, kb, bb)


POST_TT = 256


def _post_kernel(yf, yb, bonus, g, vec_ref, hs_ref, he_ref, o_ref):
    y = yf[...] + yb[...]
    hsum, hexp = hs_ref[...], he_ref[...]
    inv = 1.0 / A_HEAD_DIM
    mu = _head_sums(y, hsum, hexp) * inv
    d = y - mu
    var = _head_sums(d * d, hsum, hexp) * inv
    yn = d * lax.rsqrt(var + GN_EPS) * vec_ref[0:1, :] + vec_ref[1:2, :]
    o_ref[...] = ((yn + bonus[...]) * g[...]).astype(o_ref.dtype)


def _rwkv_post(yf, yb, bonus, g, gnvec, hsum, hexp):
    n = yf.shape[0]
    tt = min(POST_TT, n)
    blk = pl.BlockSpec((tt, A_WIDTH), lambda i: (i, 0))
    full = lambda a: pl.BlockSpec(a.shape, lambda i: (0,) * a.ndim, pipeline_mode=pl.Buffered(1))
    return pl.pallas_call(
        _post_kernel,
        grid=(n // tt,),
        in_specs=[blk, blk, blk, blk, full(gnvec), full(hsum), full(hexp)],
        out_specs=blk,
        out_shape=jax.ShapeDtypeStruct((n, A_WIDTH), BF16),
        compiler_params=_cparams(("parallel",)),
        name="rwkv_post",
    )(yf, yb, bonus, g, gnvec, hsum, hexp)


def _gelu_tanh(x):
    return 0.5 * x * (1.0 + jnp.tanh(np.sqrt(2.0 / np.pi).astype(np.float32) * (x + 0.044715 * (x * x * x))))


def _sg_kernel(pb_ref, lnv_ref, w_ref, bias_ref, o_ref):
    z = _gelu_tanh(pb_ref[0])
    u = z[:, :B_WIDTH]
    v = _layer_norm_rows(z[:, B_WIDTH:], lnv_ref[0:1, :], lnv_ref[1:2, :])
    lane = lax.broadcasted_iota(I32, (B_CHUNK, LANES), 1)
    m0 = lane < B_GROUP_DIM
    parts = []
    for q in range(B_WIDTH // LANES):
        vq = v[:, q * LANES:(q + 1) * LANES]
        parts.append(_dot(w_ref[q], _stack2(vq, m0)))
    mixed = jnp.concatenate(parts, axis=1) + bias_ref[...]
    o_ref[0] = (u * mixed).astype(o_ref.dtype)


def _spatial_gating(proj3, lnvec, w2, bias_full):
    b, t, _ = proj3.shape
    full = lambda a: pl.BlockSpec(a.shape, lambda bi, i: (0,) * a.ndim)
    return pl.pallas_call(
        _sg_kernel,
        grid=(b, t // B_CHUNK),
        in_specs=[pl.BlockSpec((1, B_CHUNK, B_COLS), lambda bi, i: (bi, i, COL_B // B_COLS)),
                  full(lnvec), full(w2), full(bias_full)],
        out_specs=pl.BlockSpec((1, B_CHUNK, B_WIDTH), lambda bi, i: (bi, i, 0)),
        out_shape=jax.ShapeDtypeStruct((b, t, B_WIDTH), BF16),
        compiler_params=_cparams(("parallel", "parallel")),
        name="spatial_gating",
    )(proj3, lnvec, w2, bias_full)


NA_WIN = C_WIN_ROWS * GRID_W


def _na_row_start(i, rows):
    return jnp.clip(i - C_WIN_ROWS // 2, 0, rows - C_WIN_ROWS)


NA_QR = 4


def _na_kernel(q_ref, k_ref, v_ref, *rest, rows):
    bias_refs, o_ref = rest[:NA_QR], rest[NA_QR]
    kh_ref, vh_ref = rest[NA_QR + 1:]
    i = pl.program_id(1)
    scale = C_HEAD_DIM ** -0.5

    @pl.when(i == 0)
    def _():
        for h in range(C_HEADS):
            sl = slice(h * C_HEAD_DIM, (h + 1) * C_HEAD_DIM)
            kh_ref[h] = k_ref[0, :, sl]
            vh_ref[h] = v_ref[0, :, sl]

    qs, ks, vs, bs = [], [], [], []
    for j in range(NA_QR):
        start = pl.multiple_of(_na_row_start(i * NA_QR + j, rows) * GRID_W, GRID_W)
        q = q_ref[0, j * GRID_W:(j + 1) * GRID_W, :]
        for h in range(C_HEADS):
            sl = slice(h * C_HEAD_DIM, (h + 1) * C_HEAD_DIM)
            qs.append(q[:, sl])
            ks.append(kh_ref[h, pl.ds(start, NA_WIN), :])
            vs.append(vh_ref[h, pl.ds(start, NA_WIN), :])
            bs.append(bias_refs[j][0, h])
    nt = lambda a, b: lax.dot_general(a, b, (((1,), (1,)), ((), ())), preferred_element_type=F32)
    ss = [nt(q, k) * scale + bias for q, k, bias in zip(qs, ks, bs)]
    ps = []
    for s in ss:
        e = jnp.exp(s - jnp.max(s, axis=-1, keepdims=True))
        ps.append((e / jnp.sum(e, axis=-1, keepdims=True)).astype(BF16))
    os_ = [jnp.dot(p, v, preferred_element_type=F32) for p, v in zip(ps, vs)]
    for j in range(NA_QR):
        o_ref[0, j * GRID_W:(j + 1) * GRID_W, :] = jnp.concatenate(
            os_[j * C_HEADS:(j + 1) * C_HEADS], axis=1).astype(o_ref.dtype)


def _neighbourhood_attention(proj3, bias_tab):
    b, t, _ = proj3.shape
    rows = t // GRID_W
    cq = COL_C // C_WIDTH
    qr = NA_QR
    seq = lambda cb: pl.BlockSpec((1, t, C_WIDTH), lambda bi, i: (bi, 0, cb))
    bias = lambda j: pl.BlockSpec((1, C_HEADS, GRID_W, NA_WIN),
                                  lambda bi, i: (i * qr + j - _na_row_start(i * qr + j, rows), 0, 0, 0))
    return pl.pallas_call(
        functools.partial(_na_kernel, rows=rows),
        grid=(b, rows // qr),
        in_specs=[pl.BlockSpec((1, qr * GRID_W, C_WIDTH), lambda bi, i: (bi, i, cq)), seq(cq + 1), seq(cq + 2)]
                 + [bias(j) for j in range(qr)],
        out_specs=pl.BlockSpec((1, qr * GRID_W, C_WIDTH), lambda bi, i: (bi, i, 0)),
        out_shape=jax.ShapeDtypeStruct((b, t, C_WIDTH), BF16),
        scratch_shapes=[pltpu.VMEM((C_HEADS, t, C_HEAD_DIM), BF16), pltpu.VMEM((C_HEADS, t, C_HEAD_DIM), BF16)],
        compiler_params=_cparams(("parallel", "arbitrary")),
        name="nbr_attention",
    )(proj3, proj3, proj3, *([bias_tab] * qr))


def _na_bias_table(rpb, rows):
    kc = C_WIN_COLS
    cols = np.arange(GRID_W)
    col_start = np.clip(cols - kc // 2, 0, GRID_W - kc)
    key_col = np.arange(GRID_W)
    in_win = (key_col[None, :] >= col_start[:, None]) & (key_col[None, :] < col_start[:, None] + kc)
    delta = np.arange(C_WIN_ROWS)
    row_off = np.arange(C_WIN_ROWS)[None, :] - delta[:, None] + (C_WIN_ROWS - 1)
    by_row = rpb[:, row_off]
    span = 2 * GRID_W
    lead = GRID_W - C_WIN_COLS
    padded = jnp.pad(by_row, [(0, 0)] * 3 + [(lead, span - lead - (2 * C_WIN_COLS - 1))])
    skew = jnp.tile(padded, GRID_W)[..., :GRID_W * (span - 1)].reshape(by_row.shape[:3] + (GRID_W, span - 1))
    bias = skew[..., GRID_W - 1:]
    bias = jnp.where(in_win, bias, -1e30)
    bias = jnp.transpose(bias, (1, 0, 3, 2, 4)).reshape(C_WIN_ROWS, C_HEADS, GRID_W, NA_WIN)
    return bias.astype(F32)


MERGE_TM = 512


def _merge_kernel(ya, yb, yc, ga, gb, gc, pa, pb, pc, o_ref):
    m = _sigmoid(ga[...].astype(F32)) * jnp.dot(ya[...], pa[...], preferred_element_type=F32)
    m = m + _sigmoid(gb[...].astype(F32)) * jnp.dot(yb[...], pb[...], preferred_element_type=F32)
    m = m + _sigmoid(gc[...].astype(F32)) * jnp.dot(yc[...], pc[...], preferred_element_type=F32)
    o_ref[...] = m.astype(o_ref.dtype)


def _merge(ya, yb, yc, proj, p_a, p_b, p_c):
    n = ya.shape[0]
    tm = min(MERGE_TM, n)
    rowblk = lambda w: pl.BlockSpec((tm, w), lambda i: (i, 0))
    gate = lambda j: pl.BlockSpec((tm, D_MODEL), lambda i: (i, COL_G // D_MODEL + j))
    full = lambda a: pl.BlockSpec(a.shape, lambda i: (0,) * a.ndim, pipeline_mode=pl.Buffered(1))
    return pl.pallas_call(
        _merge_kernel,
        grid=(n // tm,),
        in_specs=[rowblk(A_WIDTH), rowblk(B_WIDTH), rowblk(C_WIDTH), gate(0), gate(1), gate(2),
                  full(p_a), full(p_b), full(p_c)],
        out_specs=rowblk(D_MODEL),
        out_shape=jax.ShapeDtypeStruct((n, D_MODEL), BF16),
        compiler_params=_cparams(("parallel",)),
        name="branch_merge",
    )(ya, yb, yc, proj, proj, proj, p_a, p_b, p_c)


OUT_TM = 512


def _outln_kernel(m_ref, w_ref, x_ref, lnv_ref, o_ref, ob_ref):
    h = jnp.dot(m_ref[...], w_ref[...], preferred_element_type=F32)
    y = _layer_norm_rows(DEEPNORM_ALPHA * x_ref[...] + h, lnv_ref[0:1, :], lnv_ref[1:2, :])
    o_ref[...] = y
    ob_ref[...] = y.astype(BF16)


def _out_proj_ln(m, w_out, x, lnvec):
    n = m.shape[0]
    tm = min(OUT_TM, n)
    blk = pl.BlockSpec((tm, D_MODEL), lambda i: (i, 0))
    full = lambda a: pl.BlockSpec(a.shape, lambda i: (0,) * a.ndim, pipeline_mode=pl.Buffered(1))
    return pl.pallas_call(
        _outln_kernel,
        grid=(n // tm,),
        in_specs=[blk, full(w_out), blk, full(lnvec)],
        out_specs=[blk, blk],
        out_shape=[jax.ShapeDtypeStruct((n, D_MODEL), F32), jax.ShapeDtypeStruct((n, D_MODEL), BF16)],
        compiler_params=_cparams(("parallel",)),
        name="out_proj_ln",
    )(m, w_out, x, lnvec)


ROUTER_TM = 512


def _router_kernel(x_ref, wt_ref, o_ref):
    xh, xl = _split2(x_ref[...])
    wh, wl = _split2(wt_ref[...])
    nt = lambda a, b: lax.dot_general(a, b, (((1,), (1,)), ((), ())), preferred_element_type=F32)
    logits = nt(wh, xh) + nt(wl, xh) + nt(wh, xl)
    logits = logits - jnp.max(logits, axis=0, keepdims=True)
    e = jnp.exp(logits)
    o_ref[...] = e / jnp.sum(e, axis=0, keepdims=True)


def _router(x, w_router_t):
    n = x.shape[0]
    tm = min(ROUTER_TM, n)
    return pl.pallas_call(
        _router_kernel,
        grid=(n // tm,),
        in_specs=[pl.BlockSpec((tm, D_MODEL), lambda i: (i, 0)),
                  pl.BlockSpec(w_router_t.shape, lambda i: (0, 0))],
        out_specs=pl.BlockSpec((N_EXPERTS, tm), lambda i: (0, i)),
        out_shape=jax.ShapeDtypeStruct((N_EXPERTS, n), F32),
        compiler_params=_cparams(("parallel",)),
        name="router",
    )(x, w_router_t)


def _select_kernel(aff_ref, pos_ref, off_ref, *, cap):
    aff = aff_ref[...]
    e_, g_, l_ = aff.shape
    n = g_ * l_
    bits = pltpu.bitcast(aff, I32)

    def count(mask):
        c = jnp.sum(jnp.where(mask, 1.0, 0.0), axis=2, keepdims=True)
        return jnp.sum(c, axis=1, keepdims=True)

    def thr_body(_, carry):
        lo, hi = carry
        mid = lo + (hi - lo + 1) // 2
        ok = count(bits >= mid) >= cap
        return jnp.where(ok, mid, lo), jnp.where(ok, hi, mid - 1)

    lo0 = jnp.zeros((e_, 1, 1), I32)
    hi0 = jnp.full((e_, 1, 1), 0x7F800000, I32)
    thr, _ = lax.fori_loop(0, 32, thr_body, (lo0, hi0))
    gt = bits > thr
    tie = bits == thr
    need = cap - count(gt)
    idx = lax.broadcasted_iota(I32, aff.shape, 1) * l_ + lax.broadcasted_iota(I32, aff.shape, 2)

    def idx_body(_, carry):
        lo, hi = carry
        mid = (lo + hi) // 2
        ok = count(tie & (idx <= mid)) >= need
        return jnp.where(ok, lo, mid + 1), jnp.where(ok, mid, hi)

    lo1 = jnp.zeros((e_, 1, 1), I32)
    hi1 = jnp.full((e_, 1, 1), n - 1, I32)
    cut, _ = lax.fori_loop(0, int(np.ceil(np.log2(n))) + 1, idx_body, (lo1, hi1))
    sel = jnp.where(gt | (tie & (idx <= cut)), 1.0, 0.0)

    sel2 = sel.reshape(e_ * g_, l_)
    ls = lax.broadcasted_iota(I32, (l_, l_), 0)
    lt = lax.broadcasted_iota(I32, (l_, l_), 1)
    incl = jnp.dot(sel2.astype(BF16), jnp.where(ls <= lt, 1.0, 0.0).astype(BF16), preferred_element_type=F32)
    tot = jnp.dot(sel2.astype(BF16), jnp.ones((l_, l_), BF16), preferred_element_type=F32)
    gs = lax.broadcasted_iota(I32, (g_, g_), 0)
    gt_ = lax.broadcasted_iota(I32, (g_, g_), 1)
    lower = jnp.where(gt_ < gs, 1.0, 0.0).astype(BF16)
    offs = [jnp.dot(lower, tot[e * g_:(e + 1) * g_].astype(BF16), preferred_element_type=F32)
            for e in range(e_)]
    off = jnp.concatenate(offs, axis=0)
    pos = incl + off - sel2
    pos_ref[...] = jnp.where(sel2 > 0.5, pos, -1.0).astype(I32).reshape(e_, g_, l_)
    off_ref[...] = off.astype(I32).reshape(e_, g_, l_)


def _select(aff3, cap):
    full = pl.BlockSpec(aff3.shape, lambda i: (0, 0, 0))
    sds = jax.ShapeDtypeStruct(aff3.shape, I32)
    return pl.pallas_call(
        functools.partial(_select_kernel, cap=cap),
        grid=(1,),
        in_specs=[full],
        out_specs=[full, full],
        out_shape=[sds, sds],
        compiler_params=_cparams(("arbitrary",)),
        name="expert_select",
    )(aff3)


def _gather_kernel(tile_s, valid_s, e_g, slab_g, first_g, *refs):
    k_ = GATHER_GROUP
    x_refs, pos_refs, aff_refs = refs[:k_], refs[k_:2 * k_], refs[2 * k_:3 * k_]
    o_ref, g_ref = refs[3 * k_:]
    g = pl.program_id(0)
    t = MOE_TILE

    @pl.when(first_g[g] == 1)
    def _():
        o_ref[...] = jnp.zeros_like(o_ref)
        g_ref[...] = jnp.zeros_like(g_ref)

    nvalid = valid_s[g * k_]
    for k in range(1, k_):
        nvalid = nvalid + valid_s[g * k_ + k]

    @pl.when(nvalid > 0)
    def _():
        rank = lax.broadcasted_iota(I32, (t, GATHER_TOK), 0) + slab_g[g] * t
        got = o_ref[0].astype(F32)
        gate = g_ref[0]
        per_dot = MXU_DIM // GATHER_TOK
        for k0 in range(0, k_, per_dot):
            onehots, xs = [], []
            for k in range(k0, k0 + per_dot):
                want = jnp.where(valid_s[g * k_ + k] == 1, rank, -2)
                hit = pos_refs[k][0] == want
                onehots.append(jnp.where(hit, 1.0, 0.0).astype(BF16))
                xs.append(x_refs[k][...])
                gate = gate + jnp.sum(jnp.where(hit, aff_refs[k][0], 0.0), axis=1, keepdims=True)
            got = got + jnp.dot(jnp.concatenate(onehots, axis=1), jnp.concatenate(xs, axis=0),
                                preferred_element_type=F32)
        o_ref[0] = got.astype(o_ref.dtype)
        g_ref[0] = gate


def _moe_gather(x_bf, pos_en, aff_en, items, cap):
    n = x_bf.shape[0]
    t = MOE_TILE
    k_ = GATHER_GROUP
    ng = items[2].shape[0]
    tk = GATHER_TOK
    xs = [pl.BlockSpec((tk, D_MODEL), lambda g, tl, va, e, s, fi, k=k: (tl[g * k_ + k], 0)) for k in range(k_)]
    rows = [pl.BlockSpec((1, 1, tk), lambda g, tl, va, e, s, fi, k=k: (e[g], 0, tl[g * k_ + k])) for k in range(k_)]
    grid_spec = pltpu.PrefetchScalarGridSpec(
        num_scalar_prefetch=5,
        grid=(ng,),
        in_specs=xs + rows + rows,
        out_specs=[pl.BlockSpec((1, t, D_MODEL), lambda g, tl, va, e, s, fi: (e[g], s[g], 0)),
                   pl.BlockSpec((1, t, 1), lambda g, tl, va, e, s, fi: (e[g], s[g], 0))],
    )
    pos3 = pos_en.reshape(N_EXPERTS, 1, n)
    aff3 = aff_en.reshape(N_EXPERTS, 1, n)
    return pl.pallas_call(
        _gather_kernel,
        grid_spec=grid_spec,
        out_shape=[jax.ShapeDtypeStruct((N_EXPERTS, cap, D_MODEL), BF16),
                   jax.ShapeDtypeStruct((N_EXPERTS, cap, 1), F32)],
        compiler_params=_cparams(("arbitrary",)),
        name="moe_gather",
    )(*items, *([x_bf] * k_), *([pos3] * k_), *([aff3] * k_))


FFN_TM = 512


def _ffn_kernel(x_ref, gate_ref, wg_ref, wu_ref, wd_ref, o_ref):
    x = x_ref[0]
    g = jnp.dot(x, wg_ref[0], preferred_element_type=F32)
    u = jnp.dot(x, wu_ref[0], preferred_element_type=F32)
    h = (g * _sigmoid(g)) * u
    y = jnp.dot(h.astype(BF16), wd_ref[0], preferred_element_type=F32) * gate_ref[0]
    o_ref[0] = y.astype(o_ref.dtype)


def _expert_ffn(xe, gate, wg, wu, wd, layer):
    e_, cap, _ = xe.shape
    tm = min(FFN_TM, cap)
    wspec = lambda a: pl.BlockSpec((None, 1) + a.shape[2:], lambda e, i: (layer, e, 0, 0))
    rows = lambda w: pl.BlockSpec((1, tm, w), lambda e, i: (e, i, 0))
    sds = jax.ShapeDtypeStruct((e_, cap, D_MODEL), BF16)
    return pl.pallas_call(
        _ffn_kernel,
        grid=(e_, cap // tm),
        in_specs=[rows(D_MODEL), rows(1), wspec(wg), wspec(wu), wspec(wd)],
        out_specs=rows(D_MODEL),
        out_shape=sds,
        compiler_params=_cparams(("parallel", "arbitrary")),
        name="expert_ffn",
    )(xe, gate, wg, wu, wd)


def _combine_kernel(e_s, slab_s, valid_s, tile_g, first_g, last_g, *refs):
    k_ = COMBINE_GROUP
    ye_refs = refs[:k_]
    pos_ref, x_ref, lnv_ref, o_ref, ob_ref, acc_ref = refs[k_:]
    g = pl.program_id(0)
    t = MOE_TILE

    @pl.when(first_g[g] == 1)
    def _():
        acc_ref[...] = jnp.zeros_like(acc_ref)

    nvalid = valid_s[g * k_]
    for k in range(1, k_):
        nvalid = nvalid + valid_s[g * k_ + k]

    @pl.when(nvalid > 0)
    def _():
        w = k_ * MOE_SUB
        p_hi, p_lo = _split2(pos_ref[...].astype(F32))
        expert = lax.broadcasted_iota(I32, (N_EXPERTS, w), 0)
        col_e = lax.broadcasted_iota(I32, (N_EXPERTS, w), 1)
        col = lax.broadcasted_iota(I32, (1, w), 1)
        pick = jnp.zeros((N_EXPERTS, w), F32)
        want = jnp.zeros((1, w), I32)
        for k in range(k_):
            s = g * k_ + k
            in_slot = (col >= k * MOE_SUB) & (col < (k + 1) * MOE_SUB)
            in_slot_e = (col_e >= k * MOE_SUB) & (col_e < (k + 1) * MOE_SUB)
            pick = jnp.where(in_slot_e & (expert == e_s[s]), 1.0, pick)
            base = jnp.where(valid_s[s] == 1, slab_s[s] * MOE_SUB, -2 * MOE_SUB)
            want = jnp.where(in_slot, col - k * MOE_SUB + base, want)
        pick = pick.astype(BF16)
        pos = jnp.dot(p_hi, pick, preferred_element_type=F32) + jnp.dot(p_lo, pick, preferred_element_type=F32)
        onehot = jnp.where(pos == want.astype(F32), 1.0, 0.0).astype(BF16)
        ye = jnp.concatenate([r[0] for r in ye_refs], axis=0)
        acc_ref[...] += jnp.dot(onehot, ye, preferred_element_type=F32)

    @pl.when(last_g[g] == 1)
    def _():
        y = _layer_norm_rows(DEEPNORM_ALPHA * x_ref[...] + acc_ref[...], lnv_ref[0:1, :], lnv_ref[1:2, :])
        o_ref[...] = y
        ob_ref[...] = y.astype(BF16)


def _moe_combine(ye, pos_ne, x, lnvec, items):
    n = x.shape[0]
    t = MOE_TILE
    k_ = COMBINE_GROUP
    ng = items[3].shape[0]
    tile_blk = lambda w: pl.BlockSpec((t, w), lambda g, e, s, va, tl, fi, la: (tl[g], 0))
    slabs = [pl.BlockSpec((1, MOE_SUB, D_MODEL), lambda g, e, s, va, tl, fi, la, k=k: (e[g * k_ + k], s[g * k_ + k], 0))
             for k in range(k_)]
    grid_spec = pltpu.PrefetchScalarGridSpec(
        num_scalar_prefetch=6,
        grid=(ng,),
        in_specs=slabs + [tile_blk(N_EXPERTS), tile_blk(D_MODEL),
                          pl.BlockSpec(lnvec.shape, lambda g, e, s, va, tl, fi, la: (0, 0))],
        out_specs=[tile_blk(D_MODEL), tile_blk(D_MODEL)],
        scratch_shapes=[pltpu.VMEM((t, D_MODEL), F32)],
    )
    return pl.pallas_call(
        _combine_kernel,
        grid_spec=grid_spec,
        out_shape=[jax.ShapeDtypeStruct((n, D_MODEL), F32), jax.ShapeDtypeStruct((n, D_MODEL), BF16)],
        compiler_params=_cparams(("arbitrary",)),
        name="moe_combine",
    )(*items, *([ye] * k_), pos_ne, x, lnvec)


def _moe_items(group_off, n, cap):
    e_ = N_EXPERTS
    big = jnp.int32(2 ** 30)

    def build(rows, by_tile, k_, t):
        ns = cap // rows
        nt = n // t
        starts = group_off[:, ::t // LANES, 0]
        ends = jnp.concatenate([starts[:, 1:], jnp.full((e_, 1), cap, I32)], axis=1)
        e_p = jnp.broadcast_to(jnp.arange(e_, dtype=I32)[:, None], (e_, nt))
        tile_p = jnp.broadcast_to(jnp.arange(nt, dtype=I32)[None, :], (e_, nt))
        slab_p = jnp.minimum(starts // rows, ns - 1)
        valid_p = (ends > starts).astype(I32)
        bound = jnp.arange(ns, dtype=I32) * rows
        tile_s = jnp.sum((starts[:, :, None] <= bound[None, None, :]).astype(I32), axis=1) - 1
        tile_s = jnp.clip(tile_s, 0, nt - 1)
        at_tile = tile_s[:, :, None] == jnp.arange(nt, dtype=I32)[None, None, :]
        st_s = jnp.sum(jnp.where(at_tile, starts[:, None, :], 0), axis=2)
        en_s = jnp.sum(jnp.where(at_tile, ends[:, None, :], 0), axis=2)
        valid_s = ((st_s < bound[None, :]) & (bound[None, :] < en_s)).astype(I32)
        e_s = jnp.broadcast_to(jnp.arange(e_, dtype=I32)[:, None], (e_, ns))
        slab_s = jnp.broadcast_to(jnp.arange(ns, dtype=I32)[None, :], (e_, ns))
        cat = lambda a, b: jnp.concatenate([a.reshape(-1), b.reshape(-1)])
        e_a, slab_a, tile_a = cat(e_p, e_s), cat(slab_p, slab_s), cat(tile_p, tile_s)
        valid_a = cat(valid_p, valid_s)
        secondary = cat(jnp.zeros_like(e_p), jnp.ones_like(e_s))
        ni = e_a.shape[0]
        ar = jnp.arange(ni, dtype=I32)
        if by_tile:
            keep = jnp.maximum(valid_a, 1 - secondary)
            key = ((tile_a * e_ + e_a) * 2 + secondary) * ns + slab_a
        else:
            keep = valid_a
            key = (e_a * ns + slab_a) * nt + tile_a
        key = jnp.where(keep == 1, key, big + ar)
        rank = jnp.sum((key[None, :] < key[:, None]).astype(I32), axis=1)
        code = ((e_a * ns + slab_a) * nt + tile_a) * 2 + valid_a
        decode = lambda cd: (cd // (2 * nt * ns), (cd // (2 * nt)) % ns, (cd // 2) % nt, cd % 2)
        ocode = jnp.sum(jnp.where(rank[None, :] == ar[:, None], code[None, :], 0), axis=1)
        nkeep = jnp.sum(keep)
        kept = ar < nkeep
        oe, os_, otl, _ = decode(ocode)
        blk = otl if by_tile else oe * ns + os_

        nblk = nt if by_tile else e_ * ns
        ng = -(-ni // k_) + nblk
        first = jnp.concatenate([jnp.ones((1,), bool), blk[1:] != blk[:-1]])
        run_start = lax.cummax(jnp.where(first, ar, 0))
        sub = (ar - run_start) % k_
        leads = (sub == 0) & kept
        gid = jnp.cumsum(leads.astype(I32)) - 1
        slot = jnp.where(kept, gid * k_ + sub, -1)
        sar = jnp.arange(ng * k_, dtype=I32)
        hit = slot[None, :] == sar[:, None]
        has = jnp.sum(hit.astype(I32), axis=1)
        scode = jnp.sum(jnp.where(hit, ocode[None, :], 0), axis=1)
        n_real = jnp.sum(leads.astype(I32))
        gar = jnp.arange(ng, dtype=I32)
        lead = scode.reshape(ng, k_)[:, 0]
        lead = jnp.where(gar < n_real, lead, jnp.sum(jnp.where(gar == n_real - 1, lead, 0)))
        scode = jnp.where(has == 1, scode, jnp.repeat(lead, k_))
        se, ss, stl, sv = decode(scode)
        ge, gsl, gtl, _ = decode(lead)
        gblk = gtl if by_tile else ge * ns + gsl
        change = (gblk[1:] != gblk[:-1]).astype(I32)
        one = jnp.ones((1,), I32)
        slots = dict(e=se, slab=ss, tile=stl, valid=sv * has)
        groups = dict(e=ge, slab=gsl, tile=gtl, first=jnp.concatenate([one, change]),
                      last=jnp.concatenate([change, one]))
        return slots, groups

    gs, gg = build(MOE_TILE, False, GATHER_GROUP, GATHER_TOK)
    cs, cg = build(MOE_SUB, True, COMBINE_GROUP, MOE_TILE)
    return ((gs["tile"], gs["valid"], gg["e"], gg["slab"], gg["first"]),
            (cs["e"], cs["slab"], cs["valid"], cg["tile"], cg["first"], cg["last"]))


def _expert_choice_moe_ln(x, x_bf, w_router_t, wg, wu, wd, layer, lnvec):
    n = x.shape[0]
    cap = EC_CAPACITY_FACTOR * n // N_EXPERTS
    aff_en = _router(x, w_router_t)
    pos3, off3 = _select(aff_en.reshape(N_EXPERTS, n // LANES, LANES), cap)
    pos_en = pos3.reshape(N_EXPERTS, n)
    g_items, c_items = _moe_items(off3, n, cap)
    xe, gate = _moe_gather(x_bf, pos_en, aff_en, g_items, cap)
    ye = _expert_ffn(xe, gate, wg, wu, wd, layer)
    return _moe_combine(ye, pos_en.T, x, lnvec, c_items)


def _pack_cols(w):
    a_end = A_COLS
    b_end = a_end + B_COLS
    c_end = b_end + C_COLS
    pad = jnp.zeros(w.shape[:-1] + (LORA_PAD - LORA_COLS,), w.dtype)
    return (jnp.concatenate([w[..., :3 * A_WIDTH], w[..., a_end:b_end], w[..., 3 * A_WIDTH:a_end], pad], axis=-1),
            jnp.concatenate([w[..., c_end:], w[..., b_end:c_end]], axis=-1))


def _lora_rows(w, start):
    k = w.shape[-2]
    return jnp.pad(w, [(0, 0)] * (w.ndim - 2) + [(start, LORA_PAD - start - k), (0, 0)])


def _prepare(w_in, mu_prev, mu_next, decay_w0, decay_w2, iclr_a0, iclr_a2, gate_g2, k_k, k_a, r_k, gn_g, gn_b,
             sg_ln_g, sg_ln_b, sg_w, sg_b, rpb, p_a, p_b, p_c, w_out, ln_mix_g, ln_mix_b, w_router, e_gate, e_up,
             e_down, ln_ffn_g, ln_ffn_b, rows):
    l_ = w_in.shape[0]
    pad_a = lambda m: jnp.pad(m, ((0, 0), (0, LORA_PAD - LORA_COLS)))
    mup, mun = mu_prev, mu_next
    vec_rows = [mup[:, :A_WIDTH], mup[:, A_WIDTH:2 * A_WIDTH], mup[:, 2 * A_WIDTH:3 * A_WIDTH],
                mun[:, :A_WIDTH], mun[:, A_WIDTH:2 * A_WIDTH], mun[:, 2 * A_WIDTH:3 * A_WIDTH],
                k_k, k_a, r_k.reshape(l_, A_WIDTH), decay_w0[:, 0], decay_w0[:, 1], iclr_a0[:, 0], iclr_a0[:, 1]]
    vec = jnp.stack(vec_rows + [jnp.zeros_like(k_k)] * (16 - len(vec_rows)), axis=1)
    lvec = jnp.stack([pad_a(mup[:, 3 * A_WIDTH:]), pad_a(mun[:, 3 * A_WIDTH:])]
                     + [jnp.zeros((l_, LORA_PAD), F32)] * 6, axis=1)
    w2f = jnp.stack([_lora_rows(decay_w2[:, 0], 0), _lora_rows(decay_w2[:, 1], A_DECAY_LORA)], axis=1)
    a2f = jnp.stack([_lora_rows(iclr_a2[:, 0], 2 * A_DECAY_LORA),
                     _lora_rows(iclr_a2[:, 1], 2 * A_DECAY_LORA + A_ICLR_LORA)], axis=1)
    g2f = _lora_rows(gate_g2, 2 * A_DECAY_LORA + 2 * A_ICLR_LORA)
    head = np.arange(A_WIDTH) // A_HEAD_DIM
    hsum = jnp.asarray(head[:, None] == np.arange(LANES)[None, :], BF16)
    sgw2 = sg_w.reshape(l_, B_GROUPS // 2, 2, B_CHUNK, B_CHUNK).transpose(0, 1, 3, 2, 4)
    sgw2 = sgw2.reshape(l_, B_GROUPS // 2, B_CHUNK, 2 * B_CHUNK)
    sg_bias = jnp.repeat(jnp.swapaxes(sg_b, 1, 2), B_GROUP_DIM, axis=2)
    return dict(
        w_in=_pack_cols(w_in.astype(BF16)), vec=vec, lvec=lvec, w2f=w2f.astype(BF16), a2f=a2f.astype(BF16),
        g2f=g2f.astype(BF16), hsum=hsum, hexp=hsum.T, gnvec=jnp.stack([gn_g, gn_b], axis=1),
        sg_ln=jnp.stack([sg_ln_g, sg_ln_b], axis=1), sgw2=sgw2.astype(BF16), sg_bias=sg_bias,
        na_bias=jnp.stack([_na_bias_table(rpb[l], rows) for l in range(l_)]),
        p_a=p_a.astype(BF16), p_b=p_b.astype(BF16), p_c=p_c.astype(BF16), w_out=w_out.astype(BF16),
        ln_mix=jnp.stack([ln_mix_g, ln_mix_b], axis=1), w_router_t=jnp.swapaxes(w_router, 1, 2),
        e_gate=e_gate.astype(BF16), e_up=e_up.astype(BF16), e_down=e_down.astype(BF16),
        ln_ffn=jnp.stack([ln_ffn_g, ln_ffn_b], axis=1))


def _mixer(x, x_bf, p, l, b, t):
    n = b * t
    w_f, w_h = p["w_in"]
    proj = _matmul(x_bf, w_f, l, 1024, IN_TN)
    proj_h = _matmul(x_bf, w_h, l, 1024, IN_TN, BF16)
    proj3 = proj.reshape(b, t, F_COLS)
    proj3_h = proj_h.reshape(b, t, H_COLS)
    r, v, kk, lwf, kf, bf, lwb, kb, bb, bonus, g = _rwkv_prep(
        proj3, p["vec"][l], p["lvec"][l], p["w2f"][l], p["a2f"][l], p["g2f"][l], p["hsum"], p["hexp"])
    yf, yb = _rwkv_scan(r, v, kk, lwf, kf, bf, lwb, kb, bb)
    flat = lambda a: a.reshape(n, a.shape[-1])
    ya = _rwkv_post(flat(yf), flat(yb), flat(bonus), flat(g), p["gnvec"][l], p["hsum"], p["hexp"])
    ybr = _spatial_gating(proj3, p["sg_ln"][l], p["sgw2"][l], p["sg_bias"][l])
    ycr = _neighbourhood_attention(proj3_h, p["na_bias"][l])
    m = _merge(ya, flat(ybr), flat(ycr), proj_h, p["p_a"][l], p["p_b"][l], p["p_c"][l])
    return _out_proj_ln(m, p["w_out"][l], x, p["ln_mix"][l])


def _trunk(x3, p):
    b, t, _ = x3.shape
    x = x3.reshape(b * t, D_MODEL)
    x_bf = x.astype(BF16)
    for l in range(DEPTH):
        x, x_bf = _mixer(x, x_bf, p, l, b, t)
        x, x_bf = _expert_choice_moe_ln(x, x_bf, p["w_router_t"][l], p["e_gate"], p["e_up"], p["e_down"], l,
                                        p["ln_ffn"][l])
    return x.reshape(b, t, D_MODEL)


def kernel(x_prompt, x_sample, w_in, mu_prev, mu_next, decay_w0, decay_w2, iclr_a0, iclr_a2, gate_g2, k_k, k_a, r_k, gn_g, gn_b, sg_ln_g, sg_ln_b, sg_w, sg_b, rpb, p_a, p_b, p_c, w_out, ln_mix_g, ln_mix_b, w_router, e_gate, e_up, e_down, ln_ffn_g, ln_ffn_b):
    assert x_prompt.shape[1] == x_sample.shape[1]
    rows = x_prompt.shape[1] // GRID_W
    p = _prepare(w_in, mu_prev, mu_next, decay_w0, decay_w2, iclr_a0, iclr_a2, gate_g2, k_k, k_a, r_k, gn_g, gn_b,
                 sg_ln_g, sg_ln_b, sg_w, sg_b, rpb, p_a, p_b, p_c, w_out, ln_mix_g, ln_mix_b, w_router, e_gate,
                 e_up, e_down, ln_ffn_g, ln_ffn_b, rows)
    return (_trunk(x_prompt, p), _trunk(x_sample, p))
```
